```python
import math, functools
import jax, jax.numpy as jnp
from jax import lax
import numpy as np

D_MODEL = 1024
BATCH = 8
SEQ = 8192
DEPTH = 2

GRID_W = 64
CTX_LEN = 256
ROPE_BASE = 10000.0
Q_BLOCK = 128
NORM_EPS = 1e-6

MLA_HEADS = 8
MLA_NOPE = 64
MLA_ROPE = 32
MLA_V = 64
MLA_Q_RANK = 384
MLA_KV_RANK = 256

GDN_HEADS = 8
GDN_DK = 64
GDN_DV = 64
GDN_CONV = 5
GDN_CHUNK = 64
GDN_QKV = GDN_HEADS * (2 * GDN_DK + GDN_DV)

DIFF_HEADS = 4
DIFF_DH = 64

GQA_HEADS = 8
GQA_KV_HEADS = 2
GQA_DH = 64
GQA_GROUP = GQA_HEADS // GQA_KV_HEADS

MIX_WIDTH = 1024
EVEN_SPLITS = (MLA_Q_RANK, MLA_KV_RANK, MLA_ROPE, GDN_QKV, GDN_HEADS * GDN_DV, 2 * GDN_HEADS, 2 * GDN_HEADS)
ODD_SPLITS = (DIFF_HEADS * 2 * DIFF_DH, DIFF_HEADS * 2 * DIFF_DH, DIFF_HEADS * 2 * DIFF_DH,
              GQA_HEADS * GQA_DH, GQA_KV_HEADS * GQA_DH, GQA_KV_HEADS * GQA_DH)
EVEN_IN = sum(EVEN_SPLITS)
ODD_IN = sum(ODD_SPLITS)

FFN_DIM = 2816
N_EXPERTS = 8
TOP_K = 2
EXPERT_DIM = 3584
MOE_BLOCK = 512

kernel_name = 'hybrid_mla_gdn_diff_gqa_moe_dit'


def rms_norm(x, g):
    xf = x.astype(jnp.float32)
    y = xf * lax.rsqrt(jnp.mean(xf * xf, axis=-1, keepdims=True) + NORM_EPS)
    return (y * g.astype(jnp.float32)).astype(x.dtype)


def l2_normalize(x):
    xf = x.astype(jnp.float32)
    return xf * lax.rsqrt(jnp.sum(xf * xf, axis=-1, keepdims=True) + NORM_EPS)


def split_cols(z, sizes):
    cuts, acc = [], 0
    for s in sizes[:-1]:
        acc += s
        cuts.append(acc)
    return jnp.split(z, cuts, axis=-1)


def modulation(cvec, w, b):
    m = jax.nn.silu(cvec) @ w + b
    m = m.reshape(cvec.shape[:-1] + (6, cvec.shape[-1]))
    return tuple(jnp.expand_dims(m[..., k, :], -2) for k in range(6))


def axial_rope_tables(n_rows, rot_dim):
    n_freq = rot_dim // 4
    inv = 1.0 / (ROPE_BASE ** (jnp.arange(n_freq, dtype=jnp.float32) / n_freq))
    rows = jnp.repeat(jnp.arange(n_rows, dtype=jnp.float32), GRID_W)
    cols = jnp.tile(jnp.arange(GRID_W, dtype=jnp.float32), n_rows)
    ang_r = rows[:, None] * inv
    ang_c = cols[:, None] * inv
    return (jnp.cos(ang_r), jnp.sin(ang_r), jnp.cos(ang_c), jnp.sin(ang_c))


def _rotate_half(x, cos, sin):
    shape = (cos.shape[0],) + (1,) * (x.ndim - 3) + (cos.shape[1],)
    cs = cos.reshape(shape).astype(x.dtype)
    sn = sin.reshape(shape).astype(x.dtype)
    x1, x2 = jnp.split(x, 2, axis=-1)
    return jnp.concatenate([x1 * cs - x2 * sn, x2 * cs + x1 * sn], axis=-1)


def apply_axial_rope(x, tables):
    cos_r, sin_r, cos_c, sin_c = tables
    x_row, x_col = jnp.split(x, 2, axis=-1)
    return jnp.concatenate([_rotate_half(x_row, cos_r, sin_r), _rotate_half(x_col, cos_c, sin_c)], axis=-1)


def depthwise_conv_centred(x, w):
    width, ch = w.shape
    return lax.conv_general_dilated(
        x, w[:, None, :].astype(x.dtype), window_strides=(1,),
        padding=[(width // 2, width // 2)],
        dimension_numbers=('NWC', 'WIO', 'NWC'), feature_group_count=ch)


def sweep_query_blocks(fn, q):
    bsz, tq = q.shape[:2]
    nb = tq // Q_BLOCK
    qb = jnp.swapaxes(q.reshape((bsz, nb, Q_BLOCK) + q.shape[2:]), 0, 1)
    o = lax.map(fn, qb)
    return jnp.swapaxes(o, 0, 1).reshape((bsz, tq) + o.shape[3:])


def softmax_attention(q, k, v):
    scale = q.shape[-1] ** -0.5

    def block(qb):
        s = jnp.einsum('bqhgd,bkhd->bhgqk', qb, k, preferred_element_type=jnp.float32) * scale
        p = jax.nn.softmax(s, axis=-1).astype(v.dtype)
        return jnp.einsum('bhgqk,bkhe->bqhge', p, v)

    return sweep_query_blocks(block, q)


def differential_attention(q, k, v, lam):
    scale = q.shape[-1] ** -0.5

    def block(qb):
        s = jnp.einsum('bqhmd,bkhmd->bhmqk', qb, k, preferred_element_type=jnp.float32) * scale
        p = jax.nn.softmax(s, axis=-1)
        p = (p[:, :, 0] - lam * p[:, :, 1]).astype(v.dtype)
        return jnp.einsum('bhqk,bkhe->bqhe', p, v)

    return sweep_query_blocks(block, q)


def gated_delta_chunked(q, k, v, g, beta, s0):
    bsz, t, h, dk = k.shape
    dv = v.shape[-1]
    n = t // GDN_CHUNK

    def chunks(a):
        a = a.reshape((bsz, n, GDN_CHUNK, h) + a.shape[3:])
        return jnp.moveaxis(a, (1, 3), (0, 2))

    qc, kc, vc, bc = chunks(q), chunks(k), chunks(v), chunks(beta)
    gc = jnp.cumsum(chunks(g), axis=-1)
    pos = jnp.arange(GDN_CHUNK)
    incl = pos[:, None] >= pos[None, :]
    decay = jnp.exp(jnp.where(incl, gc[..., :, None] - gc[..., None, :], -jnp.inf))
    kbk = jnp.einsum('nbhcd,nbhsd->nbhcs', kc * bc[..., None], kc) * decay
    lower = jnp.where(pos[:, None] > pos[None, :], kbk, 0.0) + jnp.eye(GDN_CHUNK, dtype=kbk.dtype)
    rhs = jnp.concatenate([vc * bc[..., None], kc * (bc * jnp.exp(gc))[..., None]], axis=-1)
    sol = lax.linalg.triangular_solve(lower, rhs, left_side=True, lower=True, unit_diagonal=True)
    u, w = sol[..., :dv], sol[..., dv:]
    qk = jnp.einsum('nbhcd,nbhsd->nbhcs', qc, kc) * decay
    q_dec = qc * jnp.exp(gc)[..., None]
    k_dec = kc * jnp.exp(gc[..., -1:] - gc)[..., None]
    chunk_decay = jnp.exp(gc[..., -1])

    def step(state, inp):
        u_i, w_i, qk_i, qd_i, kd_i, cd_i = inp
        v_new = u_i - jnp.einsum('bhcd,bhde->bhce', w_i, state)
        o_i = jnp.einsum('bhcd,bhde->bhce', qd_i, state) + jnp.einsum('bhcs,bhse->bhce', qk_i, v_new)
        state = state * cd_i[..., None, None] + jnp.einsum('bhcd,bhce->bhde', kd_i, v_new)
        return state, o_i

    state, o = lax.scan(step, s0, (u, w, qk, q_dec, k_dec, chunk_decay))
    o = jnp.moveaxis(o, (0, 2), (1, 3)).reshape(bsz, t, h, dv)
    return o, state


def swiglu(h, w_gu, w_down):
    gate, up = jnp.split(h @ w_gu, 2, axis=-1)
    return (jax.nn.silu(gate) * up) @ w_down


def moe_swiglu(h, router_w, w_gu, w_down):
    shape = h.shape
    xt = h.reshape(-1, shape[-1])
    n_tok = xt.shape[0]
    nk = n_tok * TOP_K
    logits = jnp.einsum('nd,de->ne', xt, router_w, preferred_element_type=jnp.float32)
    top_logit, top_e = lax.top_k(logits, TOP_K)
    top_w = jax.nn.softmax(top_logit, axis=-1)
    flat_e = top_e.reshape(-1)
    flat_tok = jnp.repeat(jnp.arange(n_tok, dtype=jnp.int32), TOP_K)
    flat_w = top_w.reshape(-1)
    order = jnp.argsort(flat_e)
    e_sorted = flat_e[order]
    counts = jnp.bincount(flat_e, length=N_EXPERTS)
    padded = (counts + MOE_BLOCK - 1) // MOE_BLOCK * MOE_BLOCK
    pad_end = jnp.cumsum(padded)
    pad_start = pad_end - padded
    start = jnp.cumsum(counts) - counts
    dest = pad_start[e_sorted] + jnp.arange(nk, dtype=jnp.int32) - start[e_sorted]
    n_blocks = -(-nk // MOE_BLOCK) + N_EXPERTS
    rows = n_blocks * MOE_BLOCK
    buf_tok = jnp.full((rows,), n_tok, jnp.int32).at[dest].set(flat_tok[order])
    buf_w = jnp.zeros((rows,), jnp.float32).at[dest].set(flat_w[order])
    block_e = jnp.minimum(
        jnp.searchsorted(pad_end, jnp.arange(n_blocks) * MOE_BLOCK, side='right'), N_EXPERTS - 1)

    def run_block(args):
        tok, e = args
        xb = jnp.take(xt, tok, axis=0, mode='clip')
        return swiglu(xb, w_gu[e], w_down[e])

    y = lax.map(run_block, (buf_tok.reshape(n_blocks, MOE_BLOCK), block_e))
    y = y.reshape(rows, -1) * buf_w[:, None].astype(y.dtype)
    out = jnp.zeros_like(xt).at[buf_tok].add(y, mode='drop')
    return out.reshape(shape)


def mla_gdn_mixer(h_ctx, h_lat, rope_lat, need_ctx, w_in, q_norm, kv_norm, w_uq, w_ukv,
                  conv_w, a_log, dt_bias, out_norm, w_out):
    pc = split_cols(h_ctx @ w_in, EVEN_SPLITS)
    pl = split_cols(h_lat @ w_in, EVEN_SPLITS)

    def mla_qkv(cq, ckv, k_pe, rope):
        bsz, t = cq.shape[:2]
        q = (rms_norm(cq, q_norm) @ w_uq).reshape(bsz, t, MLA_HEADS, MLA_NOPE + MLA_ROPE)
        kv = (rms_norm(ckv, kv_norm) @ w_ukv).reshape(bsz, t, MLA_HEADS, MLA_NOPE + MLA_V)
        q_nope, q_pe = jnp.split(q, [MLA_NOPE], axis=-1)
        k_nope, v = jnp.split(kv, [MLA_NOPE], axis=-1)
        k_pe = k_pe[:, :, None, :]
        if rope is not None:
            q_pe = apply_axial_rope(q_pe, rope)
            k_pe = apply_axial_rope(k_pe, rope)
        q = jnp.concatenate([q_nope, q_pe], axis=-1)[:, :, :, None, :]
        k = jnp.concatenate([k_nope, jnp.broadcast_to(k_pe, (bsz, t, MLA_HEADS, MLA_ROPE))], axis=-1)
        return q, k, v

    qc, kc, vc = mla_qkv(pc[0], pc[1], pc[2], None)
    ql, kl, vl = mla_qkv(pl[0], pl[1], pl[2], rope_lat)
    bsz, t_lat = h_lat.shape[:2]
    mla_lat = softmax_attention(ql, jnp.concatenate([kc, kl], axis=1),
                                jnp.concatenate([vc, vl], axis=1)).reshape(bsz, t_lat, MLA_HEADS * MLA_V)

    def gdn_prepare(qkv, a, b):
        t = qkv.shape[1]
        qkv = jax.nn.silu(depthwise_conv_centred(qkv, conv_w))
        q, k, v = split_cols(qkv, (GDN_HEADS * GDN_DK, GDN_HEADS * GDN_DK, GDN_HEADS * GDN_DV))
        q = l2_normalize(q.reshape(bsz, t, GDN_HEADS, GDN_DK)) * GDN_DK ** -0.5
        k = l2_normalize(k.reshape(bsz, t, GDN_HEADS, GDN_DK))
        v = v.reshape(bsz, t, GDN_HEADS, GDN_DV).astype(jnp.float32)
        g = -jnp.exp(a_log.astype(jnp.float32)) * jax.nn.softplus(
            a.reshape(bsz, t, 2, GDN_HEADS).astype(jnp.float32) + dt_bias.astype(jnp.float32))
        beta = jax.nn.sigmoid(b.reshape(bsz, t, 2, GDN_HEADS).astype(jnp.float32))
        return q, k, v, g, beta

    def gdn_bidirectional(inputs, s_fwd, s_bwd):
        q, k, v, g, beta = inputs
        rev = lambda a: jnp.flip(a, axis=1)
        o_f, s_fwd = gated_delta_chunked(q, k, v, g[:, :, 0], beta[:, :, 0], s_fwd)
        o_b, s_bwd = gated_delta_chunked(rev(q), rev(k), rev(v), rev(g[:, :, 1]), rev(beta[:, :, 1]), s_bwd)
        return o_f + rev(o_b), s_fwd, s_bwd

    def gdn_output(o, z):
        t = z.shape[1]
        o = rms_norm(o, out_norm) * jax.nn.silu(z.reshape(bsz, t, GDN_HEADS, GDN_DV).astype(jnp.float32))
        return o.reshape(bsz, t, GDN_HEADS * GDN_DV).astype(z.dtype)

    s0 = jnp.zeros((bsz, GDN_HEADS, GDN_DK, GDN_DV), jnp.float32)
    o_c, s_f, s_b = gdn_bidirectional(gdn_prepare(pc[3], pc[5], pc[6]), s0, s0)
    o_l, _, _ = gdn_bidirectional(gdn_prepare(pl[3], pl[5], pl[6]), s_f, s_b)

    out_lat = jnp.concatenate([mla_lat, gdn_output(o_l, pl[4])], axis=-1) @ w_out
    if not need_ctx:
        return None, out_lat
    mla_ctx = softmax_attention(qc, kc, vc).reshape(bsz, h_ctx.shape[1], MLA_HEADS * MLA_V)
    out_ctx = jnp.concatenate([mla_ctx, gdn_output(o_c, pc[4])], axis=-1) @ w_out
    return out_ctx, out_lat


def diff_gqa_mixer(h_ctx, h_lat, rope_diff, rope_gqa, need_ctx, lambda_init, w_in, diff_lambda,
                   diff_norm, q_norm, k_norm, w_out):
    def heads(h, use_rope):
        bsz, t = h.shape[:2]
        dq, dk, dv, gq, gk, gv = split_cols(h @ w_in, ODD_SPLITS)
        dq = dq.reshape(bsz, t, DIFF_HEADS, 2, DIFF_DH)
        dk = dk.reshape(bsz, t, DIFF_HEADS, 2, DIFF_DH)
        dv = dv.reshape(bsz, t, DIFF_HEADS, 2 * DIFF_DH)
        gq = rms_norm(gq.reshape(bsz, t, GQA_HEADS, GQA_DH), q_norm)
        gk = rms_norm(gk.reshape(bsz, t, GQA_KV_HEADS, GQA_DH), k_norm)
        gv = gv.reshape(bsz, t, GQA_KV_HEADS, GQA_DH)
        if use_rope:
            dq, dk = apply_axial_rope(dq, rope_diff), apply_axial_rope(dk, rope_diff)
            gq, gk = apply_axial_rope(gq, rope_gqa), apply_axial_rope(gk, rope_gqa)
        return dq, dk, dv, gq.reshape(bsz, t, GQA_KV_HEADS, GQA_GROUP, GQA_DH), gk, gv

    lam_p = diff_lambda.astype(jnp.float32)
    lam = jnp.exp(jnp.sum(lam_p[0] * lam_p[1])) - jnp.exp(jnp.sum(lam_p[2] * lam_p[3])) + lambda_init

    def mix(dq, gq, dk, dv, gk, gv):
        bsz, t = dq.shape[:2]
        d = differential_attention(dq, dk, dv, lam)
        d = (rms_norm(d, diff_norm) * (1.0 - lambda_init)).reshape(bsz, t, DIFF_HEADS * 2 * DIFF_DH)
        a = softmax_attention(gq, gk, gv).reshape(bsz, t, GQA_HEADS * GQA_DH)
        return jnp.concatenate([d, a], axis=-1) @ w_out

    cdq, cdk, cdv, cgq, cgk, cgv = heads(h_ctx, False)
    ldq, ldk, ldv, lgq, lgk, lgv = heads(h_lat, True)
    cat = lambda a, b: jnp.concatenate([a, b], axis=1)
    out_lat = mix(ldq, lgq, cat(cdk, ldk), cat(cdv, ldv), cat(cgk, lgk), cat(cgv, lgv))
    out_ctx = mix(cdq, cgq, cdk, cdv, cgk, cgv) if need_ctx else None
    return out_ctx, out_lat


def setup_inputs(seed: int = 0) -> dict:
    key = jax.random.key(seed)
    keys = jax.random.split(key, 32)
    counter = iter(range(32))
    f32 = jnp.float32

    def normal(shape, scale):
        return jax.random.normal(keys[next(counter)], shape, f32) * scale

    def gain(shape):
        return 1.0 + 0.02 * jax.random.normal(keys[next(counter)], shape, f32)

    n_ev, n_od = (DEPTH + 1) // 2, DEPTH // 2
    d = D_MODEL
    a_log = jnp.log(jax.random.uniform(keys[next(counter)], (n_ev, 2, GDN_HEADS), f32, 1.0, 16.0))
    dt = jnp.exp(jax.random.uniform(keys[next(counter)], (n_ev, 2, GDN_HEADS), f32,
                                    math.log(1e-3), math.log(1e-1)))
    dt_bias = dt + jnp.log(-jnp.expm1(-dt))
    return {
        'x': normal((BATCH, SEQ, d), 1.0),
        'c': normal((BATCH, d), 1.0),
        'ctx': normal((BATCH, CTX_LEN, d), 1.0),
        'c_ctx': normal((d,), 1.0),
        'mod_w': normal((DEPTH, d, 6 * d), d ** -0.5),
        'mod_b': normal((DEPTH, 6 * d), 0.02),
        'norm_g': gain((DEPTH, 2, d)),
        'ev_w_in': normal((n_ev, d, EVEN_IN), d ** -0.5),
        'ev_mla_q_norm': gain((n_ev, MLA_Q_RANK)),
        'ev_mla_kv_norm': gain((n_ev, MLA_KV_RANK)),
        'ev_mla_w_uq': normal((n_ev, MLA_Q_RANK, MLA_HEADS * (MLA_NOPE + MLA_ROPE)), MLA_Q_RANK ** -0.5),
        'ev_mla_w_ukv': normal((n_ev, MLA_KV_RANK, MLA_HEADS * (MLA_NOPE + MLA_V)), MLA_KV_RANK ** -0.5),
        'ev_gdn_conv': normal((n_ev, GDN_CONV, GDN_QKV), GDN_CONV ** -0.5),
        'ev_gdn_a_log': a_log,
        'ev_gdn_dt_bias': dt_bias,
        'ev_gdn_out_norm': gain((n_ev, GDN_DV)),
        'ev_w_out': normal((n_ev, MIX_WIDTH, d), MIX_WIDTH ** -0.5),
        'ev_ffn_w_gu': normal((n_ev, d, 2 * FFN_DIM), d ** -0.5),
        'ev_ffn_w_down': normal((n_ev, FFN_DIM, d), FFN_DIM ** -0.5),
        'od_w_in': normal((n_od, d, ODD_IN), d ** -0.5),
        'od_diff_lambda': normal((n_od, 4, DIFF_DH), 0.1),
        'od_diff_norm': gain((n_od, 2 * DIFF_DH)),
        'od_gqa_q_norm': gain((n_od, GQA_DH)),
        'od_gqa_k_norm': gain((n_od, GQA_DH)),
        'od_w_out': normal((n_od, MIX_WIDTH, d), MIX_WIDTH ** -0.5),
        'od_router_w': normal((n_od, d, N_EXPERTS), d ** -0.5),
        'od_moe_w_gu': normal((n_od, N_EXPERTS, d, 2 * EXPERT_DIM), d ** -0.5),
        'od_moe_w_down': normal((n_od, N_EXPERTS, EXPERT_DIM, d), EXPERT_DIM ** -0.5),
        'final_norm': gain((d,)),
    }


def reference(x, c, ctx, c_ctx, mod_w, mod_b, norm_g, ev_w_in, ev_mla_q_norm, ev_mla_kv_norm,
              ev_mla_w_uq, ev_mla_w_ukv, ev_gdn_conv, ev_gdn_a_log, ev_gdn_dt_bias, ev_gdn_out_norm,
              ev_w_out, ev_ffn_w_gu, ev_ffn_w_down, od_w_in, od_diff_lambda, od_diff_norm,
              od_gqa_q_norm, od_gqa_k_norm, od_w_out, od_router_w, od_moe_w_gu, od_moe_w_down,
              final_norm):
    n_rows = x.shape[1] // GRID_W
    rope_mla = axial_rope_tables(n_rows, MLA_ROPE)
    rope_diff = axial_rope_tables(n_rows, DIFF_DH)
    rope_gqa = axial_rope_tables(n_rows, GQA_DH)
    for i in range(DEPTH):
        last = i == DEPTH - 1
        j = i // 2
        sh1, sc1, gt1, sh2, sc2, gt2 = modulation(c, mod_w[i], mod_b[i])
        csh1, csc1, cgt1, csh2, csc2, cgt2 = modulation(c_ctx, mod_w[i], mod_b[i])
        h_lat = rms_norm(x, norm_g[i, 0]) * (1.0 + sc1) + sh1
        h_ctx = rms_norm(ctx, norm_g[i, 0]) * (1.0 + csc1) + csh1
        if i % 2 == 0:
            mix_ctx, mix_lat = mla_gdn_mixer(
                h_ctx, h_lat, rope_mla, not last, ev_w_in[j], ev_mla_q_norm[j], ev_mla_kv_norm[j],
                ev_mla_w_uq[j], ev_mla_w_ukv[j], ev_gdn_conv[j], ev_gdn_a_log[j], ev_gdn_dt_bias[j],
                ev_gdn_out_norm[j], ev_w_out[j])
            ffn = functools.partial(swiglu, w_gu=ev_ffn_w_gu[j], w_down=ev_ffn_w_down[j])
        else:
            lambda_init = 0.8 - 0.6 * math.exp(-0.3 * i)
            mix_ctx, mix_lat = diff_gqa_mixer(
                h_ctx, h_lat, rope_diff, rope_gqa, not last, lambda_init, od_w_in[j],
                od_diff_lambda[j], od_diff_norm[j], od_gqa_q_norm[j], od_gqa_k_norm[j], od_w_out[j])
            ffn = functools.partial(moe_swiglu, router_w=od_router_w[j], w_gu=od_moe_w_gu[j],
                                    w_down=od_moe_w_down[j])
        x = x + gt1 * mix_lat
        x = x + gt2 * ffn(rms_norm(x, norm_g[i, 1]) * (1.0 + sc2) + sh2)
        if not last:
            ctx = ctx + cgt1 * mix_ctx
            ctx = ctx + cgt2 * ffn(rms_norm(ctx, norm_g[i, 1]) * (1.0 + csc2) + csh2)
    return rms_norm(x, final_norm)
```

```python
import functools
import math

import numpy as np
import jax
import jax.numpy as jnp
from jax import lax
from jax.experimental import pallas as pl
from jax.experimental.pallas import tpu as pltpu

F32 = jnp.float32
BF16 = jnp.bfloat16

GRID_W = 64
ROPE_BASE = 10000.0
NORM_EPS = 1e-6
MLA_HEADS, MLA_NOPE, MLA_ROPE, MLA_V = 8, 64, 32, 64
MLA_Q_RANK, MLA_KV_RANK = 384, 256
GDN_HEADS, GDN_DK, GDN_DV, GDN_CONV, GDN_CHUNK = 8, 64, 64, 5, 64
GDN_QKV = GDN_HEADS * (2 * GDN_DK + GDN_DV)
DIFF_HEADS, DIFF_DH = 4, 64
GQA_HEADS, GQA_KV_HEADS, GQA_DH = 8, 2, 64
N_EXPERTS, TOP_K, MOE_BLOCK = 8, 2, 512

LANES = 128
SUBLANES = 8
VMEM_LIMIT = 48 * 1024 * 1024
LOG2E = 1.4426950408889634

TOKEN_TILE = 256
ATTN_ROWS = 512


def _cparams(*sem):
    return pltpu.CompilerParams(dimension_semantics=sem, vmem_limit_bytes=VMEM_LIMIT)


def _dot(a, b):
    return jnp.dot(a, b, preferred_element_type=F32)


def _dot_t(a, b):
    return lax.dot_general(a, b, (((1,), (1,)), ((), ())), preferred_element_type=F32)


def _split3(x):
    h1 = x.astype(BF16)
    r1 = x - h1.astype(F32)
    h2 = r1.astype(BF16)
    h3 = (r1 - h2.astype(F32)).astype(BF16)
    return h1, h2, h3


def _silu(x):
    return x * jax.nn.sigmoid(x)


def _rms_mod(x, g, sc, sh):
    ms = jnp.mean(x * x, axis=-1, keepdims=True)
    return (x * lax.rsqrt(ms + NORM_EPS) * g) * (1.0 + sc) + sh


def _seg_sumsq(x, eseg):
    sq = x * x
    hi = sq.astype(BF16)
    lo = (sq - hi.astype(F32)).astype(BF16)
    return _dot(hi, eseg) + _dot(lo, eseg)


def _rope(x, c, s):
    return x * c + pltpu.roll(x, LANES // 2, 1) * s


def _mla_layout():
    src = -np.ones(LANES, np.int64)
    kind = np.zeros(LANES, np.int64)
    freq = np.zeros(LANES, np.int64)
    sign = np.zeros(LANES, np.float32)
    nf = MLA_ROPE // 4
    for f in range(nf):
        src[f] = MLA_NOPE + f; kind[f] = 1; freq[f] = f; sign[f] = -1.0
        src[nf + f] = MLA_NOPE + 2 * nf + f; kind[nf + f] = 2; freq[nf + f] = f; sign[nf + f] = -1.0
        src[64 + f] = MLA_NOPE + nf + f; kind[64 + f] = 1; freq[64 + f] = f; sign[64 + f] = 1.0
        src[64 + nf + f] = MLA_NOPE + 3 * nf + f; kind[64 + nf + f] = 2; freq[64 + nf + f] = f
        sign[64 + nf + f] = 1.0
    src[16:64] = np.arange(0, 48)
    src[80:96] = np.arange(48, 64)
    return src, kind, freq, sign


def _pair_layout():
    lane = np.arange(LANES)
    region = lane // 32
    m = region % 2
    is_b = region // 2
    within = lane % 32
    is_col = within // 16
    f = within % 16
    src = is_col * 32 + is_b * 16 + f
    kind = 1 + is_col
    sign = np.where(is_b == 0, -1.0, 1.0).astype(np.float32)
    return m, src, kind, f, sign


def _rope_tables(t_len, kind, freq, sign, n_freq, scale, with_pos):
    kind_j = jnp.asarray(kind)
    if not with_pos:
        c = jnp.full((t_len, LANES), scale, F32)
        return c, jnp.zeros((t_len, LANES), F32)
    n_rows = t_len // GRID_W
    inv = 1.0 / (ROPE_BASE ** (jnp.arange(n_freq, dtype=F32) / n_freq))
    rows = jnp.repeat(jnp.arange(n_rows, dtype=F32), GRID_W)
    cols = jnp.tile(jnp.arange(GRID_W, dtype=F32), n_rows)
    ang_r = (rows[:, None] * inv)[:, freq]
    ang_c = (cols[:, None] * inv)[:, freq]
    ang = jnp.where(kind_j[None, :] == 1, ang_r, ang_c)
    has = (kind_j > 0)[None, :]
    c = jnp.where(has, jnp.cos(ang), 1.0) * scale
    s = jnp.where(has, jnp.sin(ang) * jnp.asarray(sign)[None, :], 0.0) * scale
    return c.astype(F32), s.astype(F32)


def _take_cols(w, idx):
    idx = np.asarray(idx)
    wz = jnp.concatenate([w, jnp.zeros((w.shape[0], 1), w.dtype)], axis=1)
    return wz[:, np.where(idx < 0, w.shape[1], idx)]


def _seg_ones(width, seg):
    r = np.arange(width)
    return jnp.asarray((r[:, None] // seg) == (r[None, :] // seg), dtype=BF16)


def _mod_kernel(c_ref, w_ref, b_ref, o_ref):
    c = c_ref[...]
    s1, s2, s3 = _split3(_silu(c))
    w1, w2, _ = _split3(w_ref[...])
    acc = _dot(s1, w1) + _dot(s1, w2) + _dot(s2, w1) + _dot(s2, w2) + _dot(s3, w1)
    o_ref[...] = acc + b_ref[...]


def _modulation(cs, mod_w, mod_b):
    depth, d, n = mod_w.shape
    tn = 512
    return pl.pallas_call(
        _mod_kernel,
        grid=(depth, n // tn),
        in_specs=[
            pl.BlockSpec(cs.shape, lambda l, j: (0, 0)),
            pl.BlockSpec((None, d, tn), lambda l, j: (l, 0, j)),
            pl.BlockSpec((None, 1, tn), lambda l, j: (l, 0, j)),
        ],
        out_specs=pl.BlockSpec((None, cs.shape[0], tn), lambda l, j: (l, 0, j)),
        out_shape=jax.ShapeDtypeStruct((depth, cs.shape[0], n), F32),
        compiler_params=_cparams("parallel", "parallel"),
        name="modulation",
    )(cs, mod_w, mod_b.reshape(depth, 1, n))


def _even_proj_kernel(x_ref, mod_ref, g_ref, cq_t, sq_t, ck_t, sk_t,
                      wcq, wckv, wkpe, wgqkv, wgz, wab, qn, kvn, wuq, wuk, wuv,
                      q_out, k_out, v_out, gqkv_out, gz_out, ab_out):
    h = _rms_mod(x_ref[...], g_ref[...], mod_ref[1:2, :], mod_ref[0:1, :]).astype(BF16)
    cq = _dot(h, wcq[...])
    ckv = _dot(h, wckv[...])
    kpe = _dot(h, wkpe[...])
    nq = (cq * lax.rsqrt(jnp.mean(cq * cq, axis=-1, keepdims=True) + NORM_EPS) * qn[...]).astype(BF16)
    nkv = (ckv * lax.rsqrt(jnp.mean(ckv * ckv, axis=-1, keepdims=True) + NORM_EPS) * kvn[...]).astype(BF16)
    q = _dot(nq, wuq[...])
    kn = _dot(nkv, wuk[...])
    kper = _rope(kpe, ck_t[...], sk_t[...])
    cqv, sqv = cq_t[...], sq_t[...]
    for hd in range(MLA_HEADS):
        sl = slice(hd * LANES, (hd + 1) * LANES)
        q_out[:, sl] = _rope(q[:, sl], cqv, sqv).astype(BF16)
        k_out[:, sl] = (kn[:, sl] + kper).astype(BF16)
    v_out[...] = _dot(nkv, wuv[...]).astype(BF16)
    gqkv_out[...] = _dot(h, wgqkv[...])
    gz_out[...] = _dot(h, wgz[...])
    ab_out[...] = _dot(h, wab[...])


def _full(a):
    nd = a.ndim
    return pl.BlockSpec(a.shape, lambda *_: (0,) * nd)


def _even_proj(x, mod, g, tabs, ws):
    b, t, d = x.shape
    tm = TOKEN_TILE
    tok = lambda w: pl.BlockSpec((None, tm, w), lambda bi, i: (bi, i, 0))
    tab = pl.BlockSpec((tm, LANES), lambda bi, i: (i, 0))
    widths = (MLA_HEADS * LANES, MLA_HEADS * LANES, MLA_HEADS * MLA_V, GDN_QKV,
              GDN_HEADS * GDN_DV, 4 * GDN_HEADS)
    dts = (BF16, BF16, BF16, F32, F32, F32)
    return pl.pallas_call(
        _even_proj_kernel,
        grid=(b, t // tm),
        in_specs=[tok(d), pl.BlockSpec((None, 6, d), lambda bi, i: (bi, 0, 0)), _full(g),
                  tab, tab, tab, tab] + [_full(w) for w in ws],
        out_specs=[tok(w) for w in widths],
        out_shape=[jax.ShapeDtypeStruct((b, t, w), dt) for w, dt in zip(widths, dts)],
        compiler_params=_cparams("parallel", "parallel"),
        name="even_proj",
    )(x, mod, g, *tabs, *ws)


def _flash_kernel(q_ref, k_ref, v_ref, o_ref, m_sc, l_sc, acc_sc, *, g, tq, tk, nk):
    q = jnp.concatenate([q_ref[:, i * LANES:(i + 1) * LANES] for i in range(g)], axis=0)
    m_sc[...] = jnp.full(m_sc.shape, -jnp.inf, F32)
    l_sc[...] = jnp.zeros(l_sc.shape, F32)
    acc_sc[...] = jnp.zeros(acc_sc.shape, F32)

    def body(j, carry):
        start = pl.multiple_of(j * tk, tk)
        k = k_ref[pl.ds(start, tk), :]
        v = v_ref[pl.ds(start, tk), :]
        s = _dot_t(q, k)
        m_prev = m_sc[...]
        m_new = jnp.maximum(m_prev, jnp.max(s, axis=-1, keepdims=True))
        p = jnp.exp2(s - m_new)
        alpha = jnp.exp2(m_prev - m_new)
        l_sc[...] = alpha * l_sc[...] + jnp.sum(p, axis=-1, keepdims=True)
        acc_sc[...] = alpha * acc_sc[...] + _dot(p.astype(BF16), v)
        m_sc[...] = m_new
        return carry

    lax.fori_loop(0, nk, body, 0)
    o = acc_sc[...] / l_sc[...]
    for i in range(g):
        o_ref[:, i * LANES:(i + 1) * LANES] = o[i * tq:(i + 1) * tq, :].astype(o_ref.dtype)


def _pick_tk(tk_total):
    for cand in (768, 512, 256, 128):
        if tk_total % cand == 0:
            return cand
    raise ValueError(f"key length {tk_total} is not a multiple of 128")


def _flash(q, k, v, *, groups, g, kmap, vmap, out_dtype):
    b, tq_total, _ = q.shape
    tk_total = k.shape[1]
    tq = min(ATTN_ROWS // g, tq_total)
    tk = _pick_tk(tk_total)
    m = g * tq
    kern = functools.partial(_flash_kernel, g=g, tq=tq, tk=tk, nk=tk_total // tk)
    return pl.pallas_call(
        kern,
        grid=(b, groups, tq_total // tq),
        in_specs=[
            pl.BlockSpec((None, tq, g * LANES), lambda bi, hg, i: (bi, i, hg)),
            pl.BlockSpec((None, tk_total, LANES), lambda bi, hg, i: (bi, 0, kmap(hg))),
            pl.BlockSpec((None, tk_total, LANES), lambda bi, hg, i: (bi, 0, vmap(hg))),
        ],
        out_specs=pl.BlockSpec((None, tq, g * LANES), lambda bi, hg, i: (bi, i, hg)),
        out_shape=jax.ShapeDtypeStruct(q.shape, out_dtype),
        scratch_shapes=[pltpu.VMEM((m, 1), F32), pltpu.VMEM((m, 1), F32), pltpu.VMEM((m, LANES), F32)],
        compiler_params=_cparams("parallel", "parallel", "parallel"),
        name="flash_attention",
    )(q, k, v)


def _gdn_prep_kernel(x_ref, prev_ref, next_ref, ab_ref, cw_ref, prm_ref, eseg_ref,
                     lf_ref, lb_ref, la_ref, q_out, k_out, v_out, s1_out, s2_out, xe_sc):
    i = pl.program_id(1)
    tm = x_ref.shape[0]
    halo = SUBLANES
    xe_sc[0:halo, :] = prev_ref[...] * (i > 0).astype(F32)
    xe_sc[halo:halo + tm, :] = x_ref[...]
    xe_sc[halo + tm:2 * halo + tm, :] = next_ref[...] * (i < pl.num_programs(1) - 1).astype(F32)
    acc = jnp.zeros(x_ref.shape, F32)
    for j in range(GDN_CONV):
        acc = acc + cw_ref[j:j + 1, :] * xe_sc[pl.ds(halo - GDN_CONV // 2 + j, tm), :]
    y = _silu(acc)
    hw = GDN_HEADS * GDN_DK
    eseg = eseg_ref[...]
    q = y[:, 0:hw]
    k = y[:, hw:2 * hw]
    q_out[...] = q * lax.rsqrt(_seg_sumsq(q, eseg) + NORM_EPS) * (GDN_DK ** -0.5)
    k_out[...] = k * lax.rsqrt(_seg_sumsq(k, eseg) + NORM_EPS)
    v_out[...] = y[:, 2 * hw:]

    ab = ab_ref[...]
    nh2 = 2 * GDN_HEADS
    lane = lax.broadcasted_iota(jnp.int32, ab.shape, 1)
    z = ab + prm_ref[1:2, :]
    softplus = jnp.maximum(z, 0.0) + jnp.log1p(jnp.exp(-jnp.abs(z)))
    gate = jnp.where(lane < nh2, -jnp.exp(prm_ref[0:1, :]) * softplus, 0.0)
    g1, g2, g3 = _split3(gate)
    lf, lb, la = lf_ref[...], lb_ref[...], la_ref[...]
    cum_f = _dot(lf, g1) + _dot(lf, g2) + _dot(lf, g3)
    cum_b = _dot(lb, g1) + _dot(lb, g2) + _dot(lb, g3)
    tot = _dot(la, g1) + _dot(la, g2) + _dot(la, g3)
    cum = jnp.where(lane < GDN_HEADS, cum_f, cum_b)
    s1_out[...] = jnp.where(lane < nh2, cum, jax.nn.sigmoid(ab))
    s2_out[...] = tot


def _gdn_prep(gqkv, ab, conv_w, prm, eseg):
    b, t, w = gqkv.shape
    tm = TOKEN_TILE
    nt8 = t // SUBLANES
    per = tm // SUBLANES
    r = np.arange(tm)
    same = (r[:, None] // GDN_CHUNK) == (r[None, :] // GDN_CHUNK)
    lf = jnp.asarray(same & (r[None, :] <= r[:, None]), dtype=BF16)
    lb = jnp.asarray(same & (r[None, :] >= r[:, None]), dtype=BF16)
    la = jnp.asarray(same, dtype=BF16)
    halo = gqkv.reshape(b, nt8, SUBLANES, w)
    tok = lambda wd: pl.BlockSpec((None, tm, wd), lambda bi, i: (bi, i, 0))
    hw = GDN_HEADS * GDN_DK
    nab = ab.shape[-1]
    return pl.pallas_call(
        _gdn_prep_kernel,
        grid=(b, t // tm),
        in_specs=[
            tok(w),
            pl.BlockSpec((None, None, SUBLANES, w), lambda bi, i: (bi, jnp.maximum(i * per - 1, 0), 0, 0)),
            pl.BlockSpec((None, None, SUBLANES, w), lambda bi, i: (bi, jnp.minimum((i + 1) * per, nt8 - 1), 0, 0)),
            tok(nab), _full(conv_w), _full(prm), _full(eseg), _full(lf), _full(lb), _full(la),
        ],
        out_specs=[tok(hw), tok(hw), tok(GDN_HEADS * GDN_DV), tok(nab), tok(nab)],
        out_shape=[jax.ShapeDtypeStruct((b, t, hw), F32), jax.ShapeDtypeStruct((b, t, hw), F32),
                   jax.ShapeDtypeStruct((b, t, GDN_HEADS * GDN_DV), F32),
                   jax.ShapeDtypeStruct((b, t, nab), F32), jax.ShapeDtypeStruct((b, t, nab), F32)],
        scratch_shapes=[pltpu.VMEM((tm + 2 * SUBLANES, w), F32)],
        compiler_params=_cparams("parallel", "parallel"),
        name="gdn_prep",
    )(gqkv, halo, halo, ab, conv_w, prm, eseg, lf, lb, la)


def _gdn_intra_kernel(q_ref, k_ref, v_ref, sc_ref, a_out, b_out, qe_out, o0_out, *, cb):
    c_len = GDN_CHUNK
    row = lax.broadcasted_iota(jnp.int32, (c_len, c_len), 0)
    col = lax.broadcasted_iota(jnp.int32, (c_len, c_len), 1)
    eye_f = (row == col).astype(F32)
    eye_b = eye_f.astype(BF16)
    ones_b = jnp.ones((c_len, c_len), BF16)
    for c in range(cb):
        sl = pl.ds(c * c_len, c_len)
        q, k, v = q_ref[sl, :], k_ref[sl, :], v_ref[sl, :]
        scv = sc_ref[sl, :]
        gc, beta, gl = scv[:, 0:1], scv[:, 1:2], scv[:, 2:3]
        eg, ek, cd = jnp.exp(gc), jnp.exp(gl - gc), jnp.exp(gl)
        g_rows = jnp.broadcast_to(gc, (c_len, c_len))
        y1, y2, y3 = _split3(jnp.where(col == 0, g_rows, 0.0))
        g_cols = _dot_t(ones_b, y1) + _dot_t(ones_b, y2) + _dot_t(ones_b, y3)
        decay = jnp.where(row >= col, jnp.exp(jnp.minimum(g_rows - g_cols, 0.0)), 0.0)
        kb = k * beta
        kq = _dot_t(jnp.concatenate([kb, q], axis=0).astype(BF16), k.astype(BF16))
        qk = kq[c_len:, :] * decay
        low = jnp.where(row > col, kq[:c_len, :] * decay, 0.0)
        mpow = jnp.where((row // 8) == (col // 8), -low, 0.0)
        tinv = eye_f + mpow
        for _ in range(2):
            mb = mpow.astype(BF16)
            mpow = _dot(mb, mb)
            tinv = tinv + _dot(tinv.astype(BF16), mpow.astype(BF16))
        for s in (8, 16, 32):
            off = ((row // (2 * s)) == (col // (2 * s))) & ((row // s) != (col // s))
            tb = tinv.astype(BF16)
            tinv = tinv - _dot(_dot(tb, jnp.where(off, low, 0.0).astype(BF16)).astype(BF16), tb)
        tb = tinv.astype(BF16)
        u = _dot(tb, (v * beta).astype(BF16))
        w = _dot(tb, (kb * eg).astype(BF16))
        ub, wb = u.astype(BF16), w.astype(BF16)
        kdt = _dot_t(eye_b, (k * ek).astype(BF16)).astype(BF16)
        a_out[c] = cd * eye_f - _dot(kdt, wb)
        b_out[c] = _dot(kdt, ub)
        qkb = qk.astype(BF16)
        qe_out[sl, :] = q * eg - _dot(qkb, wb)
        o0_out[sl, :] = _dot(qkb, ub)


def _gdn_intra(q, k, v, sc):
    nd, b, h, tt, dk = q.shape
    nc = tt // GDN_CHUNK
    cb = 4
    rows = cb * GDN_CHUNK
    tokb = lambda wd: pl.BlockSpec((None, None, None, rows, wd), lambda d, bi, hi, c: (d, bi, hi, c, 0))
    matb = pl.BlockSpec((None, None, None, cb, dk, dk), lambda d, bi, hi, c: (d, bi, hi, c, 0, 0))
    return pl.pallas_call(
        functools.partial(_gdn_intra_kernel, cb=cb),
        grid=(nd, b, h, nc // cb),
        in_specs=[tokb(dk), tokb(dk), tokb(dk), tokb(sc.shape[-1])],
        out_specs=[matb, matb, tokb(dk), tokb(dk)],
        out_shape=[jax.ShapeDtypeStruct((nd, b, h, nc, dk, dk), F32)] * 2
                  + [jax.ShapeDtypeStruct((nd, b, h, tt, dk), F32)] * 2,
        compiler_params=_cparams("parallel", "parallel", "parallel", "parallel"),
        name="gdn_intra",
    )(q, k, v, sc)


def _gdn_inter_kernel(a_ref, b_ref, qe_ref, o0_ref, o_ref, s_sc):
    @pl.when(pl.program_id(1) == 0)
    def _():
        s_sc[...] = jnp.zeros(s_sc.shape, F32)

    n = s_sc.shape[0]
    dk = s_sc.shape[1]
    sb = s_sc[...].astype(BF16)
    qe = qe_ref[...].reshape(n, GDN_CHUNK, dk).astype(BF16)
    a = a_ref[...].reshape(n, dk, dk).astype(BF16)
    o = jnp.einsum("ncd,nde->nce", qe, sb, preferred_element_type=F32)
    o_ref[...] = (o + o0_ref[...].reshape(o.shape)).reshape(o_ref.shape)
    s_new = jnp.einsum("ndk,nke->nde", a, sb, preferred_element_type=F32)
    s_sc[...] = s_new + b_ref[...].reshape(s_new.shape)


def _gdn_inter(a, bm, qe, o0):
    nd, b, h, nc, dk, _ = a.shape
    matb = pl.BlockSpec((nd, None, h, None, dk, dk), lambda bi, c: (0, bi, 0, c, 0, 0))
    tokb = pl.BlockSpec((nd, None, h, GDN_CHUNK, dk), lambda bi, c: (0, bi, 0, c, 0))
    return pl.pallas_call(
        _gdn_inter_kernel,
        grid=(b, nc),
        in_specs=[matb, matb, tokb, tokb],
        out_specs=tokb,
        out_shape=jax.ShapeDtypeStruct(qe.shape, F32),
        scratch_shapes=[pltpu.VMEM((nd * h, dk, dk), F32)],
        compiler_params=_cparams("parallel", "arbitrary"),
        name="gdn_inter",
    )(a, bm, qe, o0)


def _gdn_scan(prep_ctx, prep_lat):
    tc = prep_ctx[0].shape[1]
    b = prep_ctx[0].shape[0]
    nh = GDN_HEADS

    def dirs(xc, xl):
        fwd = jnp.concatenate([xc, xl], axis=1)
        bwd = jnp.concatenate([jnp.flip(xc, 1), jnp.flip(xl, 1)], axis=1)
        return jnp.stack([fwd, bwd])

    def heads(x):
        nd, bb, tt, _ = x.shape
        return x.reshape(nd, bb, tt, nh, -1).transpose(0, 1, 3, 2, 4)

    q, k, v = (heads(dirs(c, l)) for c, l in zip(prep_ctx[:3], prep_lat[:3]))
    s1 = dirs(prep_ctx[3], prep_lat[3])
    s2 = dirs(prep_ctx[4], prep_lat[4])
    sc = jnp.stack([
        jnp.stack([s1[d, :, :, d * nh:(d + 1) * nh], s1[d, :, :, 2 * nh + d * nh:2 * nh + (d + 1) * nh],
                   s2[d, :, :, d * nh:(d + 1) * nh]], axis=-1)
        for d in range(2)])
    sc = jnp.pad(sc.transpose(0, 1, 3, 2, 4), ((0, 0),) * 4 + ((0, SUBLANES - 3),))
    a, bm, qe, o0 = _gdn_intra(q, k, v, sc)
    o = _gdn_inter(a, bm, qe, o0)
    o = o.transpose(0, 1, 3, 2, 4).reshape(2, b, -1, nh * GDN_DV)
    o_ctx = o[0, :, :tc] + jnp.flip(o[1, :, :tc], 1)
    o_lat = o[0, :, tc:] + jnp.flip(o[1, :, tc:], 1)
    return o_ctx, o_lat


def _mix_even_kernel(x_ref, mod_ref, ao_ref, go_ref, gz_ref, on_ref, eseg_ref, wa_ref, wg_ref, out_ref):
    o = go_ref[...]
    ms = _seg_sumsq(o, eseg_ref[...]) * (1.0 / GDN_DV)
    y = o * lax.rsqrt(ms + NORM_EPS) * on_ref[...] * _silu(gz_ref[...])
    mix = _dot(ao_ref[...], wa_ref[...]) + _dot(y.astype(BF16), wg_ref[...])
    out_ref[...] = x_ref[...] + mod_ref[2:3, :] * mix


def _mix_even(x, mod, ao, go, gz, on, eseg, wa, wg):
    b, t, d = x.shape
    tm = TOKEN_TILE
    tok = lambda w: pl.BlockSpec((None, tm, w), lambda bi, i: (bi, i, 0))
    return pl.pallas_call(
        _mix_even_kernel,
        grid=(b, t // tm),
        in_specs=[tok(d), pl.BlockSpec((None, 6, d), lambda bi, i: (bi, 0, 0)), tok(ao.shape[-1]),
                  tok(go.shape[-1]), tok(gz.shape[-1]), _full(on), _full(eseg), _full(wa), _full(wg)],
        out_specs=tok(d),
        out_shape=jax.ShapeDtypeStruct(x.shape, F32),
        compiler_params=_cparams("parallel", "parallel"),
        name="mix_even",
    )(x, mod, ao, go, gz, on, eseg, wa, wg)


def _ffn_kernel(x_ref, mod_ref, g_ref, wg_ref, wu_ref, wd_ref, out_ref, h_sc, acc_sc):
    j = pl.program_id(2)

    @pl.when(j == 0)
    def _():
        h_sc[...] = _rms_mod(x_ref[...], g_ref[...], mod_ref[4:5, :], mod_ref[3:4, :]).astype(BF16)
        acc_sc[...] = jnp.zeros(acc_sc.shape, F32)

    h = h_sc[...]
    act = (_silu(_dot(h, wg_ref[...])) * _dot(h, wu_ref[...])).astype(BF16)
    acc_sc[...] += _dot(act, wd_ref[...])

    @pl.when(j == pl.num_programs(2) - 1)
    def _():
        out_ref[...] = x_ref[...] + mod_ref[5:6, :] * acc_sc[...]


def _ffn_tile(f):
    for cand in (1408, 1792, 1024, 896, 768, 512, 256, 128):
        if f % cand == 0:
            return cand
    raise ValueError(f"ffn width {f} is not a multiple of 128")


def _ffn(x, mod, g, w_gu, w_down):
    b, t, d = x.shape
    f = w_down.shape[0]
    tm = 512 if t % 512 == 0 else TOKEN_TILE
    tf = _ffn_tile(f)
    nf = f // tf
    tok = pl.BlockSpec((None, tm, d), lambda bi, i, j: (bi, i, 0))
    return pl.pallas_call(
        _ffn_kernel,
        grid=(b, t // tm, nf),
        in_specs=[tok, pl.BlockSpec((None, 6, d), lambda bi, i, j: (bi, 0, 0)),
                  pl.BlockSpec(g.shape, lambda bi, i, j: (0, 0)),
                  pl.BlockSpec((d, tf), lambda bi, i, j: (0, j)),
                  pl.BlockSpec((d, tf), lambda bi, i, j: (0, nf + j)),
                  pl.BlockSpec((tf, d), lambda bi, i, j: (j, 0))],
        out_specs=tok,
        out_shape=jax.ShapeDtypeStruct(x.shape, F32),
        scratch_shapes=[pltpu.VMEM((tm, d), BF16), pltpu.VMEM((tm, d), F32)],
        compiler_params=_cparams("parallel", "parallel", "arbitrary"),
        name="ffn",
    )(x, mod, g, w_gu, w_gu, w_down)


def _odd_proj_kernel(x_ref, mod_ref, g_ref, c_t, s_t, wdq, wdk, wdv, wgq, wgk, wgv, gqg, gkg,
                     dq_out, dk_out, dv_out, gq_out, gk_out, gv_out, *, qscale):
    h = _rms_mod(x_ref[...], g_ref[...], mod_ref[1:2, :], mod_ref[0:1, :]).astype(BF16)
    c, s = c_t[...], s_t[...]
    lane = lax.broadcasted_iota(jnp.int32, c.shape, 1)
    is_m1 = ((lane // 32) % 2) == 1
    dq = _dot(h, wdq[...])
    dk = _dot(h, wdk[...])
    for hd in range(DIFF_HEADS):
        sl = slice(hd * LANES, (hd + 1) * LANES)
        r = _rope(dq[:, sl], c, s) * qscale
        dq_out[:, 2 * hd * LANES:(2 * hd + 1) * LANES] = jnp.where(is_m1, 0.0, r).astype(BF16)
        dq_out[:, (2 * hd + 1) * LANES:(2 * hd + 2) * LANES] = jnp.where(is_m1, r, 0.0).astype(BF16)
        dk_out[:, sl] = _rope(dk[:, sl], c, s).astype(BF16)
    dv_out[...] = _dot(h, wdv[...]).astype(BF16)
    gq = _dot(h, wgq[...])
    gqgv = gqg[...]
    inv_dh = 1.0 / GQA_DH
    for hd in range(GQA_HEADS):
        sl = slice(hd * LANES, (hd + 1) * LANES)
        xh = gq[:, sl]
        ms = jnp.sum(xh * xh, axis=-1, keepdims=True) * inv_dh
        xn = xh * lax.rsqrt(ms + NORM_EPS) * gqgv[:, sl]
        gq_out[:, sl] = (_rope(xn, c, s) * qscale).astype(BF16)
    gk = _dot(h, wgk[...])
    sq = gk * gk
    ms0 = jnp.sum(jnp.where(is_m1, 0.0, sq), axis=-1, keepdims=True) * inv_dh
    ms1 = jnp.sum(jnp.where(is_m1, sq, 0.0), axis=-1, keepdims=True) * inv_dh
    rs = jnp.where(is_m1, lax.rsqrt(ms1 + NORM_EPS), lax.rsqrt(ms0 + NORM_EPS))
    gk_out[...] = _rope(gk * rs * gkg[...], c, s).astype(BF16)
    gv_out[...] = _dot(h, wgv[...]).astype(BF16)


def _odd_proj(x, mod, g, tabs, ws, gains, qscale):
    b, t, d = x.shape
    tm = TOKEN_TILE
    tok = lambda w: pl.BlockSpec((None, tm, w), lambda bi, i: (bi, i, 0))
    tab = pl.BlockSpec((tm, LANES), lambda bi, i: (i, 0))
    widths = (2 * DIFF_HEADS * LANES, DIFF_HEADS * LANES, DIFF_HEADS * LANES,
              GQA_HEADS * LANES, LANES, LANES)
    return pl.pallas_call(
        functools.partial(_odd_proj_kernel, qscale=qscale),
        grid=(b, t // tm),
        in_specs=[tok(d), pl.BlockSpec((None, 6, d), lambda bi, i: (bi, 0, 0)), _full(g), tab, tab]
                 + [_full(w) for w in ws] + [_full(w) for w in gains],
        out_specs=[tok(w) for w in widths],
        out_shape=[jax.ShapeDtypeStruct((b, t, w), BF16) for w in widths],
        compiler_params=_cparams("parallel", "parallel"),
        name="odd_proj",
    )(x, mod, g, *tabs, *ws, *gains)


def _mix_odd_kernel(x_ref, mod_ref, do_ref, go_ref, lam_ref, dn_ref, wd_ref, wg_ref, out_ref, *, lambda_init):
    lp = lam_ref[...]
    lam = (jnp.exp(jnp.sum(lp[0:1, :] * lp[1:2, :], axis=-1, keepdims=True))
           - jnp.exp(jnp.sum(lp[2:3, :] * lp[3:4, :], axis=-1, keepdims=True)) + lambda_init)
    dn = dn_ref[...]
    parts = []
    for hd in range(DIFF_HEADS):
        d0 = do_ref[:, 2 * hd * LANES:(2 * hd + 1) * LANES]
        d1 = do_ref[:, (2 * hd + 1) * LANES:(2 * hd + 2) * LANES]
        dd = d0 - lam * d1
        ms = jnp.mean(dd * dd, axis=-1, keepdims=True)
        parts.append(((dd * lax.rsqrt(ms + NORM_EPS) * dn) * (1.0 - lambda_init)).astype(BF16))
    dcat = jnp.concatenate(parts, axis=1)
    mix = _dot(dcat, wd_ref[...]) + _dot(go_ref[...], wg_ref[...])
    out_ref[...] = x_ref[...] + mod_ref[2:3, :] * mix


def _mix_odd(x, mod, do, go, lam_p, dn, wd, wg, lambda_init):
    b, t, d = x.shape
    tm = TOKEN_TILE
    tok = lambda w: pl.BlockSpec((None, tm, w), lambda bi, i: (bi, i, 0))
    return pl.pallas_call(
        functools.partial(_mix_odd_kernel, lambda_init=lambda_init),
        grid=(b, t // tm),
        in_specs=[tok(d), pl.BlockSpec((None, 6, d), lambda bi, i: (bi, 0, 0)), tok(do.shape[-1]),
                  tok(go.shape[-1]), _full(lam_p), _full(dn), _full(wd), _full(wg)],
        out_specs=tok(d),
        out_shape=jax.ShapeDtypeStruct(x.shape, F32),
        compiler_params=_cparams("parallel", "parallel"),
        name="mix_odd",
    )(x, mod, do, go, lam_p, dn, wd, wg)


def _router_kernel(x_ref, mod_ref, g_ref, rw_ref, h_out, route_out):
    h = _rms_mod(x_ref[...], g_ref[...], mod_ref[4:5, :], mod_ref[3:4, :])
    h_out[...] = h
    h1, h2, h3 = _split3(h)
    w1, w2, w3 = _split3(rw_ref[...])
    logits = (_dot(h1, w1) + _dot(h1, w2) + _dot(h2, w1) + _dot(h2, w2) + _dot(h1, w3) + _dot(h3, w1))
    lane = lax.broadcasted_iota(jnp.int32, logits.shape, 1).astype(F32)
    neg = -jnp.inf
    l1 = jnp.where(lane < N_EXPERTS, logits, neg)
    m1 = jnp.max(l1, axis=-1, keepdims=True)
    i1 = jnp.min(jnp.where(l1 == m1, lane, float(LANES)), axis=-1, keepdims=True)
    l2 = jnp.where(lane == i1, neg, l1)
    m2 = jnp.max(l2, axis=-1, keepdims=True)
    i2 = jnp.min(jnp.where(l2 == m2, lane, float(LANES)), axis=-1, keepdims=True)
    e = jnp.exp(m2 - m1)
    p1 = 1.0 / (1.0 + e)
    p2 = e / (1.0 + e)
    route = jnp.where(lane == 0.0, i1,
                      jnp.where(lane == 1.0, i2, jnp.where(lane == 2.0, p1, jnp.where(lane == 3.0, p2, 0.0))))
    route_out[...] = route[:, 0:SUBLANES]


def _router(x, mod, g, rw):
    b, t, d = x.shape
    tm = TOKEN_TILE
    tok = lambda w: pl.BlockSpec((None, tm, w), lambda bi, i: (bi, i, 0))
    return pl.pallas_call(
        _router_kernel,
        grid=(b, t // tm),
        in_specs=[tok(d), pl.BlockSpec((None, 6, d), lambda bi, i: (bi, 0, 0)), _full(g), _full(rw)],
        out_specs=[tok(d), tok(SUBLANES)],
        out_shape=[jax.ShapeDtypeStruct(x.shape, F32), jax.ShapeDtypeStruct((b, t, SUBLANES), F32)],
        compiler_params=_cparams("parallel", "parallel"),
        name="moe_router",
    )(x, mod, g, rw)


def _issue_row_gather(idx_ref, n, src_hbm, dst_ref, sem):
    def body(r, carry):
        pltpu.make_async_copy(src_hbm.at[pl.ds(idx_ref[0, r], 1), :], dst_ref.at[pl.ds(r, 1), :], sem).start()
        return carry
    lax.fori_loop(0, n, body, 0)


def _wait_row_gather(src_hbm, dst_ref, sem):
    pltpu.make_async_copy(src_hbm.at[pl.ds(0, dst_ref.shape[0]), :], dst_ref, sem).wait()


def _gather_kernel(idx_ref, src_hbm, out_ref, buf, sem):
    n = buf.shape[0]
    _issue_row_gather(idx_ref, n, src_hbm, buf, sem)
    _wait_row_gather(src_hbm, buf, sem)
    out_ref[...] = buf[...].astype(out_ref.dtype)


def _gather_rows(src, idx, out_dtype):
    nb, blk = idx.shape
    d = src.shape[1]
    return pl.pallas_call(
        _gather_kernel,
        grid=(nb,),
        in_specs=[pl.BlockSpec((None, 1, blk), lambda i: (i, 0, 0), memory_space=pltpu.SMEM),
                  pl.BlockSpec(memory_space=pl.ANY)],
        out_specs=pl.BlockSpec((blk, d), lambda i: (i, 0)),
        out_shape=jax.ShapeDtypeStruct((nb * blk, d), out_dtype),
        scratch_shapes=[pltpu.VMEM((blk, d), src.dtype), pltpu.SemaphoreType.DMA(())],
        compiler_params=_cparams("arbitrary"),
        name="moe_gather",
    )(idx.reshape(nb, 1, blk), src)


def _moe_kernel(be_ref, xs_ref, wg_ref, wu_ref, wd_ref, y_ref, acc_sc):
    j = pl.program_id(1)

    @pl.when(j == 0)
    def _():
        acc_sc[...] = jnp.zeros(acc_sc.shape, F32)

    x = xs_ref[...]
    act = (_silu(_dot(x, wg_ref[...])) * _dot(x, wu_ref[...])).astype(BF16)
    acc_sc[...] += _dot(act, wd_ref[...])

    @pl.when(j == pl.num_programs(1) - 1)
    def _():
        y_ref[...] = acc_sc[...]


def _moe_experts(xs, block_e, w_gu, w_down):
    rows, d = xs.shape
    f = w_down.shape[1]
    tf = _ffn_tile(f)
    nf = f // tf
    nb = rows // MOE_BLOCK
    grid_spec = pltpu.PrefetchScalarGridSpec(
        num_scalar_prefetch=1,
        grid=(nb, nf),
        in_specs=[pl.BlockSpec((MOE_BLOCK, d), lambda i, j, be: (i, 0)),
                  pl.BlockSpec((None, d, tf), lambda i, j, be: (be[i], 0, j)),
                  pl.BlockSpec((None, d, tf), lambda i, j, be: (be[i], 0, nf + j)),
                  pl.BlockSpec((None, tf, d), lambda i, j, be: (be[i], j, 0))],
        out_specs=pl.BlockSpec((MOE_BLOCK, d), lambda i, j, be: (i, 0)),
        scratch_shapes=[pltpu.VMEM((MOE_BLOCK, d), F32)],
    )
    return pl.pallas_call(
        _moe_kernel,
        grid_spec=grid_spec,
        out_shape=jax.ShapeDtypeStruct((rows, d), F32),
        compiler_params=_cparams("parallel", "arbitrary"),
        name="moe_experts",
    )(block_e, xs, w_gu, w_gu, w_down)


def _final_kernel(pos_ref, x_ref, mod_ref, rt_ref, fn_ref, y_hbm, out_ref, ybuf, sem):
    tm = x_ref.shape[0]
    _issue_row_gather(pos_ref, TOP_K * tm, y_hbm, ybuf, sem)
    _wait_row_gather(y_hbm, ybuf, sem)
    rt = rt_ref[...]
    y = rt[:, 2:3] * ybuf[0:tm, :] + rt[:, 3:4] * ybuf[tm:2 * tm, :]
    xo = x_ref[...] + mod_ref[5:6, :] * y
    ms = jnp.mean(xo * xo, axis=-1, keepdims=True)
    out_ref[...] = xo * lax.rsqrt(ms + NORM_EPS) * fn_ref[...]


def _moe_combine_final(x, mod, route, pos, y, fn):
    b, t, d = x.shape
    tm = TOKEN_TILE
    nt = t // tm
    tok = lambda w: pl.BlockSpec((None, tm, w), lambda bi, i: (bi, i, 0))
    return pl.pallas_call(
        _final_kernel,
        grid=(b, nt),
        in_specs=[pl.BlockSpec((None, None, 1, TOP_K * tm), lambda bi, i: (bi, i, 0, 0), memory_space=pltpu.SMEM),
                  tok(d), pl.BlockSpec((None, 6, d), lambda bi, i: (bi, 0, 0)), tok(route.shape[-1]),
                  _full(fn), pl.BlockSpec(memory_space=pl.ANY)],
        out_specs=tok(d),
        out_shape=jax.ShapeDtypeStruct(x.shape, F32),
        scratch_shapes=[pltpu.VMEM((TOP_K * tm, d), F32), pltpu.SemaphoreType.DMA(())],
        compiler_params=_cparams("arbitrary", "arbitrary"),
        name="moe_combine_final",
    )(pos.reshape(b, nt, 1, TOP_K * tm), x, mod, route, fn, y)


def _moe_plan(top_e, n_tok):
    nk = n_tok * TOP_K
    flat_e = top_e.reshape(-1)
    onehot = (flat_e[:, None] == jnp.arange(N_EXPERTS, dtype=jnp.int32)[None, :]).astype(jnp.int32)
    rank = jnp.sum((jnp.cumsum(onehot, axis=0) - onehot) * onehot, axis=1)
    counts = jnp.sum(onehot, axis=0)
    padded = (counts + MOE_BLOCK - 1) // MOE_BLOCK * MOE_BLOCK
    pad_end = jnp.cumsum(padded)
    pad_start = pad_end - padded
    slot = (pad_start[flat_e] + rank).astype(jnp.int32)
    n_blocks = -(-nk // MOE_BLOCK) + N_EXPERTS
    rows = n_blocks * MOE_BLOCK
    flat_tok = jnp.repeat(jnp.arange(n_tok, dtype=jnp.int32), TOP_K)
    buf_tok = jnp.zeros((rows,), jnp.int32).at[slot].set(flat_tok)
    block_e = jnp.minimum(
        jnp.searchsorted(pad_end, jnp.arange(n_blocks, dtype=jnp.int32) * MOE_BLOCK, side="right"),
        N_EXPERTS - 1).astype(jnp.int32)
    return slot.reshape(n_tok, TOP_K), buf_tok.reshape(n_blocks, MOE_BLOCK), block_e


def _prep_even(w_in, w_uq, w_ukv, w_out):
    src, _, _, _ = _mla_layout()
    o = np.cumsum((0, MLA_Q_RANK, MLA_KV_RANK, MLA_ROPE, GDN_QKV, GDN_HEADS * GDN_DV, 2 * GDN_HEADS,
                   2 * GDN_HEADS))
    wcq, wckv = w_in[:, o[0]:o[1]], w_in[:, o[1]:o[2]]
    wkpe = _take_cols(w_in[:, o[2]:o[3]], np.where(src >= MLA_NOPE, src - MLA_NOPE, -1))
    wgqkv, wgz, wab = w_in[:, o[3]:o[4]], w_in[:, o[4]:o[5]], w_in[:, o[5]:o[7]]
    dq = MLA_NOPE + MLA_ROPE
    uq_idx = np.concatenate([np.where(src >= 0, h * dq + src, -1) for h in range(MLA_HEADS)])
    dkv = MLA_NOPE + MLA_V
    uk_idx = np.concatenate([np.where((src >= 0) & (src < MLA_NOPE), h * dkv + src, -1)
                             for h in range(MLA_HEADS)])
    uv_idx = np.concatenate([h * dkv + MLA_NOPE + np.arange(MLA_V) for h in range(MLA_HEADS)])
    wuq, wuk, wuv = _take_cols(w_uq, uq_idx), _take_cols(w_ukv, uk_idx), _take_cols(w_ukv, uv_idx)
    rows = -np.ones(MLA_HEADS * LANES, np.int64)
    for h in range(MLA_HEADS):
        base = h * LANES + (h % 2) * MLA_V
        rows[base:base + MLA_V] = h * MLA_V + np.arange(MLA_V)
    n_mla = MLA_HEADS * MLA_V
    wa = _take_cols(w_out[:n_mla].T, rows).T
    wg = w_out[n_mla:]
    ws = [w.astype(BF16) for w in (wcq, wckv, wkpe, wgqkv, wgz, wab)]
    return ws, [wuq.astype(BF16), wuk.astype(BF16), wuv.astype(BF16)], wa.astype(BF16), wg.astype(BF16)


def _prep_odd(w_in, q_norm, k_norm, w_out):
    m, src, _, _, _ = _pair_layout()
    nd = DIFF_HEADS * 2 * DIFF_DH
    o_dq, o_dk, o_dv, o_gq = 0, nd, 2 * nd, 3 * nd
    o_gk = o_gq + GQA_HEADS * GQA_DH
    o_gv = o_gk + GQA_KV_HEADS * GQA_DH
    pair = np.concatenate([h * 2 * DIFF_DH + m * DIFF_DH + src for h in range(DIFF_HEADS)])
    wdq = _take_cols(w_in, o_dq + pair)
    wdk = _take_cols(w_in, o_dk + pair)
    wdv = w_in[:, o_dv:o_gq]
    grp = GQA_HEADS // GQA_KV_HEADS
    gq_idx = np.concatenate([np.where(m == h // grp, o_gq + h * GQA_DH + src, -1) for h in range(GQA_HEADS)])
    wgq = _take_cols(w_in, gq_idx)
    wgk = _take_cols(w_in, o_gk + m * GQA_DH + src)
    wgv = w_in[:, o_gv:o_gv + GQA_KV_HEADS * GQA_DH]
    gqg = jnp.tile(q_norm[src], GQA_HEADS).reshape(1, -1)
    gkg = k_norm[src].reshape(1, -1)
    rows = -np.ones(GQA_HEADS * LANES, np.int64)
    for h in range(GQA_HEADS):
        base = h * LANES + (h // grp) * GQA_DH
        rows[base:base + GQA_DH] = h * GQA_DH + np.arange(GQA_DH)
    wd = w_out[:nd]
    wg = _take_cols(w_out[nd:].T, rows).T
    ws = [w.astype(BF16) for w in (wdq, wdk, wdv, wgq, wgk, wgv)]
    return ws, [gqg.astype(F32), gkg.astype(F32)], wd.astype(BF16), wg.astype(BF16)


def _layer_mods(m_layer, batch):
    d = m_layer.shape[1] // 6
    lat = m_layer[:batch].reshape(batch, 6, d)
    ctx = jnp.broadcast_to(m_layer[batch].reshape(1, 6, d), (batch, 6, d))
    return lat, ctx


def kernel(x, c, ctx, c_ctx, mod_w, mod_b, norm_g, ev_w_in, ev_mla_q_norm, ev_mla_kv_norm, ev_mla_w_uq,
           ev_mla_w_ukv, ev_gdn_conv, ev_gdn_a_log, ev_gdn_dt_bias, ev_gdn_out_norm, ev_w_out, ev_ffn_w_gu,
           ev_ffn_w_down, od_w_in, od_diff_lambda, od_diff_norm, od_gqa_q_norm, od_gqa_k_norm, od_w_out,
           od_router_w, od_moe_w_gu, od_moe_w_down, final_norm):
    batch, t_lat, d = x.shape
    t_ctx = ctx.shape[1]
    depth = mod_w.shape[0]
    assert depth == 2 and batch < 16

    cs = jnp.zeros((16, d), F32).at[:batch].set(c).at[batch].set(c_ctx)
    mods = _modulation(cs, mod_w, mod_b)

    j = 0
    mod_lat, mod_ctx = _layer_mods(mods[0], batch)
    g1 = norm_g[0, 0].reshape(1, d)
    g2 = norm_g[0, 1].reshape(1, d)
    ws, ups, wa, wg = _prep_even(ev_w_in[j], ev_mla_w_uq[j], ev_mla_w_ukv[j], ev_w_out[j])
    ws = ws + [ev_mla_q_norm[j].reshape(1, -1), ev_mla_kv_norm[j].reshape(1, -1)] + ups
    _, kind, freq, sign = _mla_layout()
    qscale = (MLA_NOPE + MLA_ROPE) ** -0.5 * LOG2E
    nfq = MLA_ROPE // 4
    tabs_lat = (_rope_tables(t_lat, kind, freq, sign, nfq, qscale, True)
                + _rope_tables(t_lat, kind, freq, sign, nfq, 1.0, True))
    tabs_ctx = (_rope_tables(t_ctx, kind, freq, sign, nfq, qscale, False)
                + _rope_tables(t_ctx, kind, freq, sign, nfq, 1.0, False))
    ql, kl, vl, gqkv_l, gz_l, ab_l = _even_proj(x, mod_lat, g1, tabs_lat, ws)
    qc, kc, vc, gqkv_c, gz_c, ab_c = _even_proj(ctx, mod_ctx, g1, tabs_ctx, ws)

    same = lambda hg: hg
    k_all = jnp.concatenate([kc, kl], axis=1)
    v_all = jnp.concatenate([vc, vl], axis=1)
    mla_kw = dict(groups=MLA_HEADS, g=1, kmap=same, vmap=lambda hg: hg // 2, out_dtype=BF16)
    ao_l = _flash(ql, k_all, v_all, **mla_kw)
    ao_c = _flash(qc, kc, vc, **mla_kw)

    nh2 = 2 * GDN_HEADS
    prm = jnp.zeros((2, 2 * nh2), F32)
    prm = prm.at[0, :nh2].set(ev_gdn_a_log[j].reshape(-1)).at[1, :nh2].set(ev_gdn_dt_bias[j].reshape(-1))
    eseg = _seg_ones(GDN_HEADS * GDN_DK, GDN_DK)
    prep_l = _gdn_prep(gqkv_l, ab_l, ev_gdn_conv[j], prm, eseg)
    prep_c = _gdn_prep(gqkv_c, ab_c, ev_gdn_conv[j], prm, eseg)
    go_c, go_l = _gdn_scan(prep_c, prep_l)

    on = jnp.tile(ev_gdn_out_norm[j], GDN_HEADS).reshape(1, -1)
    x = _mix_even(x, mod_lat, ao_l, go_l, gz_l, on, eseg, wa, wg)
    ctx = _mix_even(ctx, mod_ctx, ao_c, go_c, gz_c, on, eseg, wa, wg)
    w_gu, w_dn = ev_ffn_w_gu[j].astype(BF16), ev_ffn_w_down[j].astype(BF16)
    x = _ffn(x, mod_lat, g2, w_gu, w_dn)
    ctx = _ffn(ctx, mod_ctx, g2, w_gu, w_dn)

    lambda_init = 0.8 - 0.6 * math.exp(-0.3 * 1)
    mod_lat, mod_ctx = _layer_mods(mods[1], batch)
    g1 = norm_g[1, 0].reshape(1, d)
    g2 = norm_g[1, 1].reshape(1, d)
    ws, gains, wd, wgx = _prep_odd(od_w_in[j], od_gqa_q_norm[j], od_gqa_k_norm[j], od_w_out[j])
    _, _, kind, freq, sign = _pair_layout()
    nfq = DIFF_DH // 4
    tabs_lat = _rope_tables(t_lat, kind, freq, sign, nfq, 1.0, True)
    tabs_ctx = _rope_tables(t_ctx, kind, freq, sign, nfq, 1.0, False)
    qscale = DIFF_DH ** -0.5 * LOG2E
    dq, dk_l, dv_l, gq, gk_l, gv_l = _odd_proj(x, mod_lat, g1, tabs_lat, ws, gains, qscale)
    _, dk_c, dv_c, _, gk_c, gv_c = _odd_proj(ctx, mod_ctx, g1, tabs_ctx, ws, gains, qscale)
    cat = lambda a, b_: jnp.concatenate([a, b_], axis=1)
    do = _flash(dq, cat(dk_c, dk_l), cat(dv_c, dv_l), groups=DIFF_HEADS, g=2, kmap=same, vmap=same,
                out_dtype=F32)
    zero = lambda hg: 0
    go = _flash(gq, cat(gk_c, gk_l), cat(gv_c, gv_l), groups=1, g=GQA_HEADS, kmap=zero, vmap=zero,
                out_dtype=BF16)
    x = _mix_odd(x, mod_lat, do, go, od_diff_lambda[j], od_diff_norm[j].reshape(1, -1), wd, wgx, lambda_init)

    rw = jnp.zeros((d, LANES), F32).at[:, :N_EXPERTS].set(od_router_w[j])
    h2, route = _router(x, mod_lat, g2, rw)
    n_tok = batch * t_lat
    top_e = route[..., 0:TOP_K].astype(jnp.int32).reshape(n_tok, TOP_K)
    slot, buf_tok, block_e = _moe_plan(top_e, n_tok)
    xs = _gather_rows(h2.reshape(n_tok, d), buf_tok, BF16)
    y = _moe_experts(xs, block_e, od_moe_w_gu[j].astype(BF16), od_moe_w_down[j].astype(BF16))
    tm = TOKEN_TILE
    pos = slot.reshape(batch, t_lat // tm, tm, TOP_K).transpose(0, 1, 3, 2).reshape(batch, t_lat // tm, TOP_K * tm)
    return _moe_combine_final(x, mod_lat, route, pos, y, final_norm.reshape(1, d))
```

```python
import functools
import math

import numpy as np
import jax
import jax.numpy as jnp
from jax import lax
from jax.experimental import pallas as pl
from jax.experimental.pallas import tpu as pltpu

F32 = jnp.float32
BF16 = jnp.bfloat16

GRID_W = 64
ROPE_BASE = 10000.0
NORM_EPS = 1e-6
MLA_HEADS, MLA_NOPE, MLA_ROPE, MLA_V = 8, 64, 32, 64
MLA_Q_RANK, MLA_KV_RANK = 384, 256
GDN_HEADS, GDN_DK, GDN_DV, GDN_CONV, GDN_CHUNK = 8, 64, 64, 5, 64
GDN_QKV = GDN_HEADS * (2 * GDN_DK + GDN_DV)
DIFF_HEADS, DIFF_DH = 4, 64
GQA_HEADS, GQA_KV_HEADS, GQA_DH = 8, 2, 64
N_EXPERTS, TOP_K, MOE_BLOCK = 8, 2, 512

LANES = 128
SUBLANES = 8
VMEM_LIMIT = 48 * 1024 * 1024
LOG2E = 1.4426950408889634

TOKEN_TILE = 256
ATTN_ROWS = 512
ATTN_KEYS = 256


def _cparams(*sem):
    return pltpu.CompilerParams(dimension_semantics=sem, vmem_limit_bytes=VMEM_LIMIT)


def _dot(a, b):
    return jnp.dot(a, b, preferred_element_type=F32)


def _dot_t(a, b):
    return lax.dot_general(a, b, (((1,), (1,)), ((), ())), preferred_element_type=F32)


def _split3(x):
    h1 = x.astype(BF16)
    r1 = x - h1.astype(F32)
    h2 = r1.astype(BF16)
    h3 = (r1 - h2.astype(F32)).astype(BF16)
    return h1, h2, h3


def _silu(x):
    return x * jax.nn.sigmoid(x)


def _rms_mod(x, g, sc, sh):
    ms = jnp.mean(x * x, axis=-1, keepdims=True)
    return (x * lax.rsqrt(ms + NORM_EPS) * g) * (1.0 + sc) + sh


def _seg_sumsq(x, eseg):
    sq = x * x
    hi = sq.astype(BF16)
    lo = (sq - hi.astype(F32)).astype(BF16)
    return _dot(hi, eseg) + _dot(lo, eseg)


def _rope(x, c, s):
    return x * c + pltpu.roll(x, LANES // 2, 1) * s


def _mla_layout():
    src = -np.ones(LANES, np.int64)
    kind = np.zeros(LANES, np.int64)
    freq = np.zeros(LANES, np.int64)
    sign = np.zeros(LANES, np.float32)
    nf = MLA_ROPE // 4
    for f in range(nf):
        src[f] = MLA_NOPE + f; kind[f] = 1; freq[f] = f; sign[f] = -1.0
        src[nf + f] = MLA_NOPE + 2 * nf + f; kind[nf + f] = 2; freq[nf + f] = f; sign[nf + f] = -1.0
        src[64 + f] = MLA_NOPE + nf + f; kind[64 + f] = 1; freq[64 + f] = f; sign[64 + f] = 1.0
        src[64 + nf + f] = MLA_NOPE + 3 * nf + f; kind[64 + nf + f] = 2; freq[64 + nf + f] = f
        sign[64 + nf + f] = 1.0
    src[16:64] = np.arange(0, 48)
    src[80:96] = np.arange(48, 64)
    return src, kind, freq, sign


def _pair_layout():
    lane = np.arange(LANES)
    region = lane // 32
    m = region % 2
    is_b = region // 2
    within = lane % 32
    is_col = within // 16
    f = within % 16
    src = is_col * 32 + is_b * 16 + f
    kind = 1 + is_col
    sign = np.where(is_b == 0, -1.0, 1.0).astype(np.float32)
    return m, src, kind, f, sign


def _rope_tables(t_len, kind, freq, sign, n_freq, scale, with_pos):
    kind_j = jnp.asarray(kind)
    if not with_pos:
        c = jnp.full((t_len, LANES), scale, F32)
        return c, jnp.zeros((t_len, LANES), F32)
    n_rows = t_len // GRID_W
    inv = 1.0 / (ROPE_BASE ** (jnp.arange(n_freq, dtype=F32) / n_freq))
    rows = jnp.repeat(jnp.arange(n_rows, dtype=F32), GRID_W)
    cols = jnp.tile(jnp.arange(GRID_W, dtype=F32), n_rows)
    ang_r = (rows[:, None] * inv)[:, freq]
    ang_c = (cols[:, None] * inv)[:, freq]
    ang = jnp.where(kind_j[None, :] == 1, ang_r, ang_c)
    has = (kind_j > 0)[None, :]
    c = jnp.where(has, jnp.cos(ang), 1.0) * scale
    s = jnp.where(has, jnp.sin(ang) * jnp.asarray(sign)[None, :], 0.0) * scale
    return c.astype(F32), s.astype(F32)


def _take_cols(w, idx):
    idx = np.asarray(idx)
    wz = jnp.concatenate([w, jnp.zeros((w.shape[0], 1), w.dtype)], axis=1)
    return wz[:, np.where(idx < 0, w.shape[1], idx)]


def _seg_ones(width, seg):
    r = np.arange(width)
    return jnp.asarray((r[:, None] // seg) == (r[None, :] // seg), dtype=BF16)


def _mod_kernel(c_ref, w_ref, b_ref, o_ref):
    c = c_ref[...]
    s1, s2, s3 = _split3(_silu(c))
    w1, w2, _ = _split3(w_ref[...])
    acc = _dot(s1, w1) + _dot(s1, w2) + _dot(s2, w1) + _dot(s2, w2) + _dot(s3, w1)
    o_ref[...] = acc + b_ref[...]


def _modulation(cs, mod_w, mod_b):
    depth, d, n = mod_w.shape
    tn = 512
    return pl.pallas_call(
        _mod_kernel,
        grid=(depth, n // tn),
        in_specs=[
            pl.BlockSpec(cs.shape, lambda l, j: (0, 0)),
            pl.BlockSpec((None, d, tn), lambda l, j: (l, 0, j)),
            pl.BlockSpec((None, 1, tn), lambda l, j: (l, 0, j)),
        ],
        out_specs=pl.BlockSpec((None, cs.shape[0], tn), lambda l, j: (l, 0, j)),
        out_shape=jax.ShapeDtypeStruct((depth, cs.shape[0], n), F32),
        compiler_params=_cparams("parallel", "parallel"),
        name="modulation",
    )(cs, mod_w, mod_b.reshape(depth, 1, n))


def _even_proj_kernel(x_ref, mod_ref, g_ref, cq_t, sq_t, ck_t, sk_t,
                      wcq, wckv, wkpe, wgqkv, wgz, wab, qn, kvn, wuq, wuk, wuv,
                      q_out, k_out, v_out, gqkv_out, gz_out, ab_out):
    h = _rms_mod(x_ref[...], g_ref[...], mod_ref[1:2, :], mod_ref[0:1, :]).astype(BF16)
    cq = _dot(h, wcq[...])
    ckv = _dot(h, wckv[...])
    kpe = _dot(h, wkpe[...])
    nq = (cq * lax.rsqrt(jnp.mean(cq * cq, axis=-1, keepdims=True) + NORM_EPS) * qn[...]).astype(BF16)
    nkv = (ckv * lax.rsqrt(jnp.mean(ckv * ckv, axis=-1, keepdims=True) + NORM_EPS) * kvn[...]).astype(BF16)
    q = _dot(nq, wuq[...])
    kn = _dot(nkv, wuk[...])
    kper = _rope(kpe, ck_t[...], sk_t[...])
    cqv, sqv = cq_t[...], sq_t[...]
    for hd in range(MLA_HEADS):
        sl = slice(hd * LANES, (hd + 1) * LANES)
        q_out[:, sl] = _rope(q[:, sl], cqv, sqv).astype(BF16)
        k_out[:, sl] = (kn[:, sl] + kper).astype(BF16)
    _store_vt(v_out, _dot_t(wuv[...], nkv))
    gqkv_out[...] = _dot(h, wgqkv[...])
    gz_out[...] = _dot(h, wgz[...])
    ab_out[...] = _dot(h, wab[...])


def _full(a):
    nd = a.ndim
    return pl.BlockSpec(a.shape, lambda *_: (0,) * nd)


def _store_vt(vt_out, vt):
    for n in range(vt_out.shape[0]):
        vt_out[n] = vt[n * LANES:(n + 1) * LANES, :].astype(vt_out.dtype)


def _vt_spec(groups, tm):
    return pl.BlockSpec((None, groups, None, LANES, tm), lambda bi, i: (bi, 0, i, 0, 0))


def _vt_shape(b, groups, t, tm):
    assert tm == ATTN_KEYS
    return jax.ShapeDtypeStruct((b, groups, t // tm, LANES, tm), BF16)


def _even_proj(x, mod, g, tabs, ws):
    b, t, d = x.shape
    tm = TOKEN_TILE
    tok = lambda w: pl.BlockSpec((None, tm, w), lambda bi, i: (bi, i, 0))
    tab = pl.BlockSpec((tm, LANES), lambda bi, i: (i, 0))
    nvg = MLA_HEADS * MLA_V // LANES
    widths = (MLA_HEADS * LANES, MLA_HEADS * LANES, None, GDN_QKV, GDN_HEADS * GDN_DV, 4 * GDN_HEADS)
    dts = (BF16, BF16, None, F32, F32, F32)
    return pl.pallas_call(
        _even_proj_kernel,
        grid=(b, t // tm),
        in_specs=[tok(d), pl.BlockSpec((None, 6, d), lambda bi, i: (bi, 0, 0)), _full(g),
                  tab, tab, tab, tab] + [_full(w) for w in ws],
        out_specs=[_vt_spec(nvg, tm) if w is None else tok(w) for w in widths],
        out_shape=[_vt_shape(b, nvg, t, tm) if w is None else jax.ShapeDtypeStruct((b, t, w), dt)
                   for w, dt in zip(widths, dts)],
        compiler_params=_cparams("parallel", "parallel"),
        name="even_proj",
    )(x, mod, g, *tabs, *ws)


def _flash_kernel(q_ref, k_ref, vt_ref, eye_ref, o_ref, acc_sc, *, g, tq, tk, nsub, unroll, split_out):
    q = jnp.concatenate([q_ref[:, i * LANES:(i + 1) * LANES] for i in range(g)], axis=0)
    rows = g * tq
    acc_sc[...] = jnp.zeros(acc_sc.shape, F32)

    def sub(jj, m, l):
        k = k_ref[pl.ds(pl.multiple_of(jj * tk, tk), tk), :]
        st = _dot_t(k, q)
        m_new = jnp.maximum(m, jnp.max(st, axis=0, keepdims=True))
        p = jnp.exp2(st - m_new)
        alpha = jnp.exp2(m - m_new)
        l = alpha * l + jnp.sum(p, axis=0, keepdims=True)
        acc_sc[...] = alpha * acc_sc[...] + _dot(vt_ref[jj], p.astype(BF16))
        return m_new, l

    def body(j, carry):
        m, l = carry
        for u in range(unroll):
            m, l = sub(j * unroll + u, m, l)
        return m, l

    m0 = jnp.full((1, rows), -jnp.inf, F32)
    l0 = jnp.zeros((1, rows), F32)
    _, l = lax.fori_loop(0, nsub // unroll, body, (m0, l0))
    on = acc_sc[...] * (1.0 / l)
    eye = eye_ref[...]
    hi = on.astype(BF16)
    o = _dot_t(eye, hi)
    if split_out:
        o = o + _dot_t(eye, (on - hi.astype(F32)).astype(BF16))
    for i in range(g):
        o_ref[:, i * LANES:(i + 1) * LANES] = o[i * tq:(i + 1) * tq, :].astype(o_ref.dtype)


def _flash(q, k, vt, *, groups, g, kmap, vmap, out_dtype):
    b, tq_total, _ = q.shape
    tk_total = k.shape[1]
    tq = min(ATTN_ROWS // g, tq_total)
    tk = ATTN_KEYS
    nsub = tk_total // tk
    assert vt.shape[2:] == (nsub, LANES, tk)
    unroll = 3 if nsub % 3 == 0 else 1
    rows = g * tq
    eye = jnp.eye(rows, dtype=BF16)
    kern = functools.partial(_flash_kernel, g=g, tq=tq, tk=tk, nsub=nsub, unroll=unroll,
                             split_out=(out_dtype == F32))
    return pl.pallas_call(
        kern,
        grid=(b, groups, tq_total // tq),
        in_specs=[
            pl.BlockSpec((None, tq, g * LANES), lambda bi, hg, i: (bi, i, hg)),
            pl.BlockSpec((None, tk_total, LANES), lambda bi, hg, i: (bi, 0, kmap(hg))),
            pl.BlockSpec((None, None, nsub, LANES, tk), lambda bi, hg, i: (bi, vmap(hg), 0, 0, 0)),
            pl.BlockSpec((rows, rows), lambda bi, hg, i: (0, 0)),
        ],
        out_specs=pl.BlockSpec((None, tq, g * LANES), lambda bi, hg, i: (bi, i, hg)),
        out_shape=jax.ShapeDtypeStruct(q.shape, out_dtype),
        scratch_shapes=[pltpu.VMEM((LANES, rows), F32)],
        compiler_params=_cparams("parallel", "parallel", "parallel"),
        name="flash_attention",
    )(q, k, vt, eye)


def _gdn_prep_kernel(x_ref, prev_ref, next_ref, ab_ref, cw_ref, prm_ref, eseg_ref,
                     lf_ref, lb_ref, la_ref, q_out, k_out, v_out, s1_out, s2_out, xe_sc):
    i = pl.program_id(1)
    tm = x_ref.shape[0]
    halo = SUBLANES
    xe_sc[0:halo, :] = prev_ref[...] * (i > 0).astype(F32)
    xe_sc[halo:halo + tm, :] = x_ref[...]
    xe_sc[halo + tm:2 * halo + tm, :] = next_ref[...] * (i < pl.num_programs(1) - 1).astype(F32)
    acc = jnp.zeros(x_ref.shape, F32)
    for j in range(GDN_CONV):
        acc = acc + cw_ref[j:j + 1, :] * xe_sc[pl.ds(halo - GDN_CONV // 2 + j, tm), :]
    y = _silu(acc)
    hw = GDN_HEADS * GDN_DK
    eseg = eseg_ref[...]
    q = y[:, 0:hw]
    k = y[:, hw:2 * hw]
    q_out[...] = q * lax.rsqrt(_seg_sumsq(q, eseg) + NORM_EPS) * (GDN_DK ** -0.5)
    k_out[...] = k * lax.rsqrt(_seg_sumsq(k, eseg) + NORM_EPS)
    v_out[...] = y[:, 2 * hw:]

    ab = ab_ref[...]
    nh2 = 2 * GDN_HEADS
    lane = lax.broadcasted_iota(jnp.int32, ab.shape, 1)
    z = ab + prm_ref[1:2, :]
    softplus = jnp.maximum(z, 0.0) + jnp.log1p(jnp.exp(-jnp.abs(z)))
    gate = jnp.where(lane < nh2, -jnp.exp(prm_ref[0:1, :]) * softplus, 0.0)
    g1, g2, g3 = _split3(gate)
    lf, lb, la = lf_ref[...], lb_ref[...], la_ref[...]
    cum_f = _dot(lf, g1) + _dot(lf, g2) + _dot(lf, g3)
    cum_b = _dot(lb, g1) + _dot(lb, g2) + _dot(lb, g3)
    tot = _dot(la, g1) + _dot(la, g2) + _dot(la, g3)
    cum = jnp.where(lane < GDN_HEADS, cum_f, cum_b)
    s1_out[...] = jnp.where(lane < nh2, cum, jax.nn.sigmoid(ab))
    s2_out[...] = tot


def _gdn_prep(gqkv, ab, conv_w, prm, eseg):
    b, t, w = gqkv.shape
    tm = TOKEN_TILE
    nt8 = t // SUBLANES
    per = tm // SUBLANES
    r = np.arange(tm)
    same = (r[:, None] // GDN_CHUNK) == (r[None, :] // GDN_CHUNK)
    lf = jnp.asarray(same & (r[None, :] <= r[:, None]), dtype=BF16)
    lb = jnp.asarray(same & (r[None, :] >= r[:, None]), dtype=BF16)
    la = jnp.asarray(same, dtype=BF16)
    halo = gqkv.reshape(b, nt8, SUBLANES, w)
    tok = lambda wd: pl.BlockSpec((None, tm, wd), lambda bi, i: (bi, i, 0))
    hw = GDN_HEADS * GDN_DK
    nab = ab.shape[-1]
    return pl.pallas_call(
        _gdn_prep_kernel,
        grid=(b, t // tm),
        in_specs=[
            tok(w),
            pl.BlockSpec((None, None, SUBLANES, w), lambda bi, i: (bi, jnp.maximum(i * per - 1, 0), 0, 0)),
            pl.BlockSpec((None, None, SUBLANES, w), lambda bi, i: (bi, jnp.minimum((i + 1) * per, nt8 - 1), 0, 0)),
            tok(nab), _full(conv_w), _full(prm), _full(eseg), _full(lf), _full(lb), _full(la),
        ],
        out_specs=[tok(hw), tok(hw), tok(GDN_HEADS * GDN_DV), tok(nab), tok(nab)],
        out_shape=[jax.ShapeDtypeStruct((b, t, hw), F32), jax.ShapeDtypeStruct((b, t, hw), F32),
                   jax.ShapeDtypeStruct((b, t, GDN_HEADS * GDN_DV), F32),
                   jax.ShapeDtypeStruct((b, t, nab), F32), jax.ShapeDtypeStruct((b, t, nab), F32)],
        scratch_shapes=[pltpu.VMEM((tm + 2 * SUBLANES, w), F32)],
        compiler_params=_cparams("parallel", "parallel"),
        name="gdn_prep",
    )(gqkv, halo, halo, ab, conv_w, prm, eseg, lf, lb, la)


def _bmm(a, b):
    return jnp.einsum("nij,njk->nik", a, b, preferred_element_type=F32)


def _bmm_t(a, b):
    return jnp.einsum("nik,njk->nij", a, b, preferred_element_type=F32)


def _gdn_intra_kernel(q_ref, k_ref, v_ref, sc_ref, gr_ref, a_out, b_out, qe_out, o0_out, *, hb, cb):
    c_len = GDN_CHUNK
    n = hb * cb
    sgn = 1 - 2 * pl.program_id(0)
    row = lax.broadcasted_iota(jnp.int32, (c_len, c_len), 0)
    col = lax.broadcasted_iota(jnp.int32, (c_len, c_len), 1)
    ahead = (row - col) * sgn
    eye_f = (row == col).astype(F32)
    eye_b = jnp.broadcast_to(eye_f.astype(BF16), (n, c_len, c_len))
    q = q_ref[...].reshape(n, c_len, GDN_DK)
    k = k_ref[...].reshape(n, c_len, GDN_DK)
    v = v_ref[...].reshape(n, c_len, GDN_DV)
    scv = sc_ref[...].reshape(n, c_len, sc_ref.shape[-1])
    gc, beta, gl = scv[:, :, 0:1], scv[:, :, 1:2], scv[:, :, 2:3]
    g_cols = gr_ref[...].reshape(n, 1, c_len)
    eg, ek, cd = jnp.exp(gc), jnp.exp(gl - gc), jnp.exp(gl)
    decay = jnp.where(ahead >= 0, jnp.exp(jnp.minimum(gc - g_cols, 0.0)), 0.0)
    kb = k * beta
    kbf = k.astype(BF16)
    low = jnp.where(ahead > 0, _bmm_t(kb.astype(BF16), kbf) * decay, 0.0)
    qk = _bmm_t(q.astype(BF16), kbf) * decay
    mpow = jnp.where((row // 8) == (col // 8), -low, 0.0)
    tinv = eye_f + mpow
    for _ in range(2):
        mb = mpow.astype(BF16)
        mpow = _bmm(mb, mb)
        tinv = tinv + _bmm(tinv.astype(BF16), mpow.astype(BF16))
    for s in (8, 16, 32):
        off = ((row // (2 * s)) == (col // (2 * s))) & ((row // s) != (col // s))
        tb = tinv.astype(BF16)
        tinv = tinv - _bmm(_bmm(tb, jnp.where(off, low, 0.0).astype(BF16)).astype(BF16), tb)
    tb = tinv.astype(BF16)
    ub = _bmm(tb, (v * beta).astype(BF16)).astype(BF16)
    wb = _bmm(tb, (kb * eg).astype(BF16)).astype(BF16)
    kdt = _bmm_t(eye_b, (k * ek).astype(BF16)).astype(BF16)
    a_out[...] = (cd * eye_f - _bmm(kdt, wb)).reshape(a_out.shape)
    b_out[...] = _bmm(kdt, ub).reshape(b_out.shape)
    qkb = qk.astype(BF16)
    qe_out[...] = (q * eg - _bmm(qkb, wb)).reshape(qe_out.shape)
    o0_out[...] = _bmm(qkb, ub).reshape(o0_out.shape)


def _gdn_intra(q, k, v, sc, gr):
    b, h, tt, dk = q.shape
    nc = tt // GDN_CHUNK
    hb, cb = h, 2
    rows = cb * GDN_CHUNK
    qkv = pl.BlockSpec((None, hb, rows, dk), lambda d, bi, c: (bi, 0, c, 0))
    tokb = lambda wd: pl.BlockSpec((None, None, hb, rows, wd), lambda d, bi, c: (d, bi, 0, c, 0))
    matb = pl.BlockSpec((None, None, hb, cb, dk, dk), lambda d, bi, c: (d, bi, 0, c, 0, 0))
    return pl.pallas_call(
        functools.partial(_gdn_intra_kernel, hb=hb, cb=cb),
        grid=(2, b, nc // cb),
        in_specs=[qkv, qkv, qkv, tokb(sc.shape[-1]),
                  pl.BlockSpec((None, None, hb, cb, 1, GDN_CHUNK), lambda d, bi, c: (d, bi, 0, c, 0, 0))],
        out_specs=[matb, matb, tokb(dk), tokb(dk)],
        out_shape=[jax.ShapeDtypeStruct((2, b, h, nc, dk, dk), F32)] * 2
                  + [jax.ShapeDtypeStruct((2, b, h, tt, dk), F32)] * 2,
        compiler_params=_cparams("parallel", "parallel", "parallel"),
        name="gdn_intra",
    )(q, k, v, sc, gr)


def _gdn_inter_kernel(af, bf, qf, of, ab, bb, qb, ob, o_f, o_b, s_sc):
    @pl.when(pl.program_id(1) == 0)
    def _():
        s_sc[...] = jnp.zeros(s_sc.shape, F32)

    for d, (a_ref, b_ref, qe_ref, o0_ref, o_ref) in enumerate(((af, bf, qf, of, o_f), (ab, bb, qb, ob, o_b))):
        sb = s_sc[d].astype(BF16)
        o_ref[...] = _bmm(qe_ref[...].astype(BF16), sb) + o0_ref[...]
        s_sc[d] = _bmm(a_ref[...].astype(BF16), sb) + b_ref[...]


def _gdn_inter(a, bm, qe, o0, nc_ctx):
    _, b, h, nc, dk, _ = a.shape
    rev = lambda c: jnp.where(c < nc_ctx, nc_ctx - 1 - c, nc - 1 - (c - nc_ctx))
    mat_f = pl.BlockSpec((None, None, h, None, dk, dk), lambda bi, c: (0, bi, 0, c, 0, 0))
    mat_b = pl.BlockSpec((None, None, h, None, dk, dk), lambda bi, c: (1, bi, 0, rev(c), 0, 0))
    tok_f = pl.BlockSpec((None, None, h, GDN_CHUNK, dk), lambda bi, c: (0, bi, 0, c, 0))
    tok_b = pl.BlockSpec((None, None, h, GDN_CHUNK, dk), lambda bi, c: (1, bi, 0, rev(c), 0))
    out_f = pl.BlockSpec((None, h, GDN_CHUNK, dk), lambda bi, c: (bi, 0, c, 0))
    out_b = pl.BlockSpec((None, h, GDN_CHUNK, dk), lambda bi, c: (bi, 0, rev(c), 0))
    osd = jax.ShapeDtypeStruct(qe.shape[1:], F32)
    return pl.pallas_call(
        _gdn_inter_kernel,
        grid=(b, nc),
        in_specs=[mat_f, mat_f, tok_f, tok_f, mat_b, mat_b, tok_b, tok_b],
        out_specs=[out_f, out_b],
        out_shape=[osd, osd],
        scratch_shapes=[pltpu.VMEM((2, h, dk, dk), F32)],
        compiler_params=_cparams("parallel", "arbitrary"),
        name="gdn_inter",
    )(a, bm, qe, o0, a, bm, qe, o0)


def _gdn_scan(prep_ctx, prep_lat):
    b, tc, _ = prep_ctx[0].shape
    nh = GDN_HEADS
    cat = [jnp.concatenate([c, l], axis=1) for c, l in zip(prep_ctx, prep_lat)]
    tt = cat[0].shape[1]
    q, k, v = (x.reshape(b, tt, nh, -1).transpose(0, 2, 1, 3) for x in cat[:3])
    s1, s2 = cat[3], cat[4]
    sc = jnp.stack([
        jnp.stack([s1[:, :, d * nh:(d + 1) * nh], s1[:, :, (2 + d) * nh:(3 + d) * nh],
                   s2[:, :, d * nh:(d + 1) * nh]], axis=-1)
        for d in range(2)])
    sc = sc.transpose(0, 1, 3, 2, 4)
    gr = sc[..., 0].reshape(2, b, nh, tt // GDN_CHUNK, 1, GDN_CHUNK)
    sc = jnp.pad(sc, ((0, 0),) * 4 + ((0, SUBLANES - 3),))
    a, bm, qe, o0 = _gdn_intra(q, k, v, sc, gr)
    o_f, o_b = _gdn_inter(a, bm, qe, o0, tc // GDN_CHUNK)
    o = (o_f + o_b).transpose(0, 2, 1, 3).reshape(b, tt, nh * GDN_DV)
    return o[:, :tc], o[:, tc:]


def _mix_even_kernel(x_ref, mod_ref, ao_ref, go_ref, gz_ref, on_ref, eseg_ref, wa_ref, wg_ref, out_ref):
    o = go_ref[...]
    ms = _seg_sumsq(o, eseg_ref[...]) * (1.0 / GDN_DV)
    y = o * lax.rsqrt(ms + NORM_EPS) * on_ref[...] * _silu(gz_ref[...])
    mix = _dot(ao_ref[...], wa_ref[...]) + _dot(y.astype(BF16), wg_ref[...])
    out_ref[...] = x_ref[...] + mod_ref[2:3, :] * mix


def _mix_even(x, mod, ao, go, gz, on, eseg, wa, wg):
    b, t, d = x.shape
    tm = TOKEN_TILE
    tok = lambda w: pl.BlockSpec((None, tm, w), lambda bi, i: (bi, i, 0))
    return pl.pallas_call(
        _mix_even_kernel,
        grid=(b, t // tm),
        in_specs=[tok(d), pl.BlockSpec((None, 6, d), lambda bi, i: (bi, 0, 0)), tok(ao.shape[-1]),
                  tok(go.shape[-1]), tok(gz.shape[-1]), _full(on), _full(eseg), _full(wa), _full(wg)],
        out_specs=tok(d),
        out_shape=jax.ShapeDtypeStruct(x.shape, F32),
        compiler_params=_cparams("parallel", "parallel"),
        name="mix_even",
    )(x, mod, ao, go, gz, on, eseg, wa, wg)


def _ffn_kernel(x_ref, mod_ref, g_ref, wg_ref, wu_ref, wd_ref, out_ref, h_sc, acc_sc):
    j = pl.program_id(2)

    @pl.when(j == 0)
    def _():
        h_sc[...] = _rms_mod(x_ref[...], g_ref[...], mod_ref[4:5, :], mod_ref[3:4, :]).astype(BF16)
        acc_sc[...] = jnp.zeros(acc_sc.shape, F32)

    h = h_sc[...]
    act = (_silu(_dot(h, wg_ref[...])) * _dot(h, wu_ref[...])).astype(BF16)
    acc_sc[...] += _dot(act, wd_ref[...])

    @pl.when(j == pl.num_programs(2) - 1)
    def _():
        out_ref[...] = x_ref[...] + mod_ref[5:6, :] * acc_sc[...]


def _ffn_tile(f):
    for cand in (1408, 1792, 1024, 896, 768, 512, 256, 128):
        if f % cand == 0:
            return cand
    raise ValueError(f"ffn width {f} is not a multiple of 128")


def _ffn(x, mod, g, w_gu, w_down):
    b, t, d = x.shape
    f = w_down.shape[0]
    tm = 512 if t % 512 == 0 else TOKEN_TILE
    tf = _ffn_tile(f)
    nf = f // tf
    tok = pl.BlockSpec((None, tm, d), lambda bi, i, j: (bi, i, 0))
    return pl.pallas_call(
        _ffn_kernel,
        grid=(b, t // tm, nf),
        in_specs=[tok, pl.BlockSpec((None, 6, d), lambda bi, i, j: (bi, 0, 0)),
                  pl.BlockSpec(g.shape, lambda bi, i, j: (0, 0)),
                  pl.BlockSpec((d, tf), lambda bi, i, j: (0, j)),
                  pl.BlockSpec((d, tf), lambda bi, i, j: (0, nf + j)),
                  pl.BlockSpec((tf, d), lambda bi, i, j: (j, 0))],
        out_specs=tok,
        out_shape=jax.ShapeDtypeStruct(x.shape, F32),
        scratch_shapes=[pltpu.VMEM((tm, d), BF16), pltpu.VMEM((tm, d), F32)],
        compiler_params=_cparams("parallel", "parallel", "arbitrary"),
        name="ffn",
    )(x, mod, g, w_gu, w_gu, w_down)


def _odd_proj_kernel(x_ref, mod_ref, g_ref, c_t, s_t, wdq, wdk, wdv, wgq, wgk, wgv, gqg, gkg,
                     dq_out, dk_out, dv_out, gq_out, gk_out, gv_out, *, qscale):
    h = _rms_mod(x_ref[...], g_ref[...], mod_ref[1:2, :], mod_ref[0:1, :]).astype(BF16)
    c, s = c_t[...], s_t[...]
    lane = lax.broadcasted_iota(jnp.int32, c.shape, 1)
    is_m1 = ((lane // 32) % 2) == 1
    dq = _dot(h, wdq[...])
    dk = _dot(h, wdk[...])
    for hd in range(DIFF_HEADS):
        sl = slice(hd * LANES, (hd + 1) * LANES)
        r = _rope(dq[:, sl], c, s) * qscale
        dq_out[:, 2 * hd * LANES:(2 * hd + 1) * LANES] = jnp.where(is_m1, 0.0, r).astype(BF16)
        dq_out[:, (2 * hd + 1) * LANES:(2 * hd + 2) * LANES] = jnp.where(is_m1, r, 0.0).astype(BF16)
        dk_out[:, sl] = _rope(dk[:, sl], c, s).astype(BF16)
    _store_vt(dv_out, _dot_t(wdv[...], h))
    gq = _dot(h, wgq[...])
    gqgv = gqg[...]
    inv_dh = 1.0 / GQA_DH
    for hd in range(GQA_HEADS):
        sl = slice(hd * LANES, (hd + 1) * LANES)
        xh = gq[:, sl]
        ms = jnp.sum(xh * xh, axis=-1, keepdims=True) * inv_dh
        xn = xh * lax.rsqrt(ms + NORM_EPS) * gqgv[:, sl]
        gq_out[:, sl] = (_rope(xn, c, s) * qscale).astype(BF16)
    gk = _dot(h, wgk[...])
    sq = gk * gk
    ms0 = jnp.sum(jnp.where(is_m1, 0.0, sq), axis=-1, keepdims=True) * inv_dh
    ms1 = jnp.sum(jnp.where(is_m1, sq, 0.0), axis=-1, keepdims=True) * inv_dh
    rs = jnp.where(is_m1, lax.rsqrt(ms1 + NORM_EPS), lax.rsqrt(ms0 + NORM_EPS))
    gk_out[...] = _rope(gk * rs * gkg[...], c, s).astype(BF16)
    _store_vt(gv_out, _dot_t(wgv[...], h))


def _odd_proj(x, mod, g, tabs, ws, gains, qscale):
    b, t, d = x.shape
    tm = TOKEN_TILE
    tok = lambda w: pl.BlockSpec((None, tm, w), lambda bi, i: (bi, i, 0))
    tab = pl.BlockSpec((tm, LANES), lambda bi, i: (i, 0))
    outs = ((2 * DIFF_HEADS * LANES, 0), (DIFF_HEADS * LANES, 0), (0, DIFF_HEADS),
            (GQA_HEADS * LANES, 0), (LANES, 0), (0, 1))
    return pl.pallas_call(
        functools.partial(_odd_proj_kernel, qscale=qscale),
        grid=(b, t // tm),
        in_specs=[tok(d), pl.BlockSpec((None, 6, d), lambda bi, i: (bi, 0, 0)), _full(g), tab, tab]
                 + [_full(w) for w in ws] + [_full(w) for w in gains],
        out_specs=[_vt_spec(n, tm) if n else tok(w) for w, n in outs],
        out_shape=[_vt_shape(b, n, t, tm) if n else jax.ShapeDtypeStruct((b, t, w), BF16) for w, n in outs],
        compiler_params=_cparams("parallel", "parallel"),
        name="odd_proj",
    )(x, mod, g, *tabs, *ws, *gains)


def _mix_odd_kernel(x_ref, mod_ref, do_ref, go_ref, lam_ref, dn_ref, wd_ref, wg_ref, out_ref, *, lambda_init):
    lp = lam_ref[...]
    lam = (jnp.exp(jnp.sum(lp[0:1, :] * lp[1:2, :], axis=-1, keepdims=True))
           - jnp.exp(jnp.sum(lp[2:3, :] * lp[3:4, :], axis=-1, keepdims=True)) + lambda_init)
    dn = dn_ref[...]
    parts = []
    for hd in range(DIFF_HEADS):
        d0 = do_ref[:, 2 * hd * LANES:(2 * hd + 1) * LANES]
        d1 = do_ref[:, (2 * hd + 1) * LANES:(2 * hd + 2) * LANES]
        dd = d0 - lam * d1
        ms = jnp.mean(dd * dd, axis=-1, keepdims=True)
        parts.append(((dd * lax.rsqrt(ms + NORM_EPS) * dn) * (1.0 - lambda_init)).astype(BF16))
    dcat = jnp.concatenate(parts, axis=1)
    mix = _dot(dcat, wd_ref[...]) + _dot(go_ref[...], wg_ref[...])
    out_ref[...] = x_ref[...] + mod_ref[2:3, :] * mix


def _mix_odd(x, mod, do, go, lam_p, dn, wd, wg, lambda_init):
    b, t, d = x.shape
    tm = TOKEN_TILE
    tok = lambda w: pl.BlockSpec((None, tm, w), lambda bi, i: (bi, i, 0))
    return pl.pallas_call(
        functools.partial(_mix_odd_kernel, lambda_init=lambda_init),
        grid=(b, t // tm),
        in_specs=[tok(d), pl.BlockSpec((None, 6, d), lambda bi, i: (bi, 0, 0)), tok(do.shape[-1]),
                  tok(go.shape[-1]), _full(lam_p), _full(dn), _full(wd), _full(wg)],
        out_specs=tok(d),
        out_shape=jax.ShapeDtypeStruct(x.shape, F32),
        compiler_params=_cparams("parallel", "parallel"),
        name="mix_odd",
    )(x, mod, do, go, lam_p, dn, wd, wg)


def _router_kernel(x_ref, mod_ref, g_ref, rw_ref, h_out, route_out):
    h = _rms_mod(x_ref[...], g_ref[...], mod_ref[4:5, :], mod_ref[3:4, :])
    h_out[...] = h
    h1, h2, h3 = _split3(h)
    w1, w2, w3 = _split3(rw_ref[...])
    logits = (_dot(h1, w1) + _dot(h1, w2) + _dot(h2, w1) + _dot(h2, w2) + _dot(h1, w3) + _dot(h3, w1))
    lane = lax.broadcasted_iota(jnp.int32, logits.shape, 1).astype(F32)
    neg = -jnp.inf
    l1 = jnp.where(lane < N_EXPERTS, logits, neg)
    m1 = jnp.max(l1, axis=-1, keepdims=True)
    i1 = jnp.min(jnp.where(l1 == m1, lane, float(LANES)), axis=-1, keepdims=True)
    l2 = jnp.where(lane == i1, neg, l1)
    m2 = jnp.max(l2, axis=-1, keepdims=True)
    i2 = jnp.min(jnp.where(l2 == m2, lane, float(LANES)), axis=-1, keepdims=True)
    e = jnp.exp(m2 - m1)
    p1 = 1.0 / (1.0 + e)
    p2 = e / (1.0 + e)
    route = jnp.where(lane == 0.0, i1,
                      jnp.where(lane == 1.0, i2, jnp.where(lane == 2.0, p1, jnp.where(lane == 3.0, p2, 0.0))))
    route_out[...] = route[:, 0:SUBLANES]


def _router(x, mod, g, rw):
    b, t, d = x.shape
    tm = TOKEN_TILE
    tok = lambda w: pl.BlockSpec((None, tm, w), lambda bi, i: (bi, i, 0))
    return pl.pallas_call(
        _router_kernel,
        grid=(b, t // tm),
        in_specs=[tok(d), pl.BlockSpec((None, 6, d), lambda bi, i: (bi, 0, 0)), _full(g), _full(rw)],
        out_specs=[tok(d), tok(SUBLANES)],
        out_shape=[jax.ShapeDtypeStruct(x.shape, F32), jax.ShapeDtypeStruct((b, t, SUBLANES), F32)],
        compiler_params=_cparams("parallel", "parallel"),
        name="moe_router",
    )(x, mod, g, rw)


def _issue_row_gather(idx_ref, n, src_hbm, dst_ref, sem):
    def body(r, carry):
        pltpu.make_async_copy(src_hbm.at[pl.ds(idx_ref[0, r], 1), :], dst_ref.at[pl.ds(r, 1), :], sem).start()
        return carry
    lax.fori_loop(0, n, body, 0)


def _wait_row_gather(src_hbm, dst_ref, sem):
    pltpu.make_async_copy(src_hbm.at[pl.ds(0, dst_ref.shape[0]), :], dst_ref, sem).wait()


def _gather_kernel(idx_ref, src_hbm, out_ref, buf, sem):
    n = buf.shape[0]
    _issue_row_gather(idx_ref, n, src_hbm, buf, sem)
    _wait_row_gather(src_hbm, buf, sem)
    out_ref[...] = buf[...].astype(out_ref.dtype)


def _gather_rows(src, idx, out_dtype):
    nb, blk = idx.shape
    d = src.shape[1]
    return pl.pallas_call(
        _gather_kernel,
        grid=(nb,),
        in_specs=[pl.BlockSpec((None, 1, blk), lambda i: (i, 0, 0), memory_space=pltpu.SMEM),
                  pl.BlockSpec(memory_space=pl.ANY)],
        out_specs=pl.BlockSpec((blk, d), lambda i: (i, 0)),
        out_shape=jax.ShapeDtypeStruct((nb * blk, d), out_dtype),
        scratch_shapes=[pltpu.VMEM((blk, d), src.dtype), pltpu.SemaphoreType.DMA(())],
        compiler_params=_cparams("arbitrary"),
        name="moe_gather",
    )(idx.reshape(nb, 1, blk), src)


def _moe_kernel(be_ref, xs_ref, wg_ref, wu_ref, wd_ref, y_ref, acc_sc):
    j = pl.program_id(1)

    @pl.when(j == 0)
    def _():
        acc_sc[...] = jnp.zeros(acc_sc.shape, F32)

    x = xs_ref[...]
    act = (_silu(_dot(x, wg_ref[...])) * _dot(x, wu_ref[...])).astype(BF16)
    acc_sc[...] += _dot(act, wd_ref[...])

    @pl.when(j == pl.num_programs(1) - 1)
    def _():
        y_ref[...] = acc_sc[...]


def _moe_experts(xs, block_e, w_gu, w_down):
    rows, d = xs.shape
    f = w_down.shape[1]
    tf = _ffn_tile(f)
    nf = f // tf
    nb = rows // MOE_BLOCK
    grid_spec = pltpu.PrefetchScalarGridSpec(
        num_scalar_prefetch=1,
        grid=(nb, nf),
        in_specs=[pl.BlockSpec((MOE_BLOCK, d), lambda i, j, be: (i, 0)),
                  pl.BlockSpec((None, d, tf), lambda i, j, be: (be[i], 0, j)),
                  pl.BlockSpec((None, d, tf), lambda i, j, be: (be[i], 0, nf + j)),
                  pl.BlockSpec((None, tf, d), lambda i, j, be: (be[i], j, 0))],
        out_specs=pl.BlockSpec((MOE_BLOCK, d), lambda i, j, be: (i, 0)),
        scratch_shapes=[pltpu.VMEM((MOE_BLOCK, d), F32)],
    )
    return pl.pallas_call(
        _moe_kernel,
        grid_spec=grid_spec,
        out_shape=jax.ShapeDtypeStruct((rows, d), F32),
        compiler_params=_cparams("parallel", "arbitrary"),
        name="moe_experts",
    )(block_e, xs, w_gu, w_gu, w_down)


def _final_kernel(pos_ref, x_ref, mod_ref, rt_ref, fn_ref, y_hbm, out_ref, ybuf, sem):
    tm = x_ref.shape[0]
    _issue_row_gather(pos_ref, TOP_K * tm, y_hbm, ybuf, sem)
    _wait_row_gather(y_hbm, ybuf, sem)
    rt = rt_ref[...]
    y = rt[:, 2:3] * ybuf[0:tm, :] + rt[:, 3:4] * ybuf[tm:2 * tm, :]
    xo = x_ref[...] + mod_ref[5:6, :] * y
    ms = jnp.mean(xo * xo, axis=-1, keepdims=True)
    out_ref[...] = xo * lax.rsqrt(ms + NORM_EPS) * fn_ref[...]


def _moe_combine_final(x, mod, route, pos, y, fn):
    b, t, d = x.shape
    tm = TOKEN_TILE
    nt = t // tm
    tok = lambda w: pl.BlockSpec((None, tm, w), lambda bi, i: (bi, i, 0))
    return pl.pallas_call(
        _final_kernel,
        grid=(b, nt),
        in_specs=[pl.BlockSpec((None, None, 1, TOP_K * tm), lambda bi, i: (bi, i, 0, 0), memory_space=pltpu.SMEM),
                  tok(d), pl.BlockSpec((None, 6, d), lambda bi, i: (bi, 0, 0)), tok(route.shape[-1]),
                  _full(fn), pl.BlockSpec(memory_space=pl.ANY)],
        out_specs=tok(d),
        out_shape=jax.ShapeDtypeStruct(x.shape, F32),
        scratch_shapes=[pltpu.VMEM((TOP_K * tm, d), F32), pltpu.SemaphoreType.DMA(())],
        compiler_params=_cparams("arbitrary", "arbitrary"),
        name="moe_combine_final",
    )(pos.reshape(b, nt, 1, TOP_K * tm), x, mod, route, fn, y)


def _moe_plan(top_e, n_tok):
    nk = n_tok * TOP_K
    flat_e = top_e.reshape(-1)
    onehot = (flat_e[:, None] == jnp.arange(N_EXPERTS, dtype=jnp.int32)[None, :]).astype(jnp.int32)
    rank = jnp.sum((jnp.cumsum(onehot, axis=0) - onehot) * onehot, axis=1)
    counts = jnp.sum(onehot, axis=0)
    padded = (counts + MOE_BLOCK - 1) // MOE_BLOCK * MOE_BLOCK
    pad_end = jnp.cumsum(padded)
    pad_start = pad_end - padded
    slot = (pad_start[flat_e] + rank).astype(jnp.int32)
    n_blocks = -(-nk // MOE_BLOCK) + N_EXPERTS
    rows = n_blocks * MOE_BLOCK
    flat_tok = jnp.repeat(jnp.arange(n_tok, dtype=jnp.int32), TOP_K)
    buf_tok = jnp.zeros((rows,), jnp.int32).at[slot].set(flat_tok)
    block_e = jnp.minimum(
        jnp.searchsorted(pad_end, jnp.arange(n_blocks, dtype=jnp.int32) * MOE_BLOCK, side="right"),
        N_EXPERTS - 1).astype(jnp.int32)
    return slot.reshape(n_tok, TOP_K), buf_tok.reshape(n_blocks, MOE_BLOCK), block_e


def _prep_even(w_in, w_uq, w_ukv, w_out):
    src, _, _, _ = _mla_layout()
    o = np.cumsum((0, MLA_Q_RANK, MLA_KV_RANK, MLA_ROPE, GDN_QKV, GDN_HEADS * GDN_DV, 2 * GDN_HEADS,
                   2 * GDN_HEADS))
    wcq, wckv = w_in[:, o[0]:o[1]], w_in[:, o[1]:o[2]]
    wkpe = _take_cols(w_in[:, o[2]:o[3]], np.where(src >= MLA_NOPE, src - MLA_NOPE, -1))
    wgqkv, wgz, wab = w_in[:, o[3]:o[4]], w_in[:, o[4]:o[5]], w_in[:, o[5]:o[7]]
    dq = MLA_NOPE + MLA_ROPE
    uq_idx = np.concatenate([np.where(src >= 0, h * dq + src, -1) for h in range(MLA_HEADS)])
    dkv = MLA_NOPE + MLA_V
    uk_idx = np.concatenate([np.where((src >= 0) & (src < MLA_NOPE), h * dkv + src, -1)
                             for h in range(MLA_HEADS)])
    uv_idx = np.concatenate([h * dkv + MLA_NOPE + np.arange(MLA_V) for h in range(MLA_HEADS)])
    wuq, wuk, wuv = _take_cols(w_uq, uq_idx), _take_cols(w_ukv, uk_idx), _take_cols(w_ukv, uv_idx)
    rows = -np.ones(MLA_HEADS * LANES, np.int64)
    for h in range(MLA_HEADS):
        base = h * LANES + (h % 2) * MLA_V
        rows[base:base + MLA_V] = h * MLA_V + np.arange(MLA_V)
    n_mla = MLA_HEADS * MLA_V
    wa = _take_cols(w_out[:n_mla].T, rows).T
    wg = w_out[n_mla:]
    ws = [w.astype(BF16) for w in (wcq, wckv, wkpe, wgqkv, wgz, wab)]
    return ws, [wuq.astype(BF16), wuk.astype(BF16), wuv.T.astype(BF16)], wa.astype(BF16), wg.astype(BF16)


def _prep_odd(w_in, q_norm, k_norm, w_out):
    m, src, _, _, _ = _pair_layout()
    nd = DIFF_HEADS * 2 * DIFF_DH
    o_dq, o_dk, o_dv, o_gq = 0, nd, 2 * nd, 3 * nd
    o_gk = o_gq + GQA_HEADS * GQA_DH
    o_gv = o_gk + GQA_KV_HEADS * GQA_DH
    pair = np.concatenate([h * 2 * DIFF_DH + m * DIFF_DH + src for h in range(DIFF_HEADS)])
    wdq = _take_cols(w_in, o_dq + pair)
    wdk = _take_cols(w_in, o_dk + pair)
    wdv = w_in[:, o_dv:o_gq].T
    grp = GQA_HEADS // GQA_KV_HEADS
    gq_idx = np.concatenate([np.where(m == h // grp, o_gq + h * GQA_DH + src, -1) for h in range(GQA_HEADS)])
    wgq = _take_cols(w_in, gq_idx)
    wgk = _take_cols(w_in, o_gk + m * GQA_DH + src)
    wgv = w_in[:, o_gv:o_gv + GQA_KV_HEADS * GQA_DH].T
    gqg = jnp.tile(q_norm[src], GQA_HEADS).reshape(1, -1)
    gkg = k_norm[src].reshape(1, -1)
    rows = -np.ones(GQA_HEADS * LANES, np.int64)
    for h in range(GQA_HEADS):
        base = h * LANES + (h // grp) * GQA_DH
        rows[base:base + GQA_DH] = h * GQA_DH + np.arange(GQA_DH)
    wd = w_out[:nd]
    wg = _take_cols(w_out[nd:].T, rows).T
    ws = [w.astype(BF16) for w in (wdq, wdk, wdv, wgq, wgk, wgv)]
    return ws, [gqg.astype(F32), gkg.astype(F32)], wd.astype(BF16), wg.astype(BF16)


def _layer_mods(m_layer, batch):
    d = m_layer.shape[1] // 6
    lat = m_layer[:batch].reshape(batch, 6, d)
    ctx = jnp.broadcast_to(m_layer[batch].reshape(1, 6, d), (batch, 6, d))
    return lat, ctx


def kernel(x, c, ctx, c_ctx, mod_w, mod_b, norm_g, ev_w_in, ev_mla_q_norm, ev_mla_kv_norm, ev_mla_w_uq,
           ev_mla_w_ukv, ev_gdn_conv, ev_gdn_a_log, ev_gdn_dt_bias, ev_gdn_out_norm, ev_w_out, ev_ffn_w_gu,
           ev_ffn_w_down, od_w_in, od_diff_lambda, od_diff_norm, od_gqa_q_norm, od_gqa_k_norm, od_w_out,
           od_router_w, od_moe_w_gu, od_moe_w_down, final_norm):
    batch, t_lat, d = x.shape
    t_ctx = ctx.shape[1]
    depth = mod_w.shape[0]
    assert depth == 2 and batch < 16

    cs = jnp.zeros((16, d), F32).at[:batch].set(c).at[batch].set(c_ctx)
    mods = _modulation(cs, mod_w, mod_b)

    j = 0
    mod_lat, mod_ctx = _layer_mods(mods[0], batch)
    g1 = norm_g[0, 0].reshape(1, d)
    g2 = norm_g[0, 1].reshape(1, d)
    ws, ups, wa, wg = _prep_even(ev_w_in[j], ev_mla_w_uq[j], ev_mla_w_ukv[j], ev_w_out[j])
    ws = ws + [ev_mla_q_norm[j].reshape(1, -1), ev_mla_kv_norm[j].reshape(1, -1)] + ups
    _, kind, freq, sign = _mla_layout()
    qscale = (MLA_NOPE + MLA_ROPE) ** -0.5 * LOG2E
    nfq = MLA_ROPE // 4
    tabs_lat = (_rope_tables(t_lat, kind, freq, sign, nfq, qscale, True)
                + _rope_tables(t_lat, kind, freq, sign, nfq, 1.0, True))
    tabs_ctx = (_rope_tables(t_ctx, kind, freq, sign, nfq, qscale, False)
                + _rope_tables(t_ctx, kind, freq, sign, nfq, 1.0, False))
    ql, kl, vl, gqkv_l, gz_l, ab_l = _even_proj(x, mod_lat, g1, tabs_lat, ws)
    qc, kc, vc, gqkv_c, gz_c, ab_c = _even_proj(ctx, mod_ctx, g1, tabs_ctx, ws)

    same = lambda hg: hg
    k_all = jnp.concatenate([kc, kl], axis=1)
    v_all = jnp.concatenate([vc, vl], axis=2)
    mla_kw = dict(groups=MLA_HEADS, g=1, kmap=same, vmap=lambda hg: hg // 2, out_dtype=BF16)
    ao_l = _flash(ql, k_all, v_all, **mla_kw)
    ao_c = _flash(qc, kc, vc, **mla_kw)

    nh2 = 2 * GDN_HEADS
    prm = jnp.zeros((2, 2 * nh2), F32)
    prm = prm.at[0, :nh2].set(ev_gdn_a_log[j].reshape(-1)).at[1, :nh2].set(ev_gdn_dt_bias[j].reshape(-1))
    eseg = _seg_ones(GDN_HEADS * GDN_DK, GDN_DK)
    prep_l = _gdn_prep(gqkv_l, ab_l, ev_gdn_conv[j], prm, eseg)
    prep_c = _gdn_prep(gqkv_c, ab_c, ev_gdn_conv[j], prm, eseg)
    go_c, go_l = _gdn_scan(prep_c, prep_l)

    on = jnp.tile(ev_gdn_out_norm[j], GDN_HEADS).reshape(1, -1)
    x = _mix_even(x, mod_lat, ao_l, go_l, gz_l, on, eseg, wa, wg)
    ctx = _mix_even(ctx, mod_ctx, ao_c, go_c, gz_c, on, eseg, wa, wg)
    w_gu, w_dn = ev_ffn_w_gu[j].astype(BF16), ev_ffn_w_down[j].astype(BF16)
    x = _ffn(x, mod_lat, g2, w_gu, w_dn)
    ctx = _ffn(ctx, mod_ctx, g2, w_gu, w_dn)

    lambda_init = 0.8 - 0.6 * math.exp(-0.3 * 1)
    mod_lat, mod_ctx = _layer_mods(mods[1], batch)
    g1 = norm_g[1, 0].reshape(1, d)
    g2 = norm_g[1, 1].reshape(1, d)
    ws, gains, wd, wgx = _prep_odd(od_w_in[j], od_gqa_q_norm[j], od_gqa_k_norm[j], od_w_out[j])
    _, _, kind, freq, sign = _pair_layout()
    nfq = DIFF_DH // 4
    tabs_lat = _rope_tables(t_lat, kind, freq, sign, nfq, 1.0, True)
    tabs_ctx = _rope_tables(t_ctx, kind, freq, sign, nfq, 1.0, False)
    qscale = DIFF_DH ** -0.5 * LOG2E
    dq, dk_l, dv_l, gq, gk_l, gv_l = _odd_proj(x, mod_lat, g1, tabs_lat, ws, gains, qscale)
    _, dk_c, dv_c, _, gk_c, gv_c = _odd_proj(ctx, mod_ctx, g1, tabs_ctx, ws, gains, qscale)
    cat = lambda a, b_: jnp.concatenate([a, b_], axis=1)
    cat_vt = lambda a, b_: jnp.concatenate([a, b_], axis=2)
    do = _flash(dq, cat(dk_c, dk_l), cat_vt(dv_c, dv_l), groups=DIFF_HEADS, g=2, kmap=same, vmap=same,
                out_dtype=F32)
    zero = lambda hg: 0
    go = _flash(gq, cat(gk_c, gk_l), cat_vt(gv_c, gv_l), groups=1, g=GQA_HEADS, kmap=zero, vmap=zero,
                out_dtype=BF16)
    x = _mix_odd(x, mod_lat, do, go, od_diff_lambda[j], od_diff_norm[j].reshape(1, -1), wd, wgx, lambda_init)

    rw = jnp.zeros((d, LANES), F32).at[:, :N_EXPERTS].set(od_router_w[j])
    h2, route = _router(x, mod_lat, g2, rw)
    n_tok = batch * t_lat
    top_e = route[..., 0:TOP_K].astype(jnp.int32).reshape(n_tok, TOP_K)
    slot, buf_tok, block_e = _moe_plan(top_e, n_tok)
    xs = _gather_rows(h2.reshape(n_tok, d), buf_tok, BF16)
    y = _moe_experts(xs, block_e, od_moe_w_gu[j].astype(BF16), od_moe_w_down[j].astype(BF16))
    tm = TOKEN_TILE
    pos = slot.reshape(batch, t_lat // tm, tm, TOP_K).transpose(0, 1, 3, 2).reshape(batch, t_lat // tm, TOP_K * tm)
    return _moe_combine_final(x, mod_lat, route, pos, y, final_norm.reshape(1, d))
```

```python
import functools
import math

import numpy as np
import jax
import jax.numpy as jnp
from jax import lax
from jax.experimental import pallas as pl
from jax.experimental.pallas import tpu as pltpu

F32 = jnp.float32
BF16 = jnp.bfloat16

GRID_W = 64
ROPE_BASE = 10000.0
NORM_EPS = 1e-6
MLA_HEADS, MLA_NOPE, MLA_ROPE, MLA_V = 8, 64, 32, 64
MLA_Q_RANK, MLA_KV_RANK = 384, 256
GDN_HEADS, GDN_DK, GDN_DV, GDN_CONV, GDN_CHUNK = 8, 64, 64, 5, 64
GDN_QKV = GDN_HEADS * (2 * GDN_DK + GDN_DV)
DIFF_HEADS, DIFF_DH = 4, 64
GQA_HEADS, GQA_KV_HEADS, GQA_DH = 8, 2, 64
N_EXPERTS, TOP_K, MOE_BLOCK = 8, 2, 512

LANES = 128
SUBLANES = 8
VMEM_LIMIT = 48 * 1024 * 1024
LOG2E = 1.4426950408889634

TOKEN_TILE = 256
ATTN_ROWS = 1024
ATTN_KEYS = 256
ATTN_UNROLL = 8


def _cparams(*sem):
    return pltpu.CompilerParams(dimension_semantics=sem, vmem_limit_bytes=VMEM_LIMIT)


def _dot(a, b):
    return jnp.dot(a, b, preferred_element_type=F32)


def _dot_t(a, b):
    return lax.dot_general(a, b, (((1,), (1,)), ((), ())), preferred_element_type=F32)


def _split3(x):
    h1 = x.astype(BF16)
    r1 = x - h1.astype(F32)
    h2 = r1.astype(BF16)
    h3 = (r1 - h2.astype(F32)).astype(BF16)
    return h1, h2, h3


def _silu(x):
    return x * jax.nn.sigmoid(x)


def _rms_mod(x, g, sc, sh):
    ms = jnp.mean(x * x, axis=-1, keepdims=True)
    return (x * lax.rsqrt(ms + NORM_EPS) * g) * (1.0 + sc) + sh


def _seg_sumsq(x, eseg):
    sq = x * x
    hi = sq.astype(BF16)
    lo = (sq - hi.astype(F32)).astype(BF16)
    return _dot(hi, eseg) + _dot(lo, eseg)


def _rope(x, c, s):
    return x * c + pltpu.roll(x, LANES // 2, 1) * s


def _mla_layout():
    src = -np.ones(LANES, np.int64)
    kind = np.zeros(LANES, np.int64)
    freq = np.zeros(LANES, np.int64)
    sign = np.zeros(LANES, np.float32)
    nf = MLA_ROPE // 4
    for f in range(nf):
        src[f] = MLA_NOPE + f; kind[f] = 1; freq[f] = f; sign[f] = -1.0
        src[nf + f] = MLA_NOPE + 2 * nf + f; kind[nf + f] = 2; freq[nf + f] = f; sign[nf + f] = -1.0
        src[64 + f] = MLA_NOPE + nf + f; kind[64 + f] = 1; freq[64 + f] = f; sign[64 + f] = 1.0
        src[64 + nf + f] = MLA_NOPE + 3 * nf + f; kind[64 + nf + f] = 2; freq[64 + nf + f] = f
        sign[64 + nf + f] = 1.0
    src[16:64] = np.arange(0, 48)
    src[80:96] = np.arange(48, 64)
    return src, kind, freq, sign


def _pair_layout():
    lane = np.arange(LANES)
    region = lane // 32
    m = region % 2
    is_b = region // 2
    within = lane % 32
    is_col = within // 16
    f = within % 16
    src = is_col * 32 + is_b * 16 + f
    kind = 1 + is_col
    sign = np.where(is_b == 0, -1.0, 1.0).astype(np.float32)
    return m, src, kind, f, sign


def _rope_tables(t_len, kind, freq, sign, n_freq, scale, with_pos):
    kind_j = jnp.asarray(kind)
    if not with_pos:
        c = jnp.full((t_len, LANES), scale, F32)
        return c, jnp.zeros((t_len, LANES), F32)
    n_rows = t_len // GRID_W
    inv = 1.0 / (ROPE_BASE ** (jnp.arange(n_freq, dtype=F32) / n_freq))
    rows = jnp.repeat(jnp.arange(n_rows, dtype=F32), GRID_W)
    cols = jnp.tile(jnp.arange(GRID_W, dtype=F32), n_rows)
    ang_r = (rows[:, None] * inv)[:, freq]
    ang_c = (cols[:, None] * inv)[:, freq]
    ang = jnp.where(kind_j[None, :] == 1, ang_r, ang_c)
    has = (kind_j > 0)[None, :]
    c = jnp.where(has, jnp.cos(ang), 1.0) * scale
    s = jnp.where(has, jnp.sin(ang) * jnp.asarray(sign)[None, :], 0.0) * scale
    return c.astype(F32), s.astype(F32)


def _take_cols(w, idx):
    idx = np.asarray(idx)
    wz = jnp.concatenate([w, jnp.zeros((w.shape[0], 1), w.dtype)], axis=1)
    return wz[:, np.where(idx < 0, w.shape[1], idx)]


def _seg_ones(width, seg):
    r = np.arange(width)
    return jnp.asarray((r[:, None] // seg) == (r[None, :] // seg), dtype=BF16)


def _mod_kernel(c_ref, w_ref, b_ref, o_ref):
    c = c_ref[...]
    s1, s2, s3 = _split3(_silu(c))
    w1, w2, _ = _split3(w_ref[...])
    acc = _dot(s1, w1) + _dot(s1, w2) + _dot(s2, w1) + _dot(s2, w2) + _dot(s3, w1)
    o_ref[...] = acc + b_ref[...]


def _modulation(cs, mod_w, mod_b):
    depth, d, n = mod_w.shape
    tn = 512
    return pl.pallas_call(
        _mod_kernel,
        grid=(depth, n // tn),
        in_specs=[
            pl.BlockSpec(cs.shape, lambda l, j: (0, 0)),
            pl.BlockSpec((None, d, tn), lambda l, j: (l, 0, j)),
            pl.BlockSpec((None, 1, tn), lambda l, j: (l, 0, j)),
        ],
        out_specs=pl.BlockSpec((None, cs.shape[0], tn), lambda l, j: (l, 0, j)),
        out_shape=jax.ShapeDtypeStruct((depth, cs.shape[0], n), F32),
        compiler_params=_cparams("parallel", "parallel"),
        name="modulation",
    )(cs, mod_w, mod_b.reshape(depth, 1, n))


def _even_proj_kernel(x_ref, mod_ref, g_ref, cq_t, sq_t, ck_t, sk_t,
                      wcq, wckv, wkpe, wgqkv, wgz, wab, qn, kvn, wuq, wuk, wuv,
                      q_out, k_out, v_out, gqkv_out, gz_out, ab_out):
    h = _rms_mod(x_ref[...], g_ref[...], mod_ref[1:2, :], mod_ref[0:1, :]).astype(BF16)
    cq = _dot(h, wcq[...])
    ckv = _dot(h, wckv[...])
    kpe = _dot(h, wkpe[...])
    nq = (cq * lax.rsqrt(jnp.mean(cq * cq, axis=-1, keepdims=True) + NORM_EPS) * qn[...]).astype(BF16)
    nkv = (ckv * lax.rsqrt(jnp.mean(ckv * ckv, axis=-1, keepdims=True) + NORM_EPS) * kvn[...]).astype(BF16)
    q = _dot(nq, wuq[...])
    kn = _dot(nkv, wuk[...])
    kper = _rope(kpe, ck_t[...], sk_t[...])
    cqv, sqv = cq_t[...], sq_t[...]
    for hd in range(MLA_HEADS):
        sl = slice(hd * LANES, (hd + 1) * LANES)
        q_out[:, sl] = _rope(q[:, sl], cqv, sqv).astype(BF16)
        k_out[:, sl] = (kn[:, sl] + kper).astype(BF16)
    _store_vt(v_out, _dot_t(wuv[...], nkv))
    gqkv_out[...] = _dot(h, wgqkv[...])
    gz_out[...] = _dot(h, wgz[...])
    ab_out[...] = _dot(h, wab[...])


def _full(a):
    nd = a.ndim
    return pl.BlockSpec(a.shape, lambda *_: (0,) * nd)


def _store_vt(vt_out, vt):
    for n in range(vt_out.shape[0]):
        vt_out[n] = vt[n * LANES:(n + 1) * LANES, :].astype(vt_out.dtype)


def _vt_spec(groups, tm):
    return pl.BlockSpec((None, groups, None, LANES, tm), lambda bi, i: (bi, 0, i, 0, 0))


def _vt_shape(b, groups, t, tm):
    assert tm == ATTN_KEYS
    return jax.ShapeDtypeStruct((b, groups, t // tm, LANES, tm), BF16)


def _even_proj(x, mod, g, tabs, ws):
    b, t, d = x.shape
    tm = TOKEN_TILE
    tok = lambda w: pl.BlockSpec((None, tm, w), lambda bi, i: (bi, i, 0))
    tab = pl.BlockSpec((tm, LANES), lambda bi, i: (i, 0))
    nvg = MLA_HEADS * MLA_V // LANES
    widths = (MLA_HEADS * LANES, MLA_HEADS * LANES, None, GDN_QKV, GDN_HEADS * GDN_DV, 4 * GDN_HEADS)
    dts = (BF16, BF16, None, F32, F32, F32)
    return pl.pallas_call(
        _even_proj_kernel,
        grid=(b, t // tm),
        in_specs=[tok(d), pl.BlockSpec((None, 6, d), lambda bi, i: (bi, 0, 0)), _full(g),
                  tab, tab, tab, tab] + [_full(w) for w in ws],
        out_specs=[_vt_spec(nvg, tm) if w is None else tok(w) for w in widths],
        out_shape=[_vt_shape(b, nvg, t, tm) if w is None else jax.ShapeDtypeStruct((b, t, w), dt)
                   for w, dt in zip(widths, dts)],
        compiler_params=_cparams("parallel", "parallel"),
        name="even_proj",
    )(x, mod, g, *tabs, *ws)


def _flash_kernel(q_ref, k_ref, vt_ref, o_ref, acc_sc, s_a, s_b, *, g, tq, tk, nsub):
    q = jnp.concatenate([q_ref[:, i * LANES:(i + 1) * LANES] for i in range(g)], axis=0)
    rows = g * tq
    acc_sc[...] = jnp.zeros(acc_sc.shape, F32)
    ones = jnp.ones((2 * SUBLANES, tk), BF16)

    def scores(jj):
        return _dot_t(k_ref[pl.ds(pl.multiple_of(jj * tk, tk), tk), :], q)

    def update(st, jj, m):
        m_new = jnp.maximum(m, jnp.max(st, axis=0, keepdims=True))
        p = jnp.exp2(st - m_new).astype(BF16)
        lhs = jnp.concatenate([vt_ref[jj], ones], axis=0)
        acc_sc[...] = jnp.exp2(m - m_new) * acc_sc[...] + _dot(lhs, p)
        return m_new

    bufs = (s_a, s_b)
    s_a[...] = scores(0)

    def run(j0, count, m, prefetch_last):
        for u in range(count):
            if u + 1 < count or prefetch_last:
                bufs[(u + 1) % 2][...] = scores(jnp.minimum(j0 + u + 1, nsub - 1))
            m = update(bufs[u % 2][...], j0 + u, m)
        return m

    trips, tail = divmod(nsub, ATTN_UNROLL)
    m = lax.fori_loop(0, trips, lambda i, m: run(i * ATTN_UNROLL, ATTN_UNROLL, m, True),
                      jnp.full((1, rows), -jnp.inf, F32))
    run(trips * ATTN_UNROLL, tail, m, False)
    on = acc_sc[0:LANES, :] * (1.0 / acc_sc[LANES:LANES + 1, :])
    o = on.T
    for i in range(g):
        o_ref[:, i * LANES:(i + 1) * LANES] = o[i * tq:(i + 1) * tq, :].astype(o_ref.dtype)


def _flash(q, k, vt, *, groups, g, kmap, vmap, out_dtype):
    b, tq_total, _ = q.shape
    tk_total = k.shape[1]
    tq = min(ATTN_ROWS // g, tq_total)
    tk = ATTN_KEYS
    nsub = tk_total // tk
    assert vt.shape[2:] == (nsub, LANES, tk)
    rows = g * tq
    kern = functools.partial(_flash_kernel, g=g, tq=tq, tk=tk, nsub=nsub)
    return pl.pallas_call(
        kern,
        grid=(b, groups, tq_total // tq),
        in_specs=[
            pl.BlockSpec((None, tq, g * LANES), lambda bi, hg, i: (bi, i, hg)),
            pl.BlockSpec((None, tk_total, LANES), lambda bi, hg, i: (bi, 0, kmap(hg))),
            pl.BlockSpec((None, None, nsub, LANES, tk), lambda bi, hg, i: (bi, vmap(hg), 0, 0, 0)),
        ],
        out_specs=pl.BlockSpec((None, tq, g * LANES), lambda bi, hg, i: (bi, i, hg)),
        out_shape=jax.ShapeDtypeStruct(q.shape, out_dtype),
        scratch_shapes=[pltpu.VMEM((LANES + 2 * SUBLANES, rows), F32), pltpu.VMEM((tk, rows), F32),
                        pltpu.VMEM((tk, rows), F32)],
        compiler_params=_cparams("parallel", "parallel", "parallel"),
        name="flash_attention",
    )(q, k, vt)


def _gdn_prep_kernel(x_ref, prev_ref, next_ref, ab_ref, cw_ref, prm_ref, eseg_ref,
                     lf_ref, lb_ref, la_ref, q_out, k_out, v_out, s1_out, s2_out, xe_sc):
    i = pl.program_id(1)
    tm = x_ref.shape[0]
    halo = SUBLANES
    xe_sc[0:halo, :] = prev_ref[...] * (i > 0).astype(F32)
    xe_sc[halo:halo + tm, :] = x_ref[...]
    xe_sc[halo + tm:2 * halo + tm, :] = next_ref[...] * (i < pl.num_programs(1) - 1).astype(F32)
    acc = jnp.zeros(x_ref.shape, F32)
    for j in range(GDN_CONV):
        acc = acc + cw_ref[j:j + 1, :] * xe_sc[pl.ds(halo - GDN_CONV // 2 + j, tm), :]
    y = _silu(acc)
    hw = GDN_HEADS * GDN_DK
    eseg = eseg_ref[...]
    q = y[:, 0:hw]
    k = y[:, hw:2 * hw]
    q_out[...] = q * lax.rsqrt(_seg_sumsq(q, eseg) + NORM_EPS) * (GDN_DK ** -0.5)
    k_out[...] = k * lax.rsqrt(_seg_sumsq(k, eseg) + NORM_EPS)
    v_out[...] = y[:, 2 * hw:]

    ab = ab_ref[...]
    nh2 = 2 * GDN_HEADS
    lane = lax.broadcasted_iota(jnp.int32, ab.shape, 1)
    z = ab + prm_ref[1:2, :]
    softplus = jnp.maximum(z, 0.0) + jnp.log1p(jnp.exp(-jnp.abs(z)))
    gate = jnp.where(lane < nh2, -jnp.exp(prm_ref[0:1, :]) * softplus, 0.0)
    g1, g2, g3 = _split3(gate)
    lf, lb, la = lf_ref[...], lb_ref[...], la_ref[...]
    cum_f = _dot(lf, g1) + _dot(lf, g2) + _dot(lf, g3)
    cum_b = _dot(lb, g1) + _dot(lb, g2) + _dot(lb, g3)
    tot = _dot(la, g1) + _dot(la, g2) + _dot(la, g3)
    cum = jnp.where(lane < GDN_HEADS, cum_f, cum_b)
    s1_out[...] = jnp.where(lane < nh2, cum, jax.nn.sigmoid(ab))
    s2_out[...] = tot


def _gdn_prep(gqkv, ab, conv_w, prm, eseg):
    b, t, w = gqkv.shape
    tm = TOKEN_TILE
    nt8 = t // SUBLANES
    per = tm // SUBLANES
    r = np.arange(tm)
    same = (r[:, None] // GDN_CHUNK) == (r[None, :] // GDN_CHUNK)
    lf = jnp.asarray(same & (r[None, :] <= r[:, None]), dtype=BF16)
    lb = jnp.asarray(same & (r[None, :] >= r[:, None]), dtype=BF16)
    la = jnp.asarray(same, dtype=BF16)
    halo = gqkv.reshape(b, nt8, SUBLANES, w)
    tok = lambda wd: pl.BlockSpec((None, tm, wd), lambda bi, i: (bi, i, 0))
    hw = GDN_HEADS * GDN_DK
    nab = ab.shape[-1]
    return pl.pallas_call(
        _gdn_prep_kernel,
        grid=(b, t // tm),
        in_specs=[
            tok(w),
            pl.BlockSpec((None, None, SUBLANES, w), lambda bi, i: (bi, jnp.maximum(i * per - 1, 0), 0, 0)),
            pl.BlockSpec((None, None, SUBLANES, w), lambda bi, i: (bi, jnp.minimum((i + 1) * per, nt8 - 1), 0, 0)),
            tok(nab), _full(conv_w), _full(prm), _full(eseg), _full(lf), _full(lb), _full(la),
        ],
        out_specs=[tok(hw), tok(hw), tok(GDN_HEADS * GDN_DV), tok(nab), tok(nab)],
        out_shape=[jax.ShapeDtypeStruct((b, t, hw), F32), jax.ShapeDtypeStruct((b, t, hw), F32),
                   jax.ShapeDtypeStruct((b, t, GDN_HEADS * GDN_DV), F32),
                   jax.ShapeDtypeStruct((b, t, nab), F32), jax.ShapeDtypeStruct((b, t, nab), F32)],
        scratch_shapes=[pltpu.VMEM((tm + 2 * SUBLANES, w), F32)],
        compiler_params=_cparams("parallel", "parallel"),
        name="gdn_prep",
    )(gqkv, halo, halo, ab, conv_w, prm, eseg, lf, lb, la)


def _bmm(a, b):
    return jnp.einsum("nij,njk->nik", a, b, preferred_element_type=F32)


def _bmm_t(a, b):
    return jnp.einsum("nik,njk->nij", a, b, preferred_element_type=F32)


def _gdn_intra_kernel(q_ref, k_ref, v_ref, gr_ref, a_out, b_out, qe_out, o0_out, *, hb, cb):
    c_len = GDN_CHUNK
    n = hb * cb
    sgn = 1 - 2 * pl.program_id(0)
    row = lax.broadcasted_iota(jnp.int32, (c_len, c_len), 0)
    col = lax.broadcasted_iota(jnp.int32, (c_len, c_len), 1)
    ahead = (row - col) * sgn
    eye_f = (row == col).astype(F32)
    eye_b = jnp.broadcast_to(eye_f.astype(BF16), (n, c_len, c_len))
    q = q_ref[...].reshape(n, c_len, GDN_DK)
    k = k_ref[...].reshape(n, c_len, GDN_DK)
    v = v_ref[...].reshape(n, c_len, GDN_DV)
    g_cols = gr_ref[:, :, 0:1, :].reshape(n, 1, c_len)
    beta_row = gr_ref[:, :, 1:2, :].reshape(n, 1, c_len)
    gl = gr_ref[:, :, 2:3, :].reshape(n, 1, c_len)

    ones_b = jnp.broadcast_to(jnp.ones((c_len, c_len), BF16), (n, c_len, c_len))

    def on_sublanes(r, pieces):
        parts = _split3(eye_f * r)[:pieces]
        return functools.reduce(lambda a, b: a + b, [_bmm_t(p, ones_b) for p in parts])

    gc = on_sublanes(g_cols, 3)
    beta = on_sublanes(beta_row, 2)
    eg, ek, cd = jnp.exp(gc), jnp.exp(gl - gc), jnp.exp(gl)
    decay = jnp.where(ahead >= 0, jnp.exp(jnp.minimum(gc - g_cols, 0.0)), 0.0)
    kb = k * beta
    kbf = k.astype(BF16)
    low = jnp.where(ahead > 0, _bmm_t(kb.astype(BF16), kbf) * decay, 0.0)
    qk = _bmm_t(q.astype(BF16), kbf) * decay
    mpow = jnp.where((row // 8) == (col // 8), -low, 0.0)
    tinv = eye_f + mpow
    for _ in range(2):
        mb = mpow.astype(BF16)
        mpow = _bmm(mb, mb)
        tinv = tinv + _bmm(tinv.astype(BF16), mpow.astype(BF16))
    for s in (8, 16, 32):
        off = ((row // (2 * s)) == (col // (2 * s))) & ((row // s) != (col // s))
        tb = tinv.astype(BF16)
        tinv = tinv - _bmm(_bmm(tb, jnp.where(off, low, 0.0).astype(BF16)).astype(BF16), tb)
    tb = tinv.astype(BF16)
    ub = _bmm(tb, (v * beta).astype(BF16)).astype(BF16)
    wb = _bmm(tb, (kb * eg).astype(BF16)).astype(BF16)
    kdt = _bmm_t(eye_b, (k * ek).astype(BF16)).astype(BF16)
    a_out[...] = (cd * eye_f - _bmm(kdt, wb)).reshape(a_out.shape)
    b_out[...] = _bmm(kdt, ub).reshape(b_out.shape)
    qkb = qk.astype(BF16)
    qe_out[...] = (q * eg - _bmm(qkb, wb)).reshape(qe_out.shape)
    o0_out[...] = _bmm(qkb, ub).reshape(o0_out.shape)


def _gdn_intra(q, k, v, gr):
    b, h, tt, dk = q.shape
    assert dk == GDN_CHUNK and v.shape[-1] == GDN_CHUNK
    nc = tt // GDN_CHUNK
    hb, cb = h, 2
    rows = cb * GDN_CHUNK
    qkv = pl.BlockSpec((None, hb, rows, dk), lambda d, bi, c: (bi, 0, c, 0))
    tokb = pl.BlockSpec((None, None, hb, rows, dk), lambda d, bi, c: (d, bi, 0, c, 0))
    matb = pl.BlockSpec((None, None, hb, cb, dk, dk), lambda d, bi, c: (d, bi, 0, c, 0, 0))
    return pl.pallas_call(
        functools.partial(_gdn_intra_kernel, hb=hb, cb=cb),
        grid=(2, b, nc // cb),
        in_specs=[qkv, qkv, qkv,
                  pl.BlockSpec((None, None, hb, cb, SUBLANES, GDN_CHUNK), lambda d, bi, c: (d, bi, 0, c, 0, 0))],
        out_specs=[matb, matb, tokb, tokb],
        out_shape=[jax.ShapeDtypeStruct((2, b, h, nc, dk, dk), F32)] * 2
                  + [jax.ShapeDtypeStruct((2, b, h, tt, dk), F32)] * 2,
        compiler_params=_cparams("parallel", "parallel", "parallel"),
        name="gdn_intra",
    )(q, k, v, gr)


def _gdn_inter_kernel(af, bf, qf, of, ab, bb, qb, ob, o_f, o_b, s_sc):
    @pl.when(pl.program_id(1) == 0)
    def _():
        s_sc[...] = jnp.zeros(s_sc.shape, F32)

    for d, (a_ref, b_ref, qe_ref, o0_ref, o_ref) in enumerate(((af, bf, qf, of, o_f), (ab, bb, qb, ob, o_b))):
        sb = s_sc[d].astype(BF16)
        o_ref[...] = _bmm(qe_ref[...].astype(BF16), sb) + o0_ref[...]
        s_sc[d] = _bmm(a_ref[...].astype(BF16), sb) + b_ref[...]


def _gdn_inter(a, bm, qe, o0, nc_ctx):
    _, b, h, nc, dk, _ = a.shape
    rev = lambda c: jnp.where(c < nc_ctx, nc_ctx - 1 - c, nc - 1 - (c - nc_ctx))
    mat_f = pl.BlockSpec((None, None, h, None, dk, dk), lambda bi, c: (0, bi, 0, c, 0, 0))
    mat_b = pl.BlockSpec((None, None, h, None, dk, dk), lambda bi, c: (1, bi, 0, rev(c), 0, 0))
    tok_f = pl.BlockSpec((None, None, h, GDN_CHUNK, dk), lambda bi, c: (0, bi, 0, c, 0))
    tok_b = pl.BlockSpec((None, None, h, GDN_CHUNK, dk), lambda bi, c: (1, bi, 0, rev(c), 0))
    out_f = pl.BlockSpec((None, h, GDN_CHUNK, dk), lambda bi, c: (bi, 0, c, 0))
    out_b = pl.BlockSpec((None, h, GDN_CHUNK, dk), lambda bi, c: (bi, 0, rev(c), 0))
    osd = jax.ShapeDtypeStruct(qe.shape[1:], F32)
    return pl.pallas_call(
        _gdn_inter_kernel,
        grid=(b, nc),
        in_specs=[mat_f, mat_f, tok_f, tok_f, mat_b, mat_b, tok_b, tok_b],
        out_specs=[out_f, out_b],
        out_shape=[osd, osd],
        scratch_shapes=[pltpu.VMEM((2, h, dk, dk), F32)],
        compiler_params=_cparams("parallel", "arbitrary"),
        name="gdn_inter",
    )(a, bm, qe, o0, a, bm, qe, o0)


def _gdn_scan(prep_ctx, prep_lat):
    b, tc, _ = prep_ctx[0].shape
    nh = GDN_HEADS
    cat = [jnp.concatenate([c, l], axis=1) for c, l in zip(prep_ctx, prep_lat)]
    tt = cat[0].shape[1]
    q, k, v = (x.reshape(b, tt, nh, -1).transpose(0, 2, 1, 3) for x in cat[:3])
    s1t, s2t = cat[3].transpose(0, 2, 1), cat[4].transpose(0, 2, 1)
    nc = tt // GDN_CHUNK
    zero = jnp.zeros((b, nh, nc, GDN_CHUNK), F32)
    gr = jnp.stack([
        jnp.stack([x.reshape(b, nh, nc, GDN_CHUNK) for x in
                   (s1t[:, d * nh:(d + 1) * nh], s1t[:, (2 + d) * nh:(3 + d) * nh], s2t[:, d * nh:(d + 1) * nh])]
                  + [zero] * (SUBLANES - 3), axis=3)
        for d in range(2)])
    a, bm, qe, o0 = _gdn_intra(q, k, v, gr)
    o_f, o_b = _gdn_inter(a, bm, qe, o0, tc // GDN_CHUNK)
    o = (o_f + o_b).transpose(0, 2, 1, 3).reshape(b, tt, nh * GDN_DV)
    return o[:, :tc], o[:, tc:]


def _mix_even_kernel(x_ref, mod_ref, ao_ref, go_ref, gz_ref, on_ref, eseg_ref, wa_ref, wg_ref, out_ref):
    o = go_ref[...]
    ms = _seg_sumsq(o, eseg_ref[...]) * (1.0 / GDN_DV)
    y = o * lax.rsqrt(ms + NORM_EPS) * on_ref[...] * _silu(gz_ref[...])
    mix = _dot(ao_ref[...], wa_ref[...]) + _dot(y.astype(BF16), wg_ref[...])
    out_ref[...] = x_ref[...] + mod_ref[2:3, :] * mix


def _mix_even(x, mod, ao, go, gz, on, eseg, wa, wg):
    b, t, d = x.shape
    tm = TOKEN_TILE
    tok = lambda w: pl.BlockSpec((None, tm, w), lambda bi, i: (bi, i, 0))
    return pl.pallas_call(
        _mix_even_kernel,
        grid=(b, t // tm),
        in_specs=[tok(d), pl.BlockSpec((None, 6, d), lambda bi, i: (bi, 0, 0)), tok(ao.shape[-1]),
                  tok(go.shape[-1]), tok(gz.shape[-1]), _full(on), _full(eseg), _full(wa), _full(wg)],
        out_specs=tok(d),
        out_shape=jax.ShapeDtypeStruct(x.shape, F32),
        compiler_params=_cparams("parallel", "parallel"),
        name="mix_even",
    )(x, mod, ao, go, gz, on, eseg, wa, wg)


def _ffn_kernel(x_ref, mod_ref, g_ref, wg_ref, wu_ref, wd_ref, out_ref, h_sc, acc_sc):
    j = pl.program_id(2)

    @pl.when(j == 0)
    def _():
        h_sc[...] = _rms_mod(x_ref[...], g_ref[...], mod_ref[4:5, :], mod_ref[3:4, :]).astype(BF16)
        acc_sc[...] = jnp.zeros(acc_sc.shape, F32)

    h = h_sc[...]
    act = (_silu(_dot(h, wg_ref[...])) * _dot(h, wu_ref[...])).astype(BF16)
    acc_sc[...] += _dot(act, wd_ref[...])

    @pl.when(j == pl.num_programs(2) - 1)
    def _():
        out_ref[...] = x_ref[...] + mod_ref[5:6, :] * acc_sc[...]


def _ffn_tile(f):
    for cand in (1408, 1792, 1024, 896, 768, 512, 256, 128):
        if f % cand == 0:
            return cand
    raise ValueError(f"ffn width {f} is not a multiple of 128")


def _ffn(x, mod, g, w_gu, w_down):
    b, t, d = x.shape
    f = w_down.shape[0]
    tm = 512 if t % 512 == 0 else TOKEN_TILE
    tf = _ffn_tile(f)
    nf = f // tf
    tok = pl.BlockSpec((None, tm, d), lambda bi, i, j: (bi, i, 0))
    return pl.pallas_call(
        _ffn_kernel,
        grid=(b, t // tm, nf),
        in_specs=[tok, pl.BlockSpec((None, 6, d), lambda bi, i, j: (bi, 0, 0)),
                  pl.BlockSpec(g.shape, lambda bi, i, j: (0, 0)),
                  pl.BlockSpec((d, tf), lambda bi, i, j: (0, j)),
                  pl.BlockSpec((d, tf), lambda bi, i, j: (0, nf + j)),
                  pl.BlockSpec((tf, d), lambda bi, i, j: (j, 0))],
        out_specs=tok,
        out_shape=jax.ShapeDtypeStruct(x.shape, F32),
        scratch_shapes=[pltpu.VMEM((tm, d), BF16), pltpu.VMEM((tm, d), F32)],
        compiler_params=_cparams("parallel", "parallel", "arbitrary"),
        name="ffn",
    )(x, mod, g, w_gu, w_gu, w_down)


def _odd_proj_kernel(x_ref, mod_ref, g_ref, c_t, s_t, wdq, wdk, wdv, wgq, wgk, wgv, gqg, gkg,
                     dq_out, dk_out, dv_out, gq_out, gk_out, gv_out, *, qscale):
    h = _rms_mod(x_ref[...], g_ref[...], mod_ref[1:2, :], mod_ref[0:1, :]).astype(BF16)
    c, s = c_t[...], s_t[...]
    lane = lax.broadcasted_iota(jnp.int32, c.shape, 1)
    is_m1 = ((lane // 32) % 2) == 1
    dq = _dot(h, wdq[...])
    dk = _dot(h, wdk[...])
    for hd in range(DIFF_HEADS):
        sl = slice(hd * LANES, (hd + 1) * LANES)
        r = _rope(dq[:, sl], c, s) * qscale
        dq_out[:, 2 * hd * LANES:(2 * hd + 1) * LANES] = jnp.where(is_m1, 0.0, r).astype(BF16)
        dq_out[:, (2 * hd + 1) * LANES:(2 * hd + 2) * LANES] = jnp.where(is_m1, r, 0.0).astype(BF16)
        dk_out[:, sl] = _rope(dk[:, sl], c, s).astype(BF16)
    _store_vt(dv_out, _dot_t(wdv[...], h))
    gq = _dot(h, wgq[...])
    gqgv = gqg[...]
    inv_dh = 1.0 / GQA_DH
    for hd in range(GQA_HEADS):
        sl = slice(hd * LANES, (hd + 1) * LANES)
        xh = gq[:, sl]
        ms = jnp.sum(xh * xh, axis=-1, keepdims=True) * inv_dh
        xn = xh * lax.rsqrt(ms + NORM_EPS) * gqgv[:, sl]
        gq_out[:, sl] = (_rope(xn, c, s) * qscale).astype(BF16)
    gk = _dot(h, wgk[...])
    sq = gk * gk
    ms0 = jnp.sum(jnp.where(is_m1, 0.0, sq), axis=-1, keepdims=True) * inv_dh
    ms1 = jnp.sum(jnp.where(is_m1, sq, 0.0), axis=-1, keepdims=True) * inv_dh
    rs = jnp.where(is_m1, lax.rsqrt(ms1 + NORM_EPS), lax.rsqrt(ms0 + NORM_EPS))
    gk_out[...] = _rope(gk * rs * gkg[...], c, s).astype(BF16)
    _store_vt(gv_out, _dot_t(wgv[...], h))


def _odd_proj(x, mod, g, tabs, ws, gains, qscale):
    b, t, d = x.shape
    tm = TOKEN_TILE
    tok = lambda w: pl.BlockSpec((None, tm, w), lambda bi, i: (bi, i, 0))
    tab = pl.BlockSpec((tm, LANES), lambda bi, i: (i, 0))
    outs = ((2 * DIFF_HEADS * LANES, 0), (DIFF_HEADS * LANES, 0), (0, DIFF_HEADS),
            (GQA_HEADS * LANES, 0), (LANES, 0), (0, 1))
    return pl.pallas_call(
        functools.partial(_odd_proj_kernel, qscale=qscale),
        grid=(b, t // tm),
        in_specs=[tok(d), pl.BlockSpec((None, 6, d), lambda bi, i: (bi, 0, 0)), _full(g), tab, tab]
                 + [_full(w) for w in ws] + [_full(w) for w in gains],
        out_specs=[_vt_spec(n, tm) if n else tok(w) for w, n in outs],
        out_shape=[_vt_shape(b, n, t, tm) if n else jax.ShapeDtypeStruct((b, t, w), BF16) for w, n in outs],
        compiler_params=_cparams("parallel", "parallel"),
        name="odd_proj",
    )(x, mod, g, *tabs, *ws, *gains)


def _mix_odd_kernel(x_ref, mod_ref, do_ref, go_ref, lam_ref, dn_ref, wd_ref, wg_ref, out_ref, *, lambda_init):
    lp = lam_ref[...]
    lam = (jnp.exp(jnp.sum(lp[0:1, :] * lp[1:2, :], axis=-1, keepdims=True))
           - jnp.exp(jnp.sum(lp[2:3, :] * lp[3:4, :], axis=-1, keepdims=True)) + lambda_init)
    dn = dn_ref[...]
    parts = []
    for hd in range(DIFF_HEADS):
        d0 = do_ref[:, 2 * hd * LANES:(2 * hd + 1) * LANES]
        d1 = do_ref[:, (2 * hd + 1) * LANES:(2 * hd + 2) * LANES]
        dd = d0 - lam * d1
        ms = jnp.mean(dd * dd, axis=-1, keepdims=True)
        parts.append(((dd * lax.rsqrt(ms + NORM_EPS) * dn) * (1.0 - lambda_init)).astype(BF16))
    dcat = jnp.concatenate(parts, axis=1)
    mix = _dot(dcat, wd_ref[...]) + _dot(go_ref[...], wg_ref[...])
    out_ref[...] = x_ref[...] + mod_ref[2:3, :] * mix


def _mix_odd(x, mod, do, go, lam_p, dn, wd, wg, lambda_init):
    b, t, d = x.shape
    tm = TOKEN_TILE
    tok = lambda w: pl.BlockSpec((None, tm, w), lambda bi, i: (bi, i, 0))
    return pl.pallas_call(
        functools.partial(_mix_odd_kernel, lambda_init=lambda_init),
        grid=(b, t // tm),
        in_specs=[tok(d), pl.BlockSpec((None, 6, d), lambda bi, i: (bi, 0, 0)), tok(do.shape[-1]),
                  tok(go.shape[-1]), _full(lam_p), _full(dn), _full(wd), _full(wg)],
        out_specs=tok(d),
        out_shape=jax.ShapeDtypeStruct(x.shape, F32),
        compiler_params=_cparams("parallel", "parallel"),
        name="mix_odd",
    )(x, mod, do, go, lam_p, dn, wd, wg)


def _router_kernel(x_ref, mod_ref, g_ref, rw_ref, h_out, route_out):
    h = _rms_mod(x_ref[...], g_ref[...], mod_ref[4:5, :], mod_ref[3:4, :])
    h_out[...] = h
    h1, h2, h3 = _split3(h)
    w1, w2, w3 = _split3(rw_ref[...])
    logits = (_dot(h1, w1) + _dot(h1, w2) + _dot(h2, w1) + _dot(h2, w2) + _dot(h1, w3) + _dot(h3, w1))
    lane = lax.broadcasted_iota(jnp.int32, logits.shape, 1).astype(F32)
    neg = -jnp.inf
    l1 = jnp.where(lane < N_EXPERTS, logits, neg)
    m1 = jnp.max(l1, axis=-1, keepdims=True)
    i1 = jnp.min(jnp.where(l1 == m1, lane, float(LANES)), axis=-1, keepdims=True)
    l2 = jnp.where(lane == i1, neg, l1)
    m2 = jnp.max(l2, axis=-1, keepdims=True)
    i2 = jnp.min(jnp.where(l2 == m2, lane, float(LANES)), axis=-1, keepdims=True)
    e = jnp.exp(m2 - m1)
    p1 = 1.0 / (1.0 + e)
    p2 = e / (1.0 + e)
    route = jnp.where(lane == 0.0, i1,
                      jnp.where(lane == 1.0, i2, jnp.where(lane == 2.0, p1, jnp.where(lane == 3.0, p2, 0.0))))
    route_out[...] = route[:, 0:SUBLANES]


def _router(x, mod, g, rw):
    b, t, d = x.shape
    tm = TOKEN_TILE
    tok = lambda w: pl.BlockSpec((None, tm, w), lambda bi, i: (bi, i, 0))
    return pl.pallas_call(
        _router_kernel,
        grid=(b, t // tm),
        in_specs=[tok(d), pl.BlockSpec((None, 6, d), lambda bi, i: (bi, 0, 0)), _full(g), _full(rw)],
        out_specs=[tok(d), tok(SUBLANES)],
        out_shape=[jax.ShapeDtypeStruct(x.shape, F32), jax.ShapeDtypeStruct((b, t, SUBLANES), F32)],
        compiler_params=_cparams("parallel", "parallel"),
        name="moe_router",
    )(x, mod, g, rw)


def _issue_row_gather(idx_ref, n, src_hbm, dst_ref, sem):
    def body(r, carry):
        pltpu.make_async_copy(src_hbm.at[pl.ds(idx_ref[0, r], 1), :], dst_ref.at[pl.ds(r, 1), :], sem).start()
        return carry
    lax.fori_loop(0, n, body, 0)


def _wait_row_gather(src_hbm, dst_ref, sem):
    pltpu.make_async_copy(src_hbm.at[pl.ds(0, dst_ref.shape[0]), :], dst_ref, sem).wait()


def _gather_kernel(idx_ref, src_hbm, out_ref, buf, sem):
    n = buf.shape[0]
    _issue_row_gather(idx_ref, n, src_hbm, buf, sem)
    _wait_row_gather(src_hbm, buf, sem)
    out_ref[...] = buf[...].astype(out_ref.dtype)


def _gather_rows(src, idx, out_dtype):
    nb, blk = idx.shape
    d = src.shape[1]
    return pl.pallas_call(
        _gather_kernel,
        grid=(nb,),
        in_specs=[pl.BlockSpec((None, 1, blk), lambda i: (i, 0, 0), memory_space=pltpu.SMEM),
                  pl.BlockSpec(memory_space=pl.ANY)],
        out_specs=pl.BlockSpec((blk, d), lambda i: (i, 0)),
        out_shape=jax.ShapeDtypeStruct((nb * blk, d), out_dtype),
        scratch_shapes=[pltpu.VMEM((blk, d), src.dtype), pltpu.SemaphoreType.DMA(())],
        compiler_params=_cparams("arbitrary"),
        name="moe_gather",
    )(idx.reshape(nb, 1, blk), src)


def _moe_kernel(be_ref, xs_ref, wg_ref, wu_ref, wd_ref, y_ref, acc_sc):
    j = pl.program_id(1)

    @pl.when(j == 0)
    def _():
        acc_sc[...] = jnp.zeros(acc_sc.shape, F32)

    x = xs_ref[...]
    act = (_silu(_dot(x, wg_ref[...])) * _dot(x, wu_ref[...])).astype(BF16)
    acc_sc[...] += _dot(act, wd_ref[...])

    @pl.when(j == pl.num_programs(1) - 1)
    def _():
        y_ref[...] = acc_sc[...]


def _moe_experts(xs, block_e, w_gu, w_down):
    rows, d = xs.shape
    f = w_down.shape[1]
    tf = _ffn_tile(f)
    nf = f // tf
    nb = rows // MOE_BLOCK
    grid_spec = pltpu.PrefetchScalarGridSpec(
        num_scalar_prefetch=1,
        grid=(nb, nf),
        in_specs=[pl.BlockSpec((MOE_BLOCK, d), lambda i, j, be: (i, 0)),
                  pl.BlockSpec((None, d, tf), lambda i, j, be: (be[i], 0, j)),
                  pl.BlockSpec((None, d, tf), lambda i, j, be: (be[i], 0, nf + j)),
                  pl.BlockSpec((None, tf, d), lambda i, j, be: (be[i], j, 0))],
        out_specs=pl.BlockSpec((MOE_BLOCK, d), lambda i, j, be: (i, 0)),
        scratch_shapes=[pltpu.VMEM((MOE_BLOCK, d), F32)],
    )
    return pl.pallas_call(
        _moe_kernel,
        grid_spec=grid_spec,
        out_shape=jax.ShapeDtypeStruct((rows, d), F32),
        compiler_params=_cparams("parallel", "arbitrary"),
        name="moe_experts",
    )(block_e, xs, w_gu, w_gu, w_down)


def _final_kernel(pos_ref, x_ref, mod_ref, rt_ref, fn_ref, y_hbm, out_ref, ybuf, sem):
    tm = x_ref.shape[0]
    _issue_row_gather(pos_ref, TOP_K * tm, y_hbm, ybuf, sem)
    _wait_row_gather(y_hbm, ybuf, sem)
    rt = rt_ref[...]
    y = rt[:, 2:3] * ybuf[0:tm, :] + rt[:, 3:4] * ybuf[tm:2 * tm, :]
    xo = x_ref[...] + mod_ref[5:6, :] * y
    ms = jnp.mean(xo * xo, axis=-1, keepdims=True)
    out_ref[...] = xo * lax.rsqrt(ms + NORM_EPS) * fn_ref[...]


def _moe_combine_final(x, mod, route, pos, y, fn):
    b, t, d = x.shape
    tm = TOKEN_TILE
    nt = t // tm
    tok = lambda w: pl.BlockSpec((None, tm, w), lambda bi, i: (bi, i, 0))
    return pl.pallas_call(
        _final_kernel,
        grid=(b, nt),
        in_specs=[pl.BlockSpec((None, None, 1, TOP_K * tm), lambda bi, i: (bi, i, 0, 0), memory_space=pltpu.SMEM),
                  tok(d), pl.BlockSpec((None, 6, d), lambda bi, i: (bi, 0, 0)), tok(route.shape[-1]),
                  _full(fn), pl.BlockSpec(memory_space=pl.ANY)],
        out_specs=tok(d),
        out_shape=jax.ShapeDtypeStruct(x.shape, F32),
        scratch_shapes=[pltpu.VMEM((TOP_K * tm, d), F32), pltpu.SemaphoreType.DMA(())],
        compiler_params=_cparams("arbitrary", "arbitrary"),
        name="moe_combine_final",
    )(pos.reshape(b, nt, 1, TOP_K * tm), x, mod, route, fn, y)


def _moe_plan(top_e, n_tok):
    nk = n_tok * TOP_K
    flat_e = top_e.reshape(-1)
    onehot = (flat_e[:, None] == jnp.arange(N_EXPERTS, dtype=jnp.int32)[None, :]).astype(jnp.int32)
    rank = jnp.sum((jnp.cumsum(onehot, axis=0) - onehot) * onehot, axis=1)
    counts = jnp.sum(onehot, axis=0)
    padded = (counts + MOE_BLOCK - 1) // MOE_BLOCK * MOE_BLOCK
    pad_end = jnp.cumsum(padded)
    pad_start = pad_end - padded
    slot = (pad_start[flat_e] + rank).astype(jnp.int32)
    n_blocks = -(-nk // MOE_BLOCK) + N_EXPERTS
    rows = n_blocks * MOE_BLOCK
    flat_tok = jnp.repeat(jnp.arange(n_tok, dtype=jnp.int32), TOP_K)
    buf_tok = jnp.zeros((rows,), jnp.int32).at[slot].set(flat_tok)
    block_e = jnp.minimum(
        jnp.searchsorted(pad_end, jnp.arange(n_blocks, dtype=jnp.int32) * MOE_BLOCK, side="right"),
        N_EXPERTS - 1).astype(jnp.int32)
    return slot.reshape(n_tok, TOP_K), buf_tok.reshape(n_blocks, MOE_BLOCK), block_e


def _prep_even(w_in, w_uq, w_ukv, w_out):
    src, _, _, _ = _mla_layout()
    o = np.cumsum((0, MLA_Q_RANK, MLA_KV_RANK, MLA_ROPE, GDN_QKV, GDN_HEADS * GDN_DV, 2 * GDN_HEADS,
                   2 * GDN_HEADS))
    wcq, wckv = w_in[:, o[0]:o[1]], w_in[:, o[1]:o[2]]
    wkpe = _take_cols(w_in[:, o[2]:o[3]], np.where(src >= MLA_NOPE, src - MLA_NOPE, -1))
    wgqkv, wgz, wab = w_in[:, o[3]:o[4]], w_in[:, o[4]:o[5]], w_in[:, o[5]:o[7]]
    dq = MLA_NOPE + MLA_ROPE
    uq_idx = np.concatenate([np.where(src >= 0, h * dq + src, -1) for h in range(MLA_HEADS)])
    dkv = MLA_NOPE + MLA_V
    uk_idx = np.concatenate([np.where((src >= 0) & (src < MLA_NOPE), h * dkv + src, -1)
                             for h in range(MLA_HEADS)])
    uv_idx = np.concatenate([h * dkv + MLA_NOPE + np.arange(MLA_V) for h in range(MLA_HEADS)])
    wuq, wuk, wuv = _take_cols(w_uq, uq_idx), _take_cols(w_ukv, uk_idx), _take_cols(w_ukv, uv_idx)
    rows = -np.ones(MLA_HEADS * LANES, np.int64)
    for h in range(MLA_HEADS):
        base = h * LANES + (h % 2) * MLA_V
        rows[base:base + MLA_V] = h * MLA_V + np.arange(MLA_V)
    n_mla = MLA_HEADS * MLA_V
    wa = _take_cols(w_out[:n_mla].T, rows).T
    wg = w_out[n_mla:]
    ws = [w.astype(BF16) for w in (wcq, wckv, wkpe, wgqkv, wgz, wab)]
    return ws, [wuq.astype(BF16), wuk.astype(BF16), wuv.T.astype(BF16)], wa.astype(BF16), wg.astype(BF16)


def _prep_odd(w_in, q_norm, k_norm, w_out):
    m, src, _, _, _ = _pair_layout()
    nd = DIFF_HEADS * 2 * DIFF_DH
    o_dq, o_dk, o_dv, o_gq = 0, nd, 2 * nd, 3 * nd
    o_gk = o_gq + GQA_HEADS * GQA_DH
    o_gv = o_gk + GQA_KV_HEADS * GQA_DH
    pair = np.concatenate([h * 2 * DIFF_DH + m * DIFF_DH + src for h in range(DIFF_HEADS)])
    wdq = _take_cols(w_in, o_dq + pair)
    wdk = _take_cols(w_in, o_dk + pair)
    wdv = w_in[:, o_dv:o_gq].T
    grp = GQA_HEADS // GQA_KV_HEADS
    gq_idx = np.concatenate([np.where(m == h // grp, o_gq + h * GQA_DH + src, -1) for h in range(GQA_HEADS)])
    wgq = _take_cols(w_in, gq_idx)
    wgk = _take_cols(w_in, o_gk + m * GQA_DH + src)
    wgv = w_in[:, o_gv:o_gv + GQA_KV_HEADS * GQA_DH].T
    gqg = jnp.tile(q_norm[src], GQA_HEADS).reshape(1, -1)
    gkg = k_norm[src].reshape(1, -1)
    rows = -np.ones(GQA_HEADS * LANES, np.int64)
    for h in range(GQA_HEADS):
        base = h * LANES + (h // grp) * GQA_DH
        rows[base:base + GQA_DH] = h * GQA_DH + np.arange(GQA_DH)
    wd = w_out[:nd]
    wg = _take_cols(w_out[nd:].T, rows).T
    ws = [w.astype(BF16) for w in (wdq, wdk, wdv, wgq, wgk, wgv)]
    return ws, [gqg.astype(F32), gkg.astype(F32)], wd.astype(BF16), wg.astype(BF16)


def _layer_mods(m_layer, batch):
    d = m_layer.shape[1] // 6
    lat = m_layer[:batch].reshape(batch, 6, d)
    ctx = jnp.broadcast_to(m_layer[batch].reshape(1, 6, d), (batch, 6, d))
    return lat, ctx


def kernel(x, c, ctx, c_ctx, mod_w, mod_b, norm_g, ev_w_in, ev_mla_q_norm, ev_mla_kv_norm, ev_mla_w_uq,
           ev_mla_w_ukv, ev_gdn_conv, ev_gdn_a_log, ev_gdn_dt_bias, ev_gdn_out_norm, ev_w_out, ev_ffn_w_gu,
           ev_ffn_w_down, od_w_in, od_diff_lambda, od_diff_norm, od_gqa_q_norm, od_gqa_k_norm, od_w_out,
           od_router_w, od_moe_w_gu, od_moe_w_down, final_norm):
    batch, t_lat, d = x.shape
    t_ctx = ctx.shape[1]
    depth = mod_w.shape[0]
    assert depth == 2 and batch < 16

    cs = jnp.zeros((16, d), F32).at[:batch].set(c).at[batch].set(c_ctx)
    mods = _modulation(cs, mod_w, mod_b)

    j = 0
    mod_lat, mod_ctx = _layer_mods(mods[0], batch)
    g1 = norm_g[0, 0].reshape(1, d)
    g2 = norm_g[0, 1].reshape(1, d)
    ws, ups, wa, wg = _prep_even(ev_w_in[j], ev_mla_w_uq[j], ev_mla_w_ukv[j], ev_w_out[j])
    ws = ws + [ev_mla_q_norm[j].reshape(1, -1), ev_mla_kv_norm[j].reshape(1, -1)] + ups
    _, kind, freq, sign = _mla_layout()
    qscale = (MLA_NOPE + MLA_ROPE) ** -0.5 * LOG2E
    nfq = MLA_ROPE // 4
    tabs_lat = (_rope_tables(t_lat, kind, freq, sign, nfq, qscale, True)
                + _rope_tables(t_lat, kind, freq, sign, nfq, 1.0, True))
    tabs_ctx = (_rope_tables(t_ctx, kind, freq, sign, nfq, qscale, False)
                + _rope_tables(t_ctx, kind, freq, sign, nfq, 1.0, False))
    ql, kl, vl, gqkv_l, gz_l, ab_l = _even_proj(x, mod_lat, g1, tabs_lat, ws)
    qc, kc, vc, gqkv_c, gz_c, ab_c = _even_proj(ctx, mod_ctx, g1, tabs_ctx, ws)

    same = lambda hg: hg
    k_all = jnp.concatenate([kc, kl], axis=1)
    v_all = jnp.concatenate([vc, vl], axis=2)
    mla_kw = dict(groups=MLA_HEADS, g=1, kmap=same, vmap=lambda hg: hg // 2, out_dtype=BF16)
    ao_l = _flash(ql, k_all, v_all, **mla_kw)
    ao_c = _flash(qc, kc, vc, **mla_kw)

    nh2 = 2 * GDN_HEADS
    prm = jnp.zeros((2, 2 * nh2), F32)
    prm = prm.at[0, :nh2].set(ev_gdn_a_log[j].reshape(-1)).at[1, :nh2].set(ev_gdn_dt_bias[j].reshape(-1))
    eseg = _seg_ones(GDN_HEADS * GDN_DK, GDN_DK)
    prep_l = _gdn_prep(gqkv_l, ab_l, ev_gdn_conv[j], prm, eseg)
    prep_c = _gdn_prep(gqkv_c, ab_c, ev_gdn_conv[j], prm, eseg)
    go_c, go_l = _gdn_scan(prep_c, prep_l)

    on = jnp.tile(ev_gdn_out_norm[j], GDN_HEADS).reshape(1, -1)
    x = _mix_even(x, mod_lat, ao_l, go_l, gz_l, on, eseg, wa, wg)
    ctx = _mix_even(ctx, mod_ctx, ao_c, go_c, gz_c, on, eseg, wa, wg)
    w_gu, w_dn = ev_ffn_w_gu[j].astype(BF16), ev_ffn_w_down[j].astype(BF16)
    x = _ffn(x, mod_lat, g2, w_gu, w_dn)
    ctx = _ffn(ctx, mod_ctx, g2, w_gu, w_dn)

    lambda_init = 0.8 - 0.6 * math.exp(-0.3 * 1)
    mod_lat, mod_ctx = _layer_mods(mods[1], batch)
    g1 = norm_g[1, 0].reshape(1, d)
    g2 = norm_g[1, 1].reshape(1, d)
    ws, gains, wd, wgx = _prep_odd(od_w_in[j], od_gqa_q_norm[j], od_gqa_k_norm[j], od_w_out[j])
    _, _, kind, freq, sign = _pair_layout()
    nfq = DIFF_DH // 4
    tabs_lat = _rope_tables(t_lat, kind, freq, sign, nfq, 1.0, True)
    tabs_ctx = _rope_tables(t_ctx, kind, freq, sign, nfq, 1.0, False)
    qscale = DIFF_DH ** -0.5 * LOG2E
    dq, dk_l, dv_l, gq, gk_l, gv_l = _odd_proj(x, mod_lat, g1, tabs_lat, ws, gains, qscale)
    _, dk_c, dv_c, _, gk_c, gv_c = _odd_proj(ctx, mod_ctx, g1, tabs_ctx, ws, gains, qscale)
    cat = lambda a, b_: jnp.concatenate([a, b_], axis=1)
    cat_vt = lambda a, b_: jnp.concatenate([a, b_], axis=2)
    do = _flash(dq, cat(dk_c, dk_l), cat_vt(dv_c, dv_l), groups=DIFF_HEADS, g=2, kmap=same, vmap=same,
                out_dtype=F32)
    zero = lambda hg: 0
    go = _flash(gq, cat(gk_c, gk_l), cat_vt(gv_c, gv_l), groups=1, g=GQA_HEADS, kmap=zero, vmap=zero,
                out_dtype=BF16)
    x = _mix_odd(x, mod_lat, do, go, od_diff_lambda[j], od_diff_norm[j].reshape(1, -1), wd, wgx, lambda_init)

    rw = jnp.zeros((d, LANES), F32).at[:, :N_EXPERTS].set(od_router_w[j])
    h2, route = _router(x, mod_lat, g2, rw)
    n_tok = batch * t_lat
    top_e = route[..., 0:TOP_K].astype(jnp.int32).reshape(n_tok, TOP_K)
    slot, buf_tok, block_e = _moe_plan(top_e, n_tok)
    xs = _gather_rows(h2.reshape(n_tok, d), buf_tok, BF16)
    y = _moe_experts(xs, block_e, od_moe_w_gu[j].astype(BF16), od_moe_w_down[j].astype(BF16))
    tm = TOKEN_TILE
    pos = slot.reshape(batch, t_lat // tm, tm, TOP_K).transpose(0, 1, 3, 2).reshape(batch, t_lat // tm, TOP_K * tm)
    return _moe_combine_final(x, mod_lat, route, pos, y, final_norm.reshape(1, d))
```

```python
import functools
import math

import numpy as np
import jax
import jax.numpy as jnp
from jax import lax
from jax.experimental import pallas as pl
from jax.experimental.pallas import tpu as pltpu

F32 = jnp.float32
BF16 = jnp.bfloat16

GRID_W = 64
ROPE_BASE = 10000.0
NORM_EPS = 1e-6
MLA_HEADS, MLA_NOPE, MLA_ROPE, MLA_V = 8, 64, 32, 64
MLA_Q_RANK, MLA_KV_RANK = 384, 256
GDN_HEADS, GDN_DK, GDN_DV, GDN_CONV, GDN_CHUNK = 8, 64, 64, 5, 64
GDN_QKV = GDN_HEADS * (2 * GDN_DK + GDN_DV)
DIFF_HEADS, DIFF_DH = 4, 64
GQA_HEADS, GQA_KV_HEADS, GQA_DH = 8, 2, 64
N_EXPERTS, TOP_K, MOE_BLOCK = 8, 2, 512

LANES = 128
SUBLANES = 8
VMEM_LIMIT = 48 * 1024 * 1024
LOG2E = 1.4426950408889634

TOKEN_TILE = 256
ATTN_ROWS = 1024
ATTN_KEYS = 256
ATTN_UNROLL = 8


def _cparams(*sem):
    return pltpu.CompilerParams(dimension_semantics=sem, vmem_limit_bytes=VMEM_LIMIT)


def _dot(a, b):
    return jnp.dot(a, b, preferred_element_type=F32)


def _dot_t(a, b):
    return lax.dot_general(a, b, (((1,), (1,)), ((), ())), preferred_element_type=F32)


def _split3(x):
    h1 = x.astype(BF16)
    r1 = x - h1.astype(F32)
    h2 = r1.astype(BF16)
    h3 = (r1 - h2.astype(F32)).astype(BF16)
    return h1, h2, h3


def _silu(x):
    return x * jax.nn.sigmoid(x)


def _rms_mod(x, g, sc, sh):
    ms = jnp.mean(x * x, axis=-1, keepdims=True)
    return (x * lax.rsqrt(ms + NORM_EPS) * g) * (1.0 + sc) + sh


def _seg_sumsq(x, eseg):
    sq = x * x
    hi = sq.astype(BF16)
    lo = (sq - hi.astype(F32)).astype(BF16)
    return _dot(hi, eseg) + _dot(lo, eseg)


def _rope(x, c, s):
    return x * c + pltpu.roll(x, LANES // 2, 1) * s


def _mla_layout():
    src = -np.ones(LANES, np.int64)
    kind = np.zeros(LANES, np.int64)
    freq = np.zeros(LANES, np.int64)
    sign = np.zeros(LANES, np.float32)
    nf = MLA_ROPE // 4
    for f in range(nf):
        src[f] = MLA_NOPE + f; kind[f] = 1; freq[f] = f; sign[f] = -1.0
        src[nf + f] = MLA_NOPE + 2 * nf + f; kind[nf + f] = 2; freq[nf + f] = f; sign[nf + f] = -1.0
        src[64 + f] = MLA_NOPE + nf + f; kind[64 + f] = 1; freq[64 + f] = f; sign[64 + f] = 1.0
        src[64 + nf + f] = MLA_NOPE + 3 * nf + f; kind[64 + nf + f] = 2; freq[64 + nf + f] = f
        sign[64 + nf + f] = 1.0
    src[16:64] = np.arange(0, 48)
    src[80:96] = np.arange(48, 64)
    return src, kind, freq, sign


def _pair_layout():
    lane = np.arange(LANES)
    region = lane // 32
    m = region % 2
    is_b = region // 2
    within = lane % 32
    is_col = within // 16
    f = within % 16
    src = is_col * 32 + is_b * 16 + f
    kind = 1 + is_col
    sign = np.where(is_b == 0, -1.0, 1.0).astype(np.float32)
    return m, src, kind, f, sign


def _rope_tables(t_len, kind, freq, sign, n_freq, scale, with_pos):
    kind_j = jnp.asarray(kind)
    if not with_pos:
        c = jnp.full((t_len, LANES), scale, F32)
        return c, jnp.zeros((t_len, LANES), F32)
    n_rows = t_len // GRID_W
    inv = 1.0 / (ROPE_BASE ** (jnp.arange(n_freq, dtype=F32) / n_freq))
    rows = jnp.repeat(jnp.arange(n_rows, dtype=F32), GRID_W)
    cols = jnp.tile(jnp.arange(GRID_W, dtype=F32), n_rows)
    ang_r = (rows[:, None] * inv)[:, freq]
    ang_c = (cols[:, None] * inv)[:, freq]
    ang = jnp.where(kind_j[None, :] == 1, ang_r, ang_c)
    has = (kind_j > 0)[None, :]
    c = jnp.where(has, jnp.cos(ang), 1.0) * scale
    s = jnp.where(has, jnp.sin(ang) * jnp.asarray(sign)[None, :], 0.0) * scale
    return c.astype(F32), s.astype(F32)


def _take_cols(w, idx):
    idx = np.asarray(idx)
    wz = jnp.concatenate([w, jnp.zeros((w.shape[0], 1), w.dtype)], axis=1)
    return wz[:, np.where(idx < 0, w.shape[1], idx)]


def _seg_ones(width, seg):
    r = np.arange(width)
    return jnp.asarray((r[:, None] // seg) == (r[None, :] // seg), dtype=BF16)


def _mod_kernel(c_ref, w_ref, b_ref, o_ref):
    c = c_ref[...]
    s1, s2, s3 = _split3(_silu(c))
    w1, w2, _ = _split3(w_ref[...])
    acc = _dot(s1, w1) + _dot(s1, w2) + _dot(s2, w1) + _dot(s2, w2) + _dot(s3, w1)
    o_ref[...] = acc + b_ref[...]


def _modulation(cs, mod_w, mod_b):
    depth, d, n = mod_w.shape
    tn = 512
    return pl.pallas_call(
        _mod_kernel,
        grid=(depth, n // tn),
        in_specs=[
            pl.BlockSpec(cs.shape, lambda l, j: (0, 0)),
            pl.BlockSpec((None, d, tn), lambda l, j: (l, 0, j)),
            pl.BlockSpec((None, 1, tn), lambda l, j: (l, 0, j)),
        ],
        out_specs=pl.BlockSpec((None, cs.shape[0], tn), lambda l, j: (l, 0, j)),
        out_shape=jax.ShapeDtypeStruct((depth, cs.shape[0], n), F32),
        compiler_params=_cparams("parallel", "parallel"),
        name="modulation",
    )(cs, mod_w, mod_b.reshape(depth, 1, n))


def _even_proj_kernel(x_ref, mod_ref, g_ref, cq_t, sq_t, ck_t, sk_t,
                      wcq, wckv, wkpe, wgqkv, wgz, wab, qn, kvn, wuq, wuk, wuv,
                      q_out, k_out, v_out, gqkv_out, gz_out, ab_out):
    h = _rms_mod(x_ref[...], g_ref[...], mod_ref[1:2, :], mod_ref[0:1, :]).astype(BF16)
    cq = _dot(h, wcq[...])
    ckv = _dot(h, wckv[...])
    kpe = _dot(h, wkpe[...])
    nq = (cq * lax.rsqrt(jnp.mean(cq * cq, axis=-1, keepdims=True) + NORM_EPS) * qn[...]).astype(BF16)
    nkv = (ckv * lax.rsqrt(jnp.mean(ckv * ckv, axis=-1, keepdims=True) + NORM_EPS) * kvn[...]).astype(BF16)
    q = _dot(nq, wuq[...])
    kn = _dot(nkv, wuk[...])
    kper = _rope(kpe, ck_t[...], sk_t[...])
    cqv, sqv = cq_t[...], sq_t[...]
    for hd in range(MLA_HEADS):
        sl = slice(hd * LANES, (hd + 1) * LANES)
        q_out[:, sl] = _rope(q[:, sl], cqv, sqv).astype(BF16)
        k_out[:, sl] = (kn[:, sl] + kper).astype(BF16)
    _store_vt(v_out, _dot_t(wuv[...], nkv))
    gqkv_out[...] = _dot(h, wgqkv[...])
    gz_out[...] = _dot(h, wgz[...])
    ab_out[...] = _dot(h, wab[...])


def _full(a):
    nd = a.ndim
    return pl.BlockSpec(a.shape, lambda *_: (0,) * nd)


def _store_vt(vt_out, vt):
    for n in range(vt_out.shape[0]):
        vt_out[n] = vt[n * LANES:(n + 1) * LANES, :].astype(vt_out.dtype)


def _vt_spec(groups, tm):
    return pl.BlockSpec((None, groups, None, LANES, tm), lambda bi, i: (bi, 0, i, 0, 0))


def _vt_shape(b, groups, t, tm):
    assert tm == ATTN_KEYS
    return jax.ShapeDtypeStruct((b, groups, t // tm, LANES, tm), BF16)


def _even_proj(x, mod, g, tabs, ws):
    b, t, d = x.shape
    tm = TOKEN_TILE
    tok = lambda w: pl.BlockSpec((None, tm, w), lambda bi, i: (bi, i, 0))
    tab = pl.BlockSpec((tm, LANES), lambda bi, i: (i, 0))
    nvg = MLA_HEADS * MLA_V // LANES
    widths = (MLA_HEADS * LANES, MLA_HEADS * LANES, None, GDN_QKV, GDN_HEADS * GDN_DV, 4 * GDN_HEADS)
    dts = (BF16, BF16, None, F32, F32, F32)
    return pl.pallas_call(
        _even_proj_kernel,
        grid=(b, t // tm),
        in_specs=[tok(d), pl.BlockSpec((None, 6, d), lambda bi, i: (bi, 0, 0)), _full(g),
                  tab, tab, tab, tab] + [_full(w) for w in ws],
        out_specs=[_vt_spec(nvg, tm) if w is None else tok(w) for w in widths],
        out_shape=[_vt_shape(b, nvg, t, tm) if w is None else jax.ShapeDtypeStruct((b, t, w), dt)
                   for w, dt in zip(widths, dts)],
        compiler_params=_cparams("parallel", "parallel"),
        name="even_proj",
    )(x, mod, g, *tabs, *ws)


def _flash_kernel(q_ref, k_ref, vt_ref, o_ref, acc_sc, s_a, s_b, *, g, tq, tk, nsub):
    q = jnp.concatenate([q_ref[:, i * LANES:(i + 1) * LANES] for i in range(g)], axis=0)
    rows = g * tq
    acc_sc[...] = jnp.zeros(acc_sc.shape, F32)
    ones = jnp.ones((2 * SUBLANES, tk), BF16)

    def scores(jj):
        return _dot_t(k_ref[pl.ds(pl.multiple_of(jj * tk, tk), tk), :], q)

    def update(st, jj, m):
        m_new = jnp.maximum(m, jnp.max(st, axis=0, keepdims=True))
        p = jnp.exp2(st - m_new).astype(BF16)
        lhs = jnp.concatenate([vt_ref[jj], ones], axis=0)
        acc_sc[...] = jnp.exp2(m - m_new) * acc_sc[...] + _dot(lhs, p)
        return m_new

    bufs = (s_a, s_b)
    s_a[...] = scores(0)

    def run(j0, count, m, prefetch_last):
        for u in range(count):
            if u + 1 < count or prefetch_last:
                bufs[(u + 1) % 2][...] = scores(jnp.minimum(j0 + u + 1, nsub - 1))
            m = update(bufs[u % 2][...], j0 + u, m)
        return m

    trips, tail = divmod(nsub, ATTN_UNROLL)
    m = lax.fori_loop(0, trips, lambda i, m: run(i * ATTN_UNROLL, ATTN_UNROLL, m, True),
                      jnp.full((1, rows), -jnp.inf, F32))
    run(trips * ATTN_UNROLL, tail, m, False)
    on = acc_sc[0:LANES, :] * (1.0 / acc_sc[LANES:LANES + 1, :])
    o = on.T
    for i in range(g):
        o_ref[:, i * LANES:(i + 1) * LANES] = o[i * tq:(i + 1) * tq, :].astype(o_ref.dtype)


def _flash(q, k, vt, *, groups, g, kmap, vmap, out_dtype):
    b, tq_total, _ = q.shape
    tk_total = k.shape[1]
    tq = min(ATTN_ROWS // g, tq_total)
    tk = ATTN_KEYS
    nsub = tk_total // tk
    assert vt.shape[2:] == (nsub, LANES, tk)
    rows = g * tq
    kern = functools.partial(_flash_kernel, g=g, tq=tq, tk=tk, nsub=nsub)
    return pl.pallas_call(
        kern,
        grid=(b, groups, tq_total // tq),
        in_specs=[
            pl.BlockSpec((None, tq, g * LANES), lambda bi, hg, i: (bi, i, hg)),
            pl.BlockSpec((None, tk_total, LANES), lambda bi, hg, i: (bi, 0, kmap(hg))),
            pl.BlockSpec((None, None, nsub, LANES, tk), lambda bi, hg, i: (bi, vmap(hg), 0, 0, 0)),
        ],
        out_specs=pl.BlockSpec((None, tq, g * LANES), lambda bi, hg, i: (bi, i, hg)),
        out_shape=jax.ShapeDtypeStruct(q.shape, out_dtype),
        scratch_shapes=[pltpu.VMEM((LANES + 2 * SUBLANES, rows), F32), pltpu.VMEM((tk, rows), F32),
                        pltpu.VMEM((tk, rows), F32)],
        compiler_params=_cparams("parallel", "parallel", "parallel"),
        name="flash_attention",
    )(q, k, vt)


def _gdn_prep_kernel(x_ref, prev_ref, next_ref, ab_ref, cw_ref, prm_ref, eseg_ref,
                     lf_ref, lb_ref, la_ref, q_out, k_out, v_out, s1_out, s2_out, xe_sc):
    i = pl.program_id(1)
    tm = x_ref.shape[0]
    halo = SUBLANES
    xe_sc[0:halo, :] = prev_ref[...] * (i > 0).astype(F32)
    xe_sc[halo:halo + tm, :] = x_ref[...]
    xe_sc[halo + tm:2 * halo + tm, :] = next_ref[...] * (i < pl.num_programs(1) - 1).astype(F32)
    acc = jnp.zeros(x_ref.shape, F32)
    for j in range(GDN_CONV):
        acc = acc + cw_ref[j:j + 1, :] * xe_sc[pl.ds(halo - GDN_CONV // 2 + j, tm), :]
    y = _silu(acc)
    hw = GDN_HEADS * GDN_DK
    eseg = eseg_ref[...]
    q = y[:, 0:hw]
    k = y[:, hw:2 * hw]
    q_out[...] = q * lax.rsqrt(_seg_sumsq(q, eseg) + NORM_EPS) * (GDN_DK ** -0.5)
    k_out[...] = k * lax.rsqrt(_seg_sumsq(k, eseg) + NORM_EPS)
    v_out[...] = y[:, 2 * hw:]

    ab = ab_ref[...]
    nh2 = 2 * GDN_HEADS
    lane = lax.broadcasted_iota(jnp.int32, ab.shape, 1)
    z = ab + prm_ref[1:2, :]
    softplus = jnp.maximum(z, 0.0) + jnp.log1p(jnp.exp(-jnp.abs(z)))
    gate = jnp.where(lane < nh2, -jnp.exp(prm_ref[0:1, :]) * softplus, 0.0)
    g1, g2, g3 = _split3(gate)
    lf, lb, la = lf_ref[...], lb_ref[...], la_ref[...]
    cum_f = _dot(lf, g1) + _dot(lf, g2) + _dot(lf, g3)
    cum_b = _dot(lb, g1) + _dot(lb, g2) + _dot(lb, g3)
    tot = _dot(la, g1) + _dot(la, g2) + _dot(la, g3)
    cum = jnp.where(lane < GDN_HEADS, cum_f, cum_b)
    s1_out[...] = jnp.where(lane < nh2, cum, jax.nn.sigmoid(ab))
    s2_out[...] = tot


def _gdn_prep(gqkv, ab, conv_w, prm, eseg):
    b, t, w = gqkv.shape
    tm = TOKEN_TILE
    nt8 = t // SUBLANES
    per = tm // SUBLANES
    r = np.arange(tm)
    same = (r[:, None] // GDN_CHUNK) == (r[None, :] // GDN_CHUNK)
    lf = jnp.asarray(same & (r[None, :] <= r[:, None]), dtype=BF16)
    lb = jnp.asarray(same & (r[None, :] >= r[:, None]), dtype=BF16)
    la = jnp.asarray(same, dtype=BF16)
    halo = gqkv.reshape(b, nt8, SUBLANES, w)
    tok = lambda wd: pl.BlockSpec((None, tm, wd), lambda bi, i: (bi, i, 0))
    hw = GDN_HEADS * GDN_DK
    nab = ab.shape[-1]
    return pl.pallas_call(
        _gdn_prep_kernel,
        grid=(b, t // tm),
        in_specs=[
            tok(w),
            pl.BlockSpec((None, None, SUBLANES, w), lambda bi, i: (bi, jnp.maximum(i * per - 1, 0), 0, 0)),
            pl.BlockSpec((None, None, SUBLANES, w), lambda bi, i: (bi, jnp.minimum((i + 1) * per, nt8 - 1), 0, 0)),
            tok(nab), _full(conv_w), _full(prm), _full(eseg), _full(lf), _full(lb), _full(la),
        ],
        out_specs=[tok(hw), tok(hw), tok(GDN_HEADS * GDN_DV), tok(nab), tok(nab)],
        out_shape=[jax.ShapeDtypeStruct((b, t, hw), F32), jax.ShapeDtypeStruct((b, t, hw), F32),
                   jax.ShapeDtypeStruct((b, t, GDN_HEADS * GDN_DV), F32),
                   jax.ShapeDtypeStruct((b, t, nab), F32), jax.ShapeDtypeStruct((b, t, nab), F32)],
        scratch_shapes=[pltpu.VMEM((tm + 2 * SUBLANES, w), F32)],
        compiler_params=_cparams("parallel", "parallel"),
        name="gdn_prep",
    )(gqkv, halo, halo, ab, conv_w, prm, eseg, lf, lb, la)


def _bmm(a, b):
    return jnp.einsum("nij,njk->nik", a, b, preferred_element_type=F32)


def _bmm_t(a, b):
    return jnp.einsum("nik,njk->nij", a, b, preferred_element_type=F32)


def _gdn_intra_kernel(q_ref, k_ref, v_ref, gr_ref, a_out, b_out, qe_out, o0_out, *, hb, cb):
    c_len = GDN_CHUNK
    n = hb * cb
    sgn = 1 - 2 * pl.program_id(0)
    row = lax.broadcasted_iota(jnp.int32, (c_len, c_len), 0)
    col = lax.broadcasted_iota(jnp.int32, (c_len, c_len), 1)
    ahead = (row - col) * sgn
    eye_f = (row == col).astype(F32)
    eye_b = jnp.broadcast_to(eye_f.astype(BF16), (n, c_len, c_len))
    q = q_ref[...].reshape(n, c_len, GDN_DK)
    k = k_ref[...].reshape(n, c_len, GDN_DK)
    v = v_ref[...].reshape(n, c_len, GDN_DV)
    g_cols = gr_ref[:, :, 0:1, :].reshape(n, 1, c_len)
    beta_row = gr_ref[:, :, 1:2, :].reshape(n, 1, c_len)
    gl = gr_ref[:, :, 2:3, :].reshape(n, 1, c_len)

    ones_b = jnp.broadcast_to(jnp.ones((c_len, c_len), BF16), (n, c_len, c_len))

    def on_sublanes(r, pieces):
        parts = _split3(eye_f * r)[:pieces]
        return functools.reduce(lambda a, b: a + b, [_bmm_t(p, ones_b) for p in parts])

    gc = on_sublanes(g_cols, 2)
    beta = on_sublanes(beta_row, 1)
    eg, ek, cd = jnp.exp(gc), jnp.exp(gl - gc), jnp.exp(gl)
    decay = jnp.where(ahead >= 0, jnp.exp(jnp.minimum(gc - g_cols, 0.0)), 0.0)
    kb = k * beta
    kbf = k.astype(BF16)
    low = jnp.where(ahead > 0, _bmm_t(kb.astype(BF16), kbf) * decay, 0.0)
    qk = _bmm_t(q.astype(BF16), kbf) * decay
    mpow = jnp.where((row // 8) == (col // 8), -low, 0.0)
    tinv = eye_f + mpow
    for _ in range(2):
        mb = mpow.astype(BF16)
        mpow = _bmm(mb, mb)
        tinv = tinv + _bmm(tinv.astype(BF16), mpow.astype(BF16))
    for s in (8, 16, 32):
        off = ((row // (2 * s)) == (col // (2 * s))) & ((row // s) != (col // s))
        tb = tinv.astype(BF16)
        tinv = tinv - _bmm(_bmm(tb, jnp.where(off, low, 0.0).astype(BF16)).astype(BF16), tb)
    tb = tinv.astype(BF16)
    ub = _bmm(tb, (v * beta).astype(BF16)).astype(BF16)
    wb = _bmm(tb, (kb * eg).astype(BF16)).astype(BF16)
    kdt = _bmm_t(eye_b, (k * ek).astype(BF16)).astype(BF16)
    a_out[...] = (cd * eye_f - _bmm(kdt, wb)).reshape(a_out.shape)
    b_out[...] = _bmm(kdt, ub).reshape(b_out.shape)
    qkb = qk.astype(BF16)
    qe_out[...] = (q * eg - _bmm(qkb, wb)).reshape(qe_out.shape)
    o0_out[...] = _bmm(qkb, ub).reshape(o0_out.shape)


def _gdn_intra(q, k, v, gr):
    b, h, tt, dk = q.shape
    assert dk == GDN_CHUNK and v.shape[-1] == GDN_CHUNK
    nc = tt // GDN_CHUNK
    hb, cb = h, 4
    rows = cb * GDN_CHUNK
    qkv = pl.BlockSpec((None, hb, rows, dk), lambda d, bi, c: (bi, 0, c, 0))
    tokb = pl.BlockSpec((None, None, hb, rows, dk), lambda d, bi, c: (d, bi, 0, c, 0))
    matb = pl.BlockSpec((None, None, hb, cb, dk, dk), lambda d, bi, c: (d, bi, 0, c, 0, 0))
    return pl.pallas_call(
        functools.partial(_gdn_intra_kernel, hb=hb, cb=cb),
        grid=(2, b, nc // cb),
        in_specs=[qkv, qkv, qkv,
                  pl.BlockSpec((None, None, hb, cb, SUBLANES, GDN_CHUNK), lambda d, bi, c: (d, bi, 0, c, 0, 0))],
        out_specs=[matb, matb, tokb, tokb],
        out_shape=[jax.ShapeDtypeStruct((2, b, h, nc, dk, dk), F32)] * 2
                  + [jax.ShapeDtypeStruct((2, b, h, tt, dk), F32)] * 2,
        compiler_params=_cparams("parallel", "parallel", "parallel"),
        name="gdn_intra",
    )(q, k, v, gr)


def _gdn_inter_kernel(af, bf, qf, of, ab, bb, qb, ob, o_f, o_b, s_sc):
    @pl.when(pl.program_id(1) == 0)
    def _():
        s_sc[...] = jnp.zeros(s_sc.shape, F32)

    kc = af.shape[1]
    c_len = GDN_CHUNK
    s_f, s_b = s_sc[0], s_sc[1]
    for i in range(kc):
        r = kc - 1 - i
        sf, sb = s_f.astype(BF16), s_b.astype(BF16)
        o_f[:, i * c_len:(i + 1) * c_len, :] = (
            _bmm(qf[:, i * c_len:(i + 1) * c_len, :].astype(BF16), sf) + of[:, i * c_len:(i + 1) * c_len, :])
        o_b[:, r * c_len:(r + 1) * c_len, :] = (
            _bmm(qb[:, r * c_len:(r + 1) * c_len, :].astype(BF16), sb) + ob[:, r * c_len:(r + 1) * c_len, :])
        s_f = _bmm(af[:, i].astype(BF16), sf) + bf[:, i]
        s_b = _bmm(ab[:, r].astype(BF16), sb) + bb[:, r]
    s_sc[0] = s_f
    s_sc[1] = s_b


def _gdn_inter(a, bm, qe, o0, nc_ctx):
    _, b, h, nc, dk, _ = a.shape
    kc = math.gcd(nc_ctx, nc - nc_ctx, 4)
    nb, nb_ctx = nc // kc, nc_ctx // kc
    rev = lambda c: jnp.where(c < nb_ctx, nb_ctx - 1 - c, nb - 1 - (c - nb_ctx))
    mat_f = pl.BlockSpec((None, None, h, kc, dk, dk), lambda bi, c: (0, bi, 0, c, 0, 0))
    mat_b = pl.BlockSpec((None, None, h, kc, dk, dk), lambda bi, c: (1, bi, 0, rev(c), 0, 0))
    tok_f = pl.BlockSpec((None, None, h, kc * GDN_CHUNK, dk), lambda bi, c: (0, bi, 0, c, 0))
    tok_b = pl.BlockSpec((None, None, h, kc * GDN_CHUNK, dk), lambda bi, c: (1, bi, 0, rev(c), 0))
    out_f = pl.BlockSpec((None, h, kc * GDN_CHUNK, dk), lambda bi, c: (bi, 0, c, 0))
    out_b = pl.BlockSpec((None, h, kc * GDN_CHUNK, dk), lambda bi, c: (bi, 0, rev(c), 0))
    osd = jax.ShapeDtypeStruct(qe.shape[1:], F32)
    return pl.pallas_call(
        _gdn_inter_kernel,
        grid=(b, nb),
        in_specs=[mat_f, mat_f, tok_f, tok_f, mat_b, mat_b, tok_b, tok_b],
        out_specs=[out_f, out_b],
        out_shape=[osd, osd],
        scratch_shapes=[pltpu.VMEM((2, h, dk, dk), F32)],
        compiler_params=_cparams("parallel", "arbitrary"),
        name="gdn_inter",
    )(a, bm, qe, o0, a, bm, qe, o0)


def _gdn_scan(prep_ctx, prep_lat):
    b, tc, _ = prep_ctx[0].shape
    nh = GDN_HEADS
    cat = [jnp.concatenate([c, l], axis=1) for c, l in zip(prep_ctx, prep_lat)]
    tt = cat[0].shape[1]
    q, k, v = (x.reshape(b, tt, nh, -1).transpose(0, 2, 1, 3) for x in cat[:3])
    s1t, s2t = cat[3].transpose(0, 2, 1), cat[4].transpose(0, 2, 1)
    nc = tt // GDN_CHUNK
    zero = jnp.zeros((b, nh, nc, GDN_CHUNK), F32)
    gr = jnp.stack([
        jnp.stack([x.reshape(b, nh, nc, GDN_CHUNK) for x in
                   (s1t[:, d * nh:(d + 1) * nh], s1t[:, (2 + d) * nh:(3 + d) * nh], s2t[:, d * nh:(d + 1) * nh])]
                  + [zero] * (SUBLANES - 3), axis=3)
        for d in range(2)])
    a, bm, qe, o0 = _gdn_intra(q, k, v, gr)
    o_f, o_b = _gdn_inter(a, bm, qe, o0, tc // GDN_CHUNK)
    o = (o_f + o_b).transpose(0, 2, 1, 3).reshape(b, tt, nh * GDN_DV)
    return o[:, :tc], o[:, tc:]


def _mix_even_kernel(x_ref, mod_ref, ao_ref, go_ref, gz_ref, on_ref, eseg_ref, wa_ref, wg_ref, out_ref):
    o = go_ref[...]
    ms = _seg_sumsq(o, eseg_ref[...]) * (1.0 / GDN_DV)
    y = o * lax.rsqrt(ms + NORM_EPS) * on_ref[...] * _silu(gz_ref[...])
    mix = _dot(ao_ref[...], wa_ref[...]) + _dot(y.astype(BF16), wg_ref[...])
    out_ref[...] = x_ref[...] + mod_ref[2:3, :] * mix


def _mix_even(x, mod, ao, go, gz, on, eseg, wa, wg):
    b, t, d = x.shape
    tm = TOKEN_TILE
    tok = lambda w: pl.BlockSpec((None, tm, w), lambda bi, i: (bi, i, 0))
    return pl.pallas_call(
        _mix_even_kernel,
        grid=(b, t // tm),
        in_specs=[tok(d), pl.BlockSpec((None, 6, d), lambda bi, i: (bi, 0, 0)), tok(ao.shape[-1]),
                  tok(go.shape[-1]), tok(gz.shape[-1]), _full(on), _full(eseg), _full(wa), _full(wg)],
        out_specs=tok(d),
        out_shape=jax.ShapeDtypeStruct(x.shape, F32),
        compiler_params=_cparams("parallel", "parallel"),
        name="mix_even",
    )(x, mod, ao, go, gz, on, eseg, wa, wg)


def _ffn_kernel(x_ref, mod_ref, g_ref, wg_ref, wu_ref, wd_ref, out_ref, h_sc, acc_sc):
    j = pl.program_id(2)

    @pl.when(j == 0)
    def _():
        h_sc[...] = _rms_mod(x_ref[...], g_ref[...], mod_ref[4:5, :], mod_ref[3:4, :]).astype(BF16)
        acc_sc[...] = jnp.zeros(acc_sc.shape, F32)

    h = h_sc[...]
    act = (_silu(_dot(h, wg_ref[...])) * _dot(h, wu_ref[...])).astype(BF16)
    acc_sc[...] += _dot(act, wd_ref[...])

    @pl.when(j == pl.num_programs(2) - 1)
    def _():
        out_ref[...] = x_ref[...] + mod_ref[5:6, :] * acc_sc[...]


def _ffn_tile(f):
    for cand in (1408, 1792, 1024, 896, 768, 512, 256, 128):
        if f % cand == 0:
            return cand
    raise ValueError(f"ffn width {f} is not a multiple of 128")


def _ffn(x, mod, g, w_gu, w_down):
    b, t, d = x.shape
    f = w_down.shape[0]
    tm = 512 if t % 512 == 0 else TOKEN_TILE
    tf = _ffn_tile(f)
    nf = f // tf
    tok = pl.BlockSpec((None, tm, d), lambda bi, i, j: (bi, i, 0))
    return pl.pallas_call(
        _ffn_kernel,
        grid=(b, t // tm, nf),
        in_specs=[tok, pl.BlockSpec((None, 6, d), lambda bi, i, j: (bi, 0, 0)),
                  pl.BlockSpec(g.shape, lambda bi, i, j: (0, 0)),
                  pl.BlockSpec((d, tf), lambda bi, i, j: (0, j)),
                  pl.BlockSpec((d, tf), lambda bi, i, j: (0, nf + j)),
                  pl.BlockSpec((tf, d), lambda bi, i, j: (j, 0))],
        out_specs=tok,
        out_shape=jax.ShapeDtypeStruct(x.shape, F32),
        scratch_shapes=[pltpu.VMEM((tm, d), BF16), pltpu.VMEM((tm, d), F32)],
        compiler_params=_cparams("parallel", "parallel", "arbitrary"),
        name="ffn",
    )(x, mod, g, w_gu, w_gu, w_down)


def _odd_proj_kernel(x_ref, mod_ref, g_ref, c_t, s_t, wdq, wdk, wdv, wgq, wgk, wgv, gqg, gkg,
                     dq_out, dk_out, dv_out, gq_out, gk_out, gv_out, *, qscale):
    h = _rms_mod(x_ref[...], g_ref[...], mod_ref[1:2, :], mod_ref[0:1, :]).astype(BF16)
    c, s = c_t[...], s_t[...]
    lane = lax.broadcasted_iota(jnp.int32, c.shape, 1)
    is_m1 = ((lane // 32) % 2) == 1
    dq = _dot(h, wdq[...])
    dk = _dot(h, wdk[...])
    for hd in range(DIFF_HEADS):
        sl = slice(hd * LANES, (hd + 1) * LANES)
        r = _rope(dq[:, sl], c, s) * qscale
        dq_out[:, 2 * hd * LANES:(2 * hd + 1) * LANES] = jnp.where(is_m1, 0.0, r).astype(BF16)
        dq_out[:, (2 * hd + 1) * LANES:(2 * hd + 2) * LANES] = jnp.where(is_m1, r, 0.0).astype(BF16)
        dk_out[:, sl] = _rope(dk[:, sl], c, s).astype(BF16)
    _store_vt(dv_out, _dot_t(wdv[...], h))
    gq = _dot(h, wgq[...])
    gqgv = gqg[...]
    inv_dh = 1.0 / GQA_DH
    for hd in range(GQA_HEADS):
        sl = slice(hd * LANES, (hd + 1) * LANES)
        xh = gq[:, sl]
        ms = jnp.sum(xh * xh, axis=-1, keepdims=True) * inv_dh
        xn = xh * lax.rsqrt(ms + NORM_EPS) * gqgv[:, sl]
        gq_out[:, sl] = (_rope(xn, c, s) * qscale).astype(BF16)
    gk = _dot(h, wgk[...])
    sq = gk * gk
    ms0 = jnp.sum(jnp.where(is_m1, 0.0, sq), axis=-1, keepdims=True) * inv_dh
    ms1 = jnp.sum(jnp.where(is_m1, sq, 0.0), axis=-1, keepdims=True) * inv_dh
    rs = jnp.where(is_m1, lax.rsqrt(ms1 + NORM_EPS), lax.rsqrt(ms0 + NORM_EPS))
    gk_out[...] = _rope(gk * rs * gkg[...], c, s).astype(BF16)
    _store_vt(gv_out, _dot_t(wgv[...], h))


def _odd_proj(x, mod, g, tabs, ws, gains, qscale):
    b, t, d = x.shape
    tm = TOKEN_TILE
    tok = lambda w: pl.BlockSpec((None, tm, w), lambda bi, i: (bi, i, 0))
    tab = pl.BlockSpec((tm, LANES), lambda bi, i: (i, 0))
    outs = ((2 * DIFF_HEADS * LANES, 0), (DIFF_HEADS * LANES, 0), (0, DIFF_HEADS),
            (GQA_HEADS * LANES, 0), (LANES, 0), (0, 1))
    return pl.pallas_call(
        functools.partial(_odd_proj_kernel, qscale=qscale),
        grid=(b, t // tm),
        in_specs=[tok(d), pl.BlockSpec((None, 6, d), lambda bi, i: (bi, 0, 0)), _full(g), tab, tab]
                 + [_full(w) for w in ws] + [_full(w) for w in gains],
        out_specs=[_vt_spec(n, tm) if n else tok(w) for w, n in outs],
        out_shape=[_vt_shape(b, n, t, tm) if n else jax.ShapeDtypeStruct((b, t, w), BF16) for w, n in outs],
        compiler_params=_cparams("parallel", "parallel"),
        name="odd_proj",
    )(x, mod, g, *tabs, *ws, *gains)


def _mix_odd_kernel(x_ref, mod_ref, do_ref, go_ref, lam_ref, dn_ref, wd_ref, wg_ref, out_ref, *, lambda_init):
    lp = lam_ref[...]
    lam = (jnp.exp(jnp.sum(lp[0:1, :] * lp[1:2, :], axis=-1, keepdims=True))
           - jnp.exp(jnp.sum(lp[2:3, :] * lp[3:4, :], axis=-1, keepdims=True)) + lambda_init)
    dn = dn_ref[...]
    parts = []
    for hd in range(DIFF_HEADS):
        d0 = do_ref[:, 2 * hd * LANES:(2 * hd + 1) * LANES]
        d1 = do_ref[:, (2 * hd + 1) * LANES:(2 * hd + 2) * LANES]
        dd = d0 - lam * d1
        ms = jnp.mean(dd * dd, axis=-1, keepdims=True)
        parts.append(((dd * lax.rsqrt(ms + NORM_EPS) * dn) * (1.0 - lambda_init)).astype(BF16))
    dcat = jnp.concatenate(parts, axis=1)
    mix = _dot(dcat, wd_ref[...]) + _dot(go_ref[...], wg_ref[...])
    out_ref[...] = x_ref[...] + mod_ref[2:3, :] * mix


def _mix_odd(x, mod, do, go, lam_p, dn, wd, wg, lambda_init):
    b, t, d = x.shape
    tm = TOKEN_TILE
    tok = lambda w: pl.BlockSpec((None, tm, w), lambda bi, i: (bi, i, 0))
    return pl.pallas_call(
        functools.partial(_mix_odd_kernel, lambda_init=lambda_init),
        grid=(b, t // tm),
        in_specs=[tok(d), pl.BlockSpec((None, 6, d), lambda bi, i: (bi, 0, 0)), tok(do.shape[-1]),
                  tok(go.shape[-1]), _full(lam_p), _full(dn), _full(wd), _full(wg)],
        out_specs=tok(d),
        out_shape=jax.ShapeDtypeStruct(x.shape, F32),
        compiler_params=_cparams("parallel", "parallel"),
        name="mix_odd",
    )(x, mod, do, go, lam_p, dn, wd, wg)


def _router_kernel(x_ref, mod_ref, g_ref, rw_ref, h_out, route_out):
    h = _rms_mod(x_ref[...], g_ref[...], mod_ref[4:5, :], mod_ref[3:4, :])
    h_out[...] = h
    h1, h2, h3 = _split3(h)
    w1, w2, w3 = _split3(rw_ref[...])
    logits = (_dot(h1, w1) + _dot(h1, w2) + _dot(h2, w1) + _dot(h2, w2) + _dot(h1, w3) + _dot(h3, w1))
    lane = lax.broadcasted_iota(jnp.int32, logits.shape, 1).astype(F32)
    neg = -jnp.inf
    l1 = jnp.where(lane < N_EXPERTS, logits, neg)
    m1 = jnp.max(l1, axis=-1, keepdims=True)
    i1 = jnp.min(jnp.where(l1 == m1, lane, float(LANES)), axis=-1, keepdims=True)
    l2 = jnp.where(lane == i1, neg, l1)
    m2 = jnp.max(l2, axis=-1, keepdims=True)
    i2 = jnp.min(jnp.where(l2 == m2, lane, float(LANES)), axis=-1, keepdims=True)
    e = jnp.exp(m2 - m1)
    p1 = 1.0 / (1.0 + e)
    p2 = e / (1.0 + e)
    route = jnp.where(lane == 0.0, i1,
                      jnp.where(lane == 1.0, i2, jnp.where(lane == 2.0, p1, jnp.where(lane == 3.0, p2, 0.0))))
    route_out[...] = route[:, 0:SUBLANES]


def _router(x, mod, g, rw):
    b, t, d = x.shape
    tm = TOKEN_TILE
    tok = lambda w: pl.BlockSpec((None, tm, w), lambda bi, i: (bi, i, 0))
    return pl.pallas_call(
        _router_kernel,
        grid=(b, t // tm),
        in_specs=[tok(d), pl.BlockSpec((None, 6, d), lambda bi, i: (bi, 0, 0)), _full(g), _full(rw)],
        out_specs=[tok(d), tok(SUBLANES)],
        out_shape=[jax.ShapeDtypeStruct(x.shape, F32), jax.ShapeDtypeStruct((b, t, SUBLANES), F32)],
        compiler_params=_cparams("parallel", "parallel"),
        name="moe_router",
    )(x, mod, g, rw)


def _row_copies(idx_ref, n, src_of, dst_of, sem):
    def body(r, carry):
        i = idx_ref[0, r]
        pltpu.make_async_copy(src_of(r, i), dst_of(r, i), sem).start()
        return carry
    lax.fori_loop(0, n, body, 0)


def _moe_kernel(be_ref, tok_ref, tok_next_ref, dst_ref, h_hbm, wg_ref, wu_ref, wd_ref, y_hbm,
                xbuf, xb_sc, acc_sc, ybuf, gsem, ssem):
    i, j = pl.program_id(0), pl.program_id(1)
    nb, nf = pl.num_programs(0), pl.num_programs(1)
    blk = acc_sc.shape[0]
    row = lambda ref: (lambda r, _: ref.at[pl.ds(r, 1), :])
    picked = lambda ref: (lambda _, idx: ref.at[pl.ds(idx, 1), :])
    gather = lambda idx_ref, s: _row_copies(idx_ref, blk, picked(h_hbm), row(xbuf.at[s]), gsem.at[s])
    gather_wait = lambda s: pltpu.make_async_copy(h_hbm.at[pl.ds(0, blk), :], xbuf.at[s], gsem.at[s]).wait()
    scatter_wait = lambda: pltpu.make_async_copy(ybuf, y_hbm.at[pl.ds(0, blk), :], ssem).wait()

    @pl.when(j == 0)
    def _():
        @pl.when(i == 0)
        def _():
            gather(tok_ref, 0)

        for s in range(2):
            @pl.when(i % 2 == s)
            def _():
                gather_wait(s)

                @pl.when(i + 1 < nb)
                def _():
                    gather(tok_next_ref, 1 - s)

                xb_sc[...] = xbuf[s].astype(BF16)

        acc_sc[...] = jnp.zeros(acc_sc.shape, F32)

    x = xb_sc[...]
    act = (_silu(_dot(x, wg_ref[...])) * _dot(x, wu_ref[...])).astype(BF16)
    acc_sc[...] += _dot(act, wd_ref[...])

    @pl.when(j == nf - 1)
    def _():
        @pl.when(i > 0)
        def _():
            scatter_wait()

        ybuf[...] = acc_sc[...]
        _row_copies(dst_ref, blk, row(ybuf), picked(y_hbm), ssem)

        @pl.when(i == nb - 1)
        def _():
            scatter_wait()


def _moe_experts(h, buf_tok, dest, block_e, w_gu, w_down):
    nb, blk = buf_tok.shape
    d = h.shape[1]
    f = w_down.shape[1]
    tf = _ffn_tile(f)
    nf = f // tf
    idx3 = lambda a: a.reshape(nb, 1, blk)
    smem = lambda imap: pl.BlockSpec((None, 1, blk), imap, memory_space=pltpu.SMEM)
    grid_spec = pltpu.PrefetchScalarGridSpec(
        num_scalar_prefetch=1,
        grid=(nb, nf),
        in_specs=[smem(lambda i, j, be: (i, 0, 0)),
                  smem(lambda i, j, be: (jnp.minimum(i + 1, nb - 1), 0, 0)),
                  smem(lambda i, j, be: (i, 0, 0)),
                  pl.BlockSpec(memory_space=pl.ANY),
                  pl.BlockSpec((None, d, tf), lambda i, j, be: (be[i], 0, j)),
                  pl.BlockSpec((None, d, tf), lambda i, j, be: (be[i], 0, nf + j)),
                  pl.BlockSpec((None, tf, d), lambda i, j, be: (be[i], j, 0))],
        out_specs=pl.BlockSpec(memory_space=pl.ANY),
        scratch_shapes=[pltpu.VMEM((2, blk, d), F32), pltpu.VMEM((blk, d), BF16), pltpu.VMEM((blk, d), F32),
                        pltpu.VMEM((blk, d), F32), pltpu.SemaphoreType.DMA((2,)), pltpu.SemaphoreType.DMA(())],
    )
    return pl.pallas_call(
        _moe_kernel,
        grid_spec=grid_spec,
        out_shape=jax.ShapeDtypeStruct((nb * blk, d), F32),
        compiler_params=_cparams("arbitrary", "arbitrary"),
        name="moe_experts",
    )(block_e, idx3(buf_tok), idx3(buf_tok), idx3(dest), h, w_gu, w_gu, w_down)


def _final_kernel(x_ref, mod_ref, rt_ref, fn_ref, y_ref, out_ref):
    d = x_ref.shape[1]
    rt = rt_ref[...]
    y = rt[:, 2:3] * y_ref[:, 0:d] + rt[:, 3:4] * y_ref[:, d:2 * d]
    xo = x_ref[...] + mod_ref[5:6, :] * y
    ms = jnp.mean(xo * xo, axis=-1, keepdims=True)
    out_ref[...] = xo * lax.rsqrt(ms + NORM_EPS) * fn_ref[...]


def _moe_combine_final(x, mod, route, y2, fn):
    b, t, d = x.shape
    tm = TOKEN_TILE
    nt = t // tm
    tok = lambda w: pl.BlockSpec((None, tm, w), lambda bi, i: (bi, i, 0))
    return pl.pallas_call(
        _final_kernel,
        grid=(b, nt),
        in_specs=[tok(d), pl.BlockSpec((None, 6, d), lambda bi, i: (bi, 0, 0)), tok(route.shape[-1]),
                  _full(fn), pl.BlockSpec((tm, TOP_K * d), lambda bi, i: (bi * nt + i, 0))],
        out_specs=tok(d),
        out_shape=jax.ShapeDtypeStruct(x.shape, F32),
        compiler_params=_cparams("parallel", "parallel"),
        name="moe_combine_final",
    )(x, mod, route, fn, y2)


def _moe_plan(top_e, n_tok):
    nk = n_tok * TOP_K
    flat_e = top_e.reshape(-1)
    onehot = (flat_e[:, None] == jnp.arange(N_EXPERTS, dtype=jnp.int32)[None, :]).astype(jnp.int32)
    rank = jnp.sum((jnp.cumsum(onehot, axis=0) - onehot) * onehot, axis=1)
    counts = jnp.sum(onehot, axis=0)
    padded = (counts + MOE_BLOCK - 1) // MOE_BLOCK * MOE_BLOCK
    pad_end = jnp.cumsum(padded)
    pad_start = pad_end - padded
    slot = (pad_start[flat_e] + rank).astype(jnp.int32)
    n_blocks = -(-nk // MOE_BLOCK) + N_EXPERTS
    rows = n_blocks * MOE_BLOCK
    assign = jnp.full((rows,), -1, jnp.int32).at[slot].set(jnp.arange(nk, dtype=jnp.int32))
    is_pad = assign < 0
    pad_rank = jnp.cumsum(is_pad.astype(jnp.int32)) - 1
    buf_tok = jnp.where(is_pad, 0, assign // TOP_K)
    dest = jnp.where(is_pad, nk + pad_rank, assign)
    block_e = jnp.minimum(
        jnp.searchsorted(pad_end, jnp.arange(n_blocks, dtype=jnp.int32) * MOE_BLOCK, side="right"),
        N_EXPERTS - 1).astype(jnp.int32)
    return buf_tok.reshape(n_blocks, MOE_BLOCK), dest.reshape(n_blocks, MOE_BLOCK), block_e


def _prep_even(w_in, w_uq, w_ukv, w_out):
    src, _, _, _ = _mla_layout()
    o = np.cumsum((0, MLA_Q_RANK, MLA_KV_RANK, MLA_ROPE, GDN_QKV, GDN_HEADS * GDN_DV, 2 * GDN_HEADS,
                   2 * GDN_HEADS))
    wcq, wckv = w_in[:, o[0]:o[1]], w_in[:, o[1]:o[2]]
    wkpe = _take_cols(w_in[:, o[2]:o[3]], np.where(src >= MLA_NOPE, src - MLA_NOPE, -1))
    wgqkv, wgz, wab = w_in[:, o[3]:o[4]], w_in[:, o[4]:o[5]], w_in[:, o[5]:o[7]]
    dq = MLA_NOPE + MLA_ROPE
    uq_idx = np.concatenate([np.where(src >= 0, h * dq + src, -1) for h in range(MLA_HEADS)])
    dkv = MLA_NOPE + MLA_V
    uk_idx = np.concatenate([np.where((src >= 0) & (src < MLA_NOPE), h * dkv + src, -1)
                             for h in range(MLA_HEADS)])
    uv_idx = np.concatenate([h * dkv + MLA_NOPE + np.arange(MLA_V) for h in range(MLA_HEADS)])
    wuq, wuk, wuv = _take_cols(w_uq, uq_idx), _take_cols(w_ukv, uk_idx), _take_cols(w_ukv, uv_idx)
    rows = -np.ones(MLA_HEADS * LANES, np.int64)
    for h in range(MLA_HEADS):
        base = h * LANES + (h % 2) * MLA_V
        rows[base:base + MLA_V] = h * MLA_V + np.arange(MLA_V)
    n_mla = MLA_HEADS * MLA_V
    wa = _take_cols(w_out[:n_mla].T, rows).T
    wg = w_out[n_mla:]
    ws = [w.astype(BF16) for w in (wcq, wckv, wkpe, wgqkv, wgz, wab)]
    return ws, [wuq.astype(BF16), wuk.astype(BF16), wuv.T.astype(BF16)], wa.astype(BF16), wg.astype(BF16)


def _prep_odd(w_in, q_norm, k_norm, w_out):
    m, src, _, _, _ = _pair_layout()
    nd = DIFF_HEADS * 2 * DIFF_DH
    o_dq, o_dk, o_dv, o_gq = 0, nd, 2 * nd, 3 * nd
    o_gk = o_gq + GQA_HEADS * GQA_DH
    o_gv = o_gk + GQA_KV_HEADS * GQA_DH
    pair = np.concatenate([h * 2 * DIFF_DH + m * DIFF_DH + src for h in range(DIFF_HEADS)])
    wdq = _take_cols(w_in, o_dq + pair)
    wdk = _take_cols(w_in, o_dk + pair)
    wdv = w_in[:, o_dv:o_gq].T
    grp = GQA_HEADS // GQA_KV_HEADS
    gq_idx = np.concatenate([np.where(m == h // grp, o_gq + h * GQA_DH + src, -1) for h in range(GQA_HEADS)])
    wgq = _take_cols(w_in, gq_idx)
    wgk = _take_cols(w_in, o_gk + m * GQA_DH + src)
    wgv = w_in[:, o_gv:o_gv + GQA_KV_HEADS * GQA_DH].T
    gqg = jnp.tile(q_norm[src], GQA_HEADS).reshape(1, -1)
    gkg = k_norm[src].reshape(1, -1)
    rows = -np.ones(GQA_HEADS * LANES, np.int64)
    for h in range(GQA_HEADS):
        base = h * LANES + (h // grp) * GQA_DH
        rows[base:base + GQA_DH] = h * GQA_DH + np.arange(GQA_DH)
    wd = w_out[:nd]
    wg = _take_cols(w_out[nd:].T, rows).T
    ws = [w.astype(BF16) for w in (wdq, wdk, wdv, wgq, wgk, wgv)]
    return ws, [gqg.astype(F32), gkg.astype(F32)], wd.astype(BF16), wg.astype(BF16)


def _layer_mods(m_layer, batch):
    d = m_layer.shape[1] // 6
    lat = m_layer[:batch].reshape(batch, 6, d)
    ctx = jnp.broadcast_to(m_layer[batch].reshape(1, 6, d), (batch, 6, d))
    return lat, ctx


def kernel(x, c, ctx, c_ctx, mod_w, mod_b, norm_g, ev_w_in, ev_mla_q_norm, ev_mla_kv_norm, ev_mla_w_uq,
           ev_mla_w_ukv, ev_gdn_conv, ev_gdn_a_log, ev_gdn_dt_bias, ev_gdn_out_norm, ev_w_out, ev_ffn_w_gu,
           ev_ffn_w_down, od_w_in, od_diff_lambda, od_diff_norm, od_gqa_q_norm, od_gqa_k_norm, od_w_out,
           od_router_w, od_moe_w_gu, od_moe_w_down, final_norm):
    batch, t_lat, d = x.shape
    t_ctx = ctx.shape[1]
    depth = mod_w.shape[0]
    assert depth == 2 and batch < 16

    cs = jnp.zeros((16, d), F32).at[:batch].set(c).at[batch].set(c_ctx)
    mods = _modulation(cs, mod_w, mod_b)

    j = 0
    mod_lat, mod_ctx = _layer_mods(mods[0], batch)
    g1 = norm_g[0, 0].reshape(1, d)
    g2 = norm_g[0, 1].reshape(1, d)
    ws, ups, wa, wg = _prep_even(ev_w_in[j], ev_mla_w_uq[j], ev_mla_w_ukv[j], ev_w_out[j])
    ws = ws + [ev_mla_q_norm[j].reshape(1, -1), ev_mla_kv_norm[j].reshape(1, -1)] + ups
    _, kind, freq, sign = _mla_layout()
    qscale = (MLA_NOPE + MLA_ROPE) ** -0.5 * LOG2E
    nfq = MLA_ROPE // 4
    tabs_lat = (_rope_tables(t_lat, kind, freq, sign, nfq, qscale, True)
                + _rope_tables(t_lat, kind, freq, sign, nfq, 1.0, True))
    tabs_ctx = (_rope_tables(t_ctx, kind, freq, sign, nfq, qscale, False)
                + _rope_tables(t_ctx, kind, freq, sign, nfq, 1.0, False))
    ql, kl, vl, gqkv_l, gz_l, ab_l = _even_proj(x, mod_lat, g1, tabs_lat, ws)
    qc, kc, vc, gqkv_c, gz_c, ab_c = _even_proj(ctx, mod_ctx, g1, tabs_ctx, ws)

    same = lambda hg: hg
    k_all = jnp.concatenate([kc, kl], axis=1)
    v_all = jnp.concatenate([vc, vl], axis=2)
    mla_kw = dict(groups=MLA_HEADS, g=1, kmap=same, vmap=lambda hg: hg // 2, out_dtype=BF16)
    ao_l = _flash(ql, k_all, v_all, **mla_kw)
    ao_c = _flash(qc, kc, vc, **mla_kw)

    nh2 = 2 * GDN_HEADS
    prm = jnp.zeros((2, 2 * nh2), F32)
    prm = prm.at[0, :nh2].set(ev_gdn_a_log[j].reshape(-1)).at[1, :nh2].set(ev_gdn_dt_bias[j].reshape(-1))
    eseg = _seg_ones(GDN_HEADS * GDN_DK, GDN_DK)
    prep_l = _gdn_prep(gqkv_l, ab_l, ev_gdn_conv[j], prm, eseg)
    prep_c = _gdn_prep(gqkv_c, ab_c, ev_gdn_conv[j], prm, eseg)
    go_c, go_l = _gdn_scan(prep_c, prep_l)

    on = jnp.tile(ev_gdn_out_norm[j], GDN_HEADS).reshape(1, -1)
    x = _mix_even(x, mod_lat, ao_l, go_l, gz_l, on, eseg, wa, wg)
    ctx = _mix_even(ctx, mod_ctx, ao_c, go_c, gz_c, on, eseg, wa, wg)
    w_gu, w_dn = ev_ffn_w_gu[j].astype(BF16), ev_ffn_w_down[j].astype(BF16)
    x = _ffn(x, mod_lat, g2, w_gu, w_dn)
    ctx = _ffn(ctx, mod_ctx, g2, w_gu, w_dn)

    lambda_init = 0.8 - 0.6 * math.exp(-0.3 * 1)
    mod_lat, mod_ctx = _layer_mods(mods[1], batch)
    g1 = norm_g[1, 0].reshape(1, d)
    g2 = norm_g[1, 1].reshape(1, d)
    ws, gains, wd, wgx = _prep_odd(od_w_in[j], od_gqa_q_norm[j], od_gqa_k_norm[j], od_w_out[j])
    _, _, kind, freq, sign = _pair_layout()
    nfq = DIFF_DH // 4
    tabs_lat = _rope_tables(t_lat, kind, freq, sign, nfq, 1.0, True)
    tabs_ctx = _rope_tables(t_ctx, kind, freq, sign, nfq, 1.0, False)
    qscale = DIFF_DH ** -0.5 * LOG2E
    dq, dk_l, dv_l, gq, gk_l, gv_l = _odd_proj(x, mod_lat, g1, tabs_lat, ws, gains, qscale)
    _, dk_c, dv_c, _, gk_c, gv_c = _odd_proj(ctx, mod_ctx, g1, tabs_ctx, ws, gains, qscale)
    cat = lambda a, b_: jnp.concatenate([a, b_], axis=1)
    cat_vt = lambda a, b_: jnp.concatenate([a, b_], axis=2)
    do = _flash(dq, cat(dk_c, dk_l), cat_vt(dv_c, dv_l), groups=DIFF_HEADS, g=2, kmap=same, vmap=same,
                out_dtype=F32)
    zero = lambda hg: 0
    go = _flash(gq, cat(gk_c, gk_l), cat_vt(gv_c, gv_l), groups=1, g=GQA_HEADS, kmap=zero, vmap=zero,
                out_dtype=BF16)
    x = _mix_odd(x, mod_lat, do, go, od_diff_lambda[j], od_diff_norm[j].reshape(1, -1), wd, wgx, lambda_init)

    rw = jnp.zeros((d, LANES), F32).at[:, :N_EXPERTS].set(od_router_w[j])
    h2, route = _router(x, mod_lat, g2, rw)
    n_tok = batch * t_lat
    top_e = route[..., 0:TOP_K].astype(jnp.int32).reshape(n_tok, TOP_K)
    buf_tok, dest, block_e = _moe_plan(top_e, n_tok)
    y = _moe_experts(h2.reshape(n_tok, d), buf_tok, dest, block_e, od_moe_w_gu[j].astype(BF16),
                     od_moe_w_down[j].astype(BF16))
    y2 = y.reshape(y.shape[0] // TOP_K, TOP_K * d)
    return _moe_combine_final(x, mod_lat, route, y2, final_norm.reshape(1, d))
```

```python
import functools
import math

import numpy as np
import jax
import jax.numpy as jnp
from jax import lax
from jax.experimental import pallas as pl
from jax.experimental.pallas import tpu as pltpu

F32 = jnp.float32
BF16 = jnp.bfloat16

GRID_W = 64
ROPE_BASE = 10000.0
NORM_EPS = 1e-6
MLA_HEADS, MLA_NOPE, MLA_ROPE, MLA_V = 8, 64, 32, 64
MLA_Q_RANK, MLA_KV_RANK = 384, 256
GDN_HEADS, GDN_DK, GDN_DV, GDN_CONV, GDN_CHUNK = 8, 64, 64, 5, 64
GDN_QKV = GDN_HEADS * (2 * GDN_DK + GDN_DV)
DIFF_HEADS, DIFF_DH = 4, 64
GQA_HEADS, GQA_KV_HEADS, GQA_DH = 8, 2, 64
N_EXPERTS, TOP_K, MOE_BLOCK = 8, 2, 512

LANES = 128
SUBLANES = 8
VMEM_LIMIT = 48 * 1024 * 1024
LOG2E = 1.4426950408889634

TOKEN_TILE = 256
ATTN_ROWS = 1024
ATTN_KEYS = 256
ATTN_UNROLL = 8


def _cparams(*sem):
    return pltpu.CompilerParams(dimension_semantics=sem, vmem_limit_bytes=VMEM_LIMIT)


def _dot(a, b):
    return jnp.dot(a, b, preferred_element_type=F32)


def _dot_t(a, b):
    return lax.dot_general(a, b, (((1,), (1,)), ((), ())), preferred_element_type=F32)


def _split3(x):
    h1 = x.astype(BF16)
    r1 = x - h1.astype(F32)
    h2 = r1.astype(BF16)
    h3 = (r1 - h2.astype(F32)).astype(BF16)
    return h1, h2, h3


def _silu(x):
    return x * jax.nn.sigmoid(x)


def _rms_mod(x, g, sc, sh):
    ms = jnp.mean(x * x, axis=-1, keepdims=True)
    return (x * lax.rsqrt(ms + NORM_EPS) * g) * (1.0 + sc) + sh


def _seg_sumsq(x, eseg):
    sq = x * x
    hi = sq.astype(BF16)
    lo = (sq - hi.astype(F32)).astype(BF16)
    return _dot(hi, eseg) + _dot(lo, eseg)


def _rope(x, c, s):
    return x * c + pltpu.roll(x, LANES // 2, 1) * s


def _mla_layout():
    src = -np.ones(LANES, np.int64)
    kind = np.zeros(LANES, np.int64)
    freq = np.zeros(LANES, np.int64)
    sign = np.zeros(LANES, np.float32)
    nf = MLA_ROPE // 4
    for f in range(nf):
        src[f] = MLA_NOPE + f; kind[f] = 1; freq[f] = f; sign[f] = -1.0
        src[nf + f] = MLA_NOPE + 2 * nf + f; kind[nf + f] = 2; freq[nf + f] = f; sign[nf + f] = -1.0
        src[64 + f] = MLA_NOPE + nf + f; kind[64 + f] = 1; freq[64 + f] = f; sign[64 + f] = 1.0
        src[64 + nf + f] = MLA_NOPE + 3 * nf + f; kind[64 + nf + f] = 2; freq[64 + nf + f] = f
        sign[64 + nf + f] = 1.0
    src[16:64] = np.arange(0, 48)
    src[80:96] = np.arange(48, 64)
    return src, kind, freq, sign


def _pair_layout():
    lane = np.arange(LANES)
    region = lane // 32
    m = region % 2
    is_b = region // 2
    within = lane % 32
    is_col = within // 16
    f = within % 16
    src = is_col * 32 + is_b * 16 + f
    kind = 1 + is_col
    sign = np.where(is_b == 0, -1.0, 1.0).astype(np.float32)
    return m, src, kind, f, sign


def _rope_tables(t_len, kind, freq, sign, n_freq, scale, with_pos):
    kind_j = jnp.asarray(kind)
    if not with_pos:
        c = jnp.full((t_len, LANES), scale, F32)
        return c, jnp.zeros((t_len, LANES), F32)
    n_rows = t_len // GRID_W
    inv = 1.0 / (ROPE_BASE ** (jnp.arange(n_freq, dtype=F32) / n_freq))
    rows = jnp.repeat(jnp.arange(n_rows, dtype=F32), GRID_W)
    cols = jnp.tile(jnp.arange(GRID_W, dtype=F32), n_rows)
    ang_r = (rows[:, None] * inv)[:, freq]
    ang_c = (cols[:, None] * inv)[:, freq]
    ang = jnp.where(kind_j[None, :] == 1, ang_r, ang_c)
    has = (kind_j > 0)[None, :]
    c = jnp.where(has, jnp.cos(ang), 1.0) * scale
    s = jnp.where(has, jnp.sin(ang) * jnp.asarray(sign)[None, :], 0.0) * scale
    return c.astype(F32), s.astype(F32)


def _take_cols(w, idx):
    idx = np.asarray(idx)
    wz = jnp.concatenate([w, jnp.zeros((w.shape[0], 1), w.dtype)], axis=1)
    return wz[:, np.where(idx < 0, w.shape[1], idx)]


def _seg_ones(width, seg):
    r = np.arange(width)
    return jnp.asarray((r[:, None] // seg) == (r[None, :] // seg), dtype=BF16)


def _mod_kernel(c_ref, w_ref, b_ref, o_ref):
    c = c_ref[...]
    s1, s2, s3 = _split3(_silu(c))
    w1, w2, _ = _split3(w_ref[...])
    acc = _dot(s1, w1) + _dot(s1, w2) + _dot(s2, w1) + _dot(s2, w2) + _dot(s3, w1)
    o_ref[...] = acc + b_ref[...]


def _modulation(cs, mod_w, mod_b):
    depth, d, n = mod_w.shape
    tn = 512
    return pl.pallas_call(
        _mod_kernel,
        grid=(depth, n // tn),
        in_specs=[
            pl.BlockSpec(cs.shape, lambda l, j: (0, 0)),
            pl.BlockSpec((None, d, tn), lambda l, j: (l, 0, j)),
            pl.BlockSpec((None, 1, tn), lambda l, j: (l, 0, j)),
        ],
        out_specs=pl.BlockSpec((None, cs.shape[0], tn), lambda l, j: (l, 0, j)),
        out_shape=jax.ShapeDtypeStruct((depth, cs.shape[0], n), F32),
        compiler_params=_cparams("parallel", "parallel"),
        name="modulation",
    )(cs, mod_w, mod_b.reshape(depth, 1, n))


def _even_proj_kernel(x_ref, mod_ref, g_ref, cq_t, sq_t, ck_t, sk_t,
                      wcq, wckv, wkpe, wgqkv, wgz, wab, qn, kvn, wuq, wuk, wuv, *rest):
    q_out, k_out, v_out, gqkv_out, gz_out, ab_out = rest[-6:]
    h = _rms_mod(x_ref[...], g_ref[...], mod_ref[1:2, :], mod_ref[0:1, :]).astype(BF16)
    cq = _dot(h, wcq[...])
    ckv = _dot(h, wckv[...])
    kpe = _dot(h, wkpe[...])
    nq = (cq * lax.rsqrt(jnp.mean(cq * cq, axis=-1, keepdims=True) + NORM_EPS) * qn[...]).astype(BF16)
    nkv = (ckv * lax.rsqrt(jnp.mean(ckv * ckv, axis=-1, keepdims=True) + NORM_EPS) * kvn[...]).astype(BF16)
    q = _dot(nq, wuq[...])
    kn = _dot(nkv, wuk[...])
    kper = _rope(kpe, ck_t[...], sk_t[...])
    cqv, sqv = cq_t[...], sq_t[...]
    for hd in range(MLA_HEADS):
        sl = slice(hd * LANES, (hd + 1) * LANES)
        q_out[:, sl] = _rope(q[:, sl], cqv, sqv).astype(BF16)
        k_out[:, sl] = (kn[:, sl] + kper).astype(BF16)
    _store_vt(v_out, _dot_t(wuv[...], nkv))
    gqkv_out[...] = _dot(h, wgqkv[...])
    gz_out[...] = _dot(h, wgz[...])
    ab_out[...] = _dot(h, wab[...])


def _full(a):
    nd = a.ndim
    return pl.BlockSpec(a.shape, lambda *_: (0,) * nd)


def _store_vt(vt_out, vt):
    for n in range(vt_out.shape[0]):
        vt_out[n] = vt[n * LANES:(n + 1) * LANES, :].astype(vt_out.dtype)


def _proj_outputs(b, t, tt, tile_off, outs):
    tm = TOKEN_TILE
    assert tm == ATTN_KEYS
    specs, shapes = [], []
    for w, dt, kind in outs:
        if kind == "own":
            specs.append(pl.BlockSpec((None, tm, w), lambda bi, i: (bi, i, 0)))
            shapes.append(jax.ShapeDtypeStruct((b, t, w), dt))
        elif kind == "keys":
            specs.append(pl.BlockSpec((None, tm, w), lambda bi, i: (bi, i + tile_off, 0)))
            shapes.append(jax.ShapeDtypeStruct((b, tt, w), dt))
        else:
            specs.append(pl.BlockSpec((None, kind, None, LANES, tm), lambda bi, i: (bi, 0, i + tile_off, 0, 0)))
            shapes.append(jax.ShapeDtypeStruct((b, kind, tt // tm, LANES, tm), dt))
    shared = [n for n, (_, _, kind) in enumerate(outs) if kind != "own"]
    return specs, shapes, shared


def _even_proj(x, mod, g, tabs, ws, tt, tile_off, bufs=None):
    b, t, d = x.shape
    tm = TOKEN_TILE
    tok = lambda w: pl.BlockSpec((None, tm, w), lambda bi, i: (bi, i, 0))
    tab = pl.BlockSpec((tm, LANES), lambda bi, i: (i, 0))
    outs = ((MLA_HEADS * LANES, BF16, "own"), (MLA_HEADS * LANES, BF16, "keys"),
            (0, BF16, MLA_HEADS * MLA_V // LANES), (GDN_QKV, F32, "own"), (GDN_HEADS * GDN_DV, F32, "own"),
            (4 * GDN_HEADS, F32, "own"))
    specs, shapes, shared = _proj_outputs(b, t, tt, tile_off, outs)
    args = [x, mod, g, *tabs, *ws]
    in_specs = [tok(d), pl.BlockSpec((None, 6, d), lambda bi, i: (bi, 0, 0)), _full(g),
                tab, tab, tab, tab] + [_full(w) for w in ws]
    aliases = {}
    if bufs is not None:
        aliases = {len(args) + n: o for n, o in enumerate(shared)}
        in_specs = in_specs + [pl.BlockSpec(memory_space=pl.ANY)] * len(bufs)
        args = args + list(bufs)
    return pl.pallas_call(
        _even_proj_kernel,
        grid=(b, t // tm),
        in_specs=in_specs,
        out_specs=specs,
        out_shape=shapes,
        input_output_aliases=aliases,
        compiler_params=_cparams("parallel", "parallel"),
        name="even_proj",
    )(*args)


def _flash_kernel(q_ref, k_ref, vt_ref, o_ref, acc_sc, s_a, s_b, *, g, tq, tk, nsub):
    q = jnp.concatenate([q_ref[:, i * LANES:(i + 1) * LANES] for i in range(g)], axis=0)
    rows = g * tq
    acc_sc[...] = jnp.zeros(acc_sc.shape, F32)
    ones = jnp.ones((2 * SUBLANES, tk), BF16)

    def scores(jj):
        return _dot_t(k_ref[pl.ds(pl.multiple_of(jj * tk, tk), tk), :], q)

    def update(st, jj, m):
        m_new = jnp.maximum(m, jnp.max(st, axis=0, keepdims=True))
        p = jnp.exp2(st - m_new).astype(BF16)
        lhs = jnp.concatenate([vt_ref[jj], ones], axis=0)
        acc_sc[...] = jnp.exp2(m - m_new) * acc_sc[...] + _dot(lhs, p)
        return m_new

    bufs = (s_a, s_b)
    s_a[...] = scores(0)

    def run(j0, count, m, prefetch_last):
        for u in range(count):
            if u + 1 < count or prefetch_last:
                bufs[(u + 1) % 2][...] = scores(jnp.minimum(j0 + u + 1, nsub - 1))
            m = update(bufs[u % 2][...], j0 + u, m)
        return m

    trips, tail = divmod(nsub, ATTN_UNROLL)
    m = lax.fori_loop(0, trips, lambda i, m: run(i * ATTN_UNROLL, ATTN_UNROLL, m, True),
                      jnp.full((1, rows), -jnp.inf, F32))
    run(trips * ATTN_UNROLL, tail, m, False)
    on = acc_sc[0:LANES, :] * (1.0 / acc_sc[LANES:LANES + 1, :])
    o = on.T
    for i in range(g):
        o_ref[:, i * LANES:(i + 1) * LANES] = o[i * tq:(i + 1) * tq, :].astype(o_ref.dtype)


def _flash(q, k, vt, *, groups, g, kmap, vmap, out_dtype, n_keys=None):
    b, tq_total, _ = q.shape
    tk_total = k.shape[1] if n_keys is None else n_keys
    tq = min(ATTN_ROWS // g, tq_total)
    tk = ATTN_KEYS
    nsub = tk_total // tk
    assert vt.shape[3:] == (LANES, tk) and vt.shape[2] >= nsub
    rows = g * tq
    kern = functools.partial(_flash_kernel, g=g, tq=tq, tk=tk, nsub=nsub)
    return pl.pallas_call(
        kern,
        grid=(b, groups, tq_total // tq),
        in_specs=[
            pl.BlockSpec((None, tq, g * LANES), lambda bi, hg, i: (bi, i, hg)),
            pl.BlockSpec((None, tk_total, LANES), lambda bi, hg, i: (bi, 0, kmap(hg))),
            pl.BlockSpec((None, None, nsub, LANES, tk), lambda bi, hg, i: (bi, vmap(hg), 0, 0, 0)),
        ],
        out_specs=pl.BlockSpec((None, tq, g * LANES), lambda bi, hg, i: (bi, i, hg)),
        out_shape=jax.ShapeDtypeStruct(q.shape, out_dtype),
        scratch_shapes=[pltpu.VMEM((LANES + 2 * SUBLANES, rows), F32), pltpu.VMEM((tk, rows), F32),
                        pltpu.VMEM((tk, rows), F32)],
        compiler_params=_cparams("parallel", "parallel", "parallel"),
        name="flash_attention",
    )(q, k, vt)


def _gdn_prep_kernel(x_ref, prev_ref, next_ref, ab_ref, cw_ref, prm_ref, eseg_ref,
                     lf_ref, lb_ref, la_ref, *rest):
    q_out, k_out, v_out, s1_out, s2_out, xe_sc = rest[-6:]
    i = pl.program_id(1)
    tm = x_ref.shape[0]
    halo = SUBLANES
    xe_sc[0:halo, :] = prev_ref[...] * (i > 0).astype(F32)
    xe_sc[halo:halo + tm, :] = x_ref[...]
    xe_sc[halo + tm:2 * halo + tm, :] = next_ref[...] * (i < pl.num_programs(1) - 1).astype(F32)
    acc = jnp.zeros(x_ref.shape, F32)
    for j in range(GDN_CONV):
        acc = acc + cw_ref[j:j + 1, :] * xe_sc[pl.ds(halo - GDN_CONV // 2 + j, tm), :]
    y = _silu(acc)
    hw = GDN_HEADS * GDN_DK
    eseg = eseg_ref[...]
    q = y[:, 0:hw]
    k = y[:, hw:2 * hw]
    qn = q * lax.rsqrt(_seg_sumsq(q, eseg) + NORM_EPS) * (GDN_DK ** -0.5)
    kn = k * lax.rsqrt(_seg_sumsq(k, eseg) + NORM_EPS)
    v = y[:, 2 * hw:]
    for hd in range(GDN_HEADS):
        q_out[hd] = qn[:, hd * GDN_DK:(hd + 1) * GDN_DK]
        k_out[hd] = kn[:, hd * GDN_DK:(hd + 1) * GDN_DK]
        v_out[hd] = v[:, hd * GDN_DV:(hd + 1) * GDN_DV]

    ab = ab_ref[...]
    nh2 = 2 * GDN_HEADS
    lane = lax.broadcasted_iota(jnp.int32, ab.shape, 1)
    z = ab + prm_ref[1:2, :]
    softplus = jnp.maximum(z, 0.0) + jnp.log1p(jnp.exp(-jnp.abs(z)))
    gate = jnp.where(lane < nh2, -jnp.exp(prm_ref[0:1, :]) * softplus, 0.0)
    g1, g2, g3 = _split3(gate)
    lf, lb, la = lf_ref[...], lb_ref[...], la_ref[...]
    cum_f = _dot(lf, g1) + _dot(lf, g2) + _dot(lf, g3)
    cum_b = _dot(lb, g1) + _dot(lb, g2) + _dot(lb, g3)
    tot = _dot(la, g1) + _dot(la, g2) + _dot(la, g3)
    cum = jnp.where(lane < GDN_HEADS, cum_f, cum_b)
    s1_out[...] = jnp.where(lane < nh2, cum, jax.nn.sigmoid(ab))
    s2_out[...] = tot


def _gdn_prep(gqkv, ab, conv_w, prm, eseg, tt, tile_off, bufs=None):
    b, t, w = gqkv.shape
    tm = TOKEN_TILE
    nt8 = t // SUBLANES
    per = tm // SUBLANES
    r = np.arange(tm)
    same = (r[:, None] // GDN_CHUNK) == (r[None, :] // GDN_CHUNK)
    lf = jnp.asarray(same & (r[None, :] <= r[:, None]), dtype=BF16)
    lb = jnp.asarray(same & (r[None, :] >= r[:, None]), dtype=BF16)
    la = jnp.asarray(same, dtype=BF16)
    halo = gqkv.reshape(b, nt8, SUBLANES, w)
    tok = lambda wd: pl.BlockSpec((None, tm, wd), lambda bi, i: (bi, i, 0))
    hw = GDN_HEADS * GDN_DK
    nab = ab.shape[-1]
    head = lambda wd: pl.BlockSpec((None, GDN_HEADS, tm, wd), lambda bi, i: (bi, 0, i + tile_off, 0))
    sca = pl.BlockSpec((None, tm, nab), lambda bi, i: (bi, i + tile_off, 0))
    in_specs = [
        tok(w),
        pl.BlockSpec((None, None, SUBLANES, w), lambda bi, i: (bi, jnp.maximum(i * per - 1, 0), 0, 0)),
        pl.BlockSpec((None, None, SUBLANES, w), lambda bi, i: (bi, jnp.minimum((i + 1) * per, nt8 - 1), 0, 0)),
        tok(nab), _full(conv_w), _full(prm), _full(eseg), _full(lf), _full(lb), _full(la),
    ]
    args = [gqkv, halo, halo, ab, conv_w, prm, eseg, lf, lb, la]
    aliases = {}
    if bufs is not None:
        aliases = {len(args) + n: n for n in range(len(bufs))}
        in_specs = in_specs + [pl.BlockSpec(memory_space=pl.ANY)] * len(bufs)
        args = args + list(bufs)
    return pl.pallas_call(
        _gdn_prep_kernel,
        grid=(b, t // tm),
        in_specs=in_specs,
        out_specs=[head(GDN_DK), head(GDN_DK), head(GDN_DV), sca, sca],
        out_shape=[jax.ShapeDtypeStruct((b, GDN_HEADS, tt, GDN_DK), F32)] * 2
                  + [jax.ShapeDtypeStruct((b, GDN_HEADS, tt, GDN_DV), F32)]
                  + [jax.ShapeDtypeStruct((b, tt, nab), F32)] * 2,
        input_output_aliases=aliases,
        scratch_shapes=[pltpu.VMEM((tm + 2 * SUBLANES, w), F32)],
        compiler_params=_cparams("parallel", "parallel"),
        name="gdn_prep",
    )(*args)


def _bmm(a, b):
    return jnp.einsum("nij,njk->nik", a, b, preferred_element_type=F32)


def _bmm_t(a, b):
    return jnp.einsum("nik,njk->nij", a, b, preferred_element_type=F32)


def _gdn_intra_kernel(q_ref, k_ref, v_ref, gr_ref, a_out, b_out, qe_out, o0_out, *, hb, cb):
    c_len = GDN_CHUNK
    n = hb * cb
    sgn = 1 - 2 * pl.program_id(0)
    row = lax.broadcasted_iota(jnp.int32, (c_len, c_len), 0)
    col = lax.broadcasted_iota(jnp.int32, (c_len, c_len), 1)
    ahead = (row - col) * sgn
    eye_f = (row == col).astype(F32)
    eye_b = jnp.broadcast_to(eye_f.astype(BF16), (n, c_len, c_len))
    q = q_ref[...].reshape(n, c_len, GDN_DK)
    k = k_ref[...].reshape(n, c_len, GDN_DK)
    v = v_ref[...].reshape(n, c_len, GDN_DV)
    g_cols = gr_ref[:, :, 0:1, :].reshape(n, 1, c_len)
    beta_row = gr_ref[:, :, 1:2, :].reshape(n, 1, c_len)
    gl = gr_ref[:, :, 2:3, :].reshape(n, 1, c_len)

    ones_b = jnp.broadcast_to(jnp.ones((c_len, c_len), BF16), (n, c_len, c_len))

    def on_sublanes(r, pieces):
        parts = _split3(eye_f * r)[:pieces]
        return functools.reduce(lambda a, b: a + b, [_bmm_t(p, ones_b) for p in parts])

    gc = on_sublanes(g_cols, 2)
    beta = on_sublanes(beta_row, 1)
    eg, ek, cd = jnp.exp(gc), jnp.exp(gl - gc), jnp.exp(gl)
    decay = jnp.where(ahead >= 0, jnp.exp(jnp.minimum(gc - g_cols, 0.0)), 0.0)
    kb = k * beta
    kbf = k.astype(BF16)
    low = jnp.where(ahead > 0, _bmm_t(kb.astype(BF16), kbf) * decay, 0.0)
    qk = _bmm_t(q.astype(BF16), kbf) * decay
    mpow = jnp.where((row // 8) == (col // 8), -low, 0.0)
    tinv = eye_f + mpow
    for _ in range(2):
        mb = mpow.astype(BF16)
        mpow = _bmm(mb, mb)
        tinv = tinv + _bmm(tinv.astype(BF16), mpow.astype(BF16))
    for s in (8, 16, 32):
        off = ((row // (2 * s)) == (col // (2 * s))) & ((row // s) != (col // s))
        tb = tinv.astype(BF16)
        tinv = tinv - _bmm(_bmm(tb, jnp.where(off, low, 0.0).astype(BF16)).astype(BF16), tb)
    tb = tinv.astype(BF16)
    ub = _bmm(tb, (v * beta).astype(BF16)).astype(BF16)
    wb = _bmm(tb, (kb * eg).astype(BF16)).astype(BF16)
    kdt = _bmm_t(eye_b, (k * ek).astype(BF16)).astype(BF16)
    a_out[...] = (cd * eye_f - _bmm(kdt, wb)).reshape(a_out.shape)
    b_out[...] = _bmm(kdt, ub).reshape(b_out.shape)
    qkb = qk.astype(BF16)
    qe_out[...] = (q * eg - _bmm(qkb, wb)).reshape(qe_out.shape)
    o0_out[...] = _bmm(qkb, ub).reshape(o0_out.shape)


def _gdn_intra(q, k, v, gr):
    b, h, tt, dk = q.shape
    assert dk == GDN_CHUNK and v.shape[-1] == GDN_CHUNK
    nc = tt // GDN_CHUNK
    hb, cb = h, 4
    rows = cb * GDN_CHUNK
    qkv = pl.BlockSpec((None, hb, rows, dk), lambda d, bi, c: (bi, 0, c, 0))
    tokb = pl.BlockSpec((None, None, hb, rows, dk), lambda d, bi, c: (d, bi, 0, c, 0))
    matb = pl.BlockSpec((None, None, hb, cb, dk, dk), lambda d, bi, c: (d, bi, 0, c, 0, 0))
    return pl.pallas_call(
        functools.partial(_gdn_intra_kernel, hb=hb, cb=cb),
        grid=(2, b, nc // cb),
        in_specs=[qkv, qkv, qkv,
                  pl.BlockSpec((None, None, hb, cb, SUBLANES, GDN_CHUNK), lambda d, bi, c: (d, bi, 0, c, 0, 0))],
        out_specs=[matb, matb, tokb, tokb],
        out_shape=[jax.ShapeDtypeStruct((2, b, h, nc, dk, dk), F32)] * 2
                  + [jax.ShapeDtypeStruct((2, b, h, tt, dk), F32)] * 2,
        compiler_params=_cparams("parallel", "parallel", "parallel"),
        name="gdn_intra",
    )(q, k, v, gr)


def _gdn_inter_kernel(af, bf, qf, of, ab, bb, qb, ob, o_f, o_b, s_sc):
    @pl.when(pl.program_id(1) == 0)
    def _():
        s_sc[...] = jnp.zeros(s_sc.shape, F32)

    kc = af.shape[1]
    c_len = GDN_CHUNK
    s_f, s_b = s_sc[0], s_sc[1]
    for i in range(kc):
        r = kc - 1 - i
        sf, sb = s_f.astype(BF16), s_b.astype(BF16)
        o_f[:, i * c_len:(i + 1) * c_len, :] = (
            _bmm(qf[:, i * c_len:(i + 1) * c_len, :].astype(BF16), sf) + of[:, i * c_len:(i + 1) * c_len, :])
        o_b[:, r * c_len:(r + 1) * c_len, :] = (
            _bmm(qb[:, r * c_len:(r + 1) * c_len, :].astype(BF16), sb) + ob[:, r * c_len:(r + 1) * c_len, :])
        s_f = _bmm(af[:, i].astype(BF16), sf) + bf[:, i]
        s_b = _bmm(ab[:, r].astype(BF16), sb) + bb[:, r]
    s_sc[0] = s_f
    s_sc[1] = s_b


def _gdn_inter(a, bm, qe, o0, nc_ctx):
    _, b, h, nc, dk, _ = a.shape
    kc = math.gcd(nc_ctx, nc - nc_ctx, 4)
    nb, nb_ctx = nc // kc, nc_ctx // kc
    rev = lambda c: jnp.where(c < nb_ctx, nb_ctx - 1 - c, nb - 1 - (c - nb_ctx))
    mat_f = pl.BlockSpec((None, None, h, kc, dk, dk), lambda bi, c: (0, bi, 0, c, 0, 0))
    mat_b = pl.BlockSpec((None, None, h, kc, dk, dk), lambda bi, c: (1, bi, 0, rev(c), 0, 0))
    tok_f = pl.BlockSpec((None, None, h, kc * GDN_CHUNK, dk), lambda bi, c: (0, bi, 0, c, 0))
    tok_b = pl.BlockSpec((None, None, h, kc * GDN_CHUNK, dk), lambda bi, c: (1, bi, 0, rev(c), 0))
    out_f = pl.BlockSpec((None, h, kc * GDN_CHUNK, dk), lambda bi, c: (bi, 0, c, 0))
    out_b = pl.BlockSpec((None, h, kc * GDN_CHUNK, dk), lambda bi, c: (bi, 0, rev(c), 0))
    osd = jax.ShapeDtypeStruct(qe.shape[1:], F32)
    return pl.pallas_call(
        _gdn_inter_kernel,
        grid=(b, nb),
        in_specs=[mat_f, mat_f, tok_f, tok_f, mat_b, mat_b, tok_b, tok_b],
        out_specs=[out_f, out_b],
        out_shape=[osd, osd],
        scratch_shapes=[pltpu.VMEM((2, h, dk, dk), F32)],
        compiler_params=_cparams("parallel", "arbitrary"),
        name="gdn_inter",
    )(a, bm, qe, o0, a, bm, qe, o0)


def _gdn_scan(prep, tc):
    q, k, v, s1, s2 = prep
    b, nh, tt, _ = q.shape
    s1t, s2t = s1.transpose(0, 2, 1), s2.transpose(0, 2, 1)
    nc = tt // GDN_CHUNK
    zero = jnp.zeros((b, nh, nc, GDN_CHUNK), F32)
    gr = jnp.stack([
        jnp.stack([x.reshape(b, nh, nc, GDN_CHUNK) for x in
                   (s1t[:, d * nh:(d + 1) * nh], s1t[:, (2 + d) * nh:(3 + d) * nh], s2t[:, d * nh:(d + 1) * nh])]
                  + [zero] * (SUBLANES - 3), axis=3)
        for d in range(2)])
    a, bm, qe, o0 = _gdn_intra(q, k, v, gr)
    return _gdn_inter(a, bm, qe, o0, tc // GDN_CHUNK)


def _mix_even_kernel(x_ref, mod_ref, ao_ref, gf_ref, gb_ref, gz_ref, on_ref, eseg_ref, wa_ref, wg_ref, out_ref):
    o = jnp.concatenate([gf_ref[hd] + gb_ref[hd] for hd in range(GDN_HEADS)], axis=-1)
    ms = _seg_sumsq(o, eseg_ref[...]) * (1.0 / GDN_DV)
    y = o * lax.rsqrt(ms + NORM_EPS) * on_ref[...] * _silu(gz_ref[...])
    mix = _dot(ao_ref[...], wa_ref[...]) + _dot(y.astype(BF16), wg_ref[...])
    out_ref[...] = x_ref[...] + mod_ref[2:3, :] * mix


def _mix_even(x, mod, ao, go_f, go_b, tile_off, gz, on, eseg, wa, wg):
    b, t, d = x.shape
    tm = TOKEN_TILE
    tok = lambda w: pl.BlockSpec((None, tm, w), lambda bi, i: (bi, i, 0))
    head = pl.BlockSpec((None, GDN_HEADS, tm, GDN_DV), lambda bi, i: (bi, 0, i + tile_off, 0))
    return pl.pallas_call(
        _mix_even_kernel,
        grid=(b, t // tm),
        in_specs=[tok(d), pl.BlockSpec((None, 6, d), lambda bi, i: (bi, 0, 0)), tok(ao.shape[-1]),
                  head, head, tok(gz.shape[-1]), _full(on), _full(eseg), _full(wa), _full(wg)],
        out_specs=tok(d),
        out_shape=jax.ShapeDtypeStruct(x.shape, F32),
        compiler_params=_cparams("parallel", "parallel"),
        name="mix_even",
    )(x, mod, ao, go_f, go_b, gz, on, eseg, wa, wg)


def _ffn_kernel(x_ref, mod_ref, g_ref, wg_ref, wu_ref, wd_ref, out_ref, h_sc, acc_sc):
    j = pl.program_id(2)

    @pl.when(j == 0)
    def _():
        h_sc[...] = _rms_mod(x_ref[...], g_ref[...], mod_ref[4:5, :], mod_ref[3:4, :]).astype(BF16)
        acc_sc[...] = jnp.zeros(acc_sc.shape, F32)

    h = h_sc[...]
    act = (_silu(_dot(h, wg_ref[...])) * _dot(h, wu_ref[...])).astype(BF16)
    acc_sc[...] += _dot(act, wd_ref[...])

    @pl.when(j == pl.num_programs(2) - 1)
    def _():
        out_ref[...] = x_ref[...] + mod_ref[5:6, :] * acc_sc[...]


def _ffn_tile(f):
    for cand in (1408, 1792, 1024, 896, 768, 512, 256, 128):
        if f % cand == 0:
            return cand
    raise ValueError(f"ffn width {f} is not a multiple of 128")


def _ffn(x, mod, g, w_gu, w_down):
    b, t, d = x.shape
    f = w_down.shape[0]
    tm = 512 if t % 512 == 0 else TOKEN_TILE
    tf = _ffn_tile(f)
    nf = f // tf
    tok = pl.BlockSpec((None, tm, d), lambda bi, i, j: (bi, i, 0))
    return pl.pallas_call(
        _ffn_kernel,
        grid=(b, t // tm, nf),
        in_specs=[tok, pl.BlockSpec((None, 6, d), lambda bi, i, j: (bi, 0, 0)),
                  pl.BlockSpec(g.shape, lambda bi, i, j: (0, 0)),
                  pl.BlockSpec((d, tf), lambda bi, i, j: (0, j)),
                  pl.BlockSpec((d, tf), lambda bi, i, j: (0, nf + j)),
                  pl.BlockSpec((tf, d), lambda bi, i, j: (j, 0))],
        out_specs=tok,
        out_shape=jax.ShapeDtypeStruct(x.shape, F32),
        scratch_shapes=[pltpu.VMEM((tm, d), BF16), pltpu.VMEM((tm, d), F32)],
        compiler_params=_cparams("parallel", "parallel", "arbitrary"),
        name="ffn",
    )(x, mod, g, w_gu, w_gu, w_down)


def _odd_proj_kernel(x_ref, mod_ref, g_ref, c_t, s_t, wdq, wdk, wdv, wgq, wgk, wgv, gqg, gkg, *rest, qscale):
    dq_out, dk_out, dv_out, gq_out, gk_out, gv_out = rest[-6:]
    h = _rms_mod(x_ref[...], g_ref[...], mod_ref[1:2, :], mod_ref[0:1, :]).astype(BF16)
    c, s = c_t[...], s_t[...]
    lane = lax.broadcasted_iota(jnp.int32, c.shape, 1)
    is_m1 = ((lane // 32) % 2) == 1
    dq = _dot(h, wdq[...])
    dk = _dot(h, wdk[...])
    for hd in range(DIFF_HEADS):
        sl = slice(hd * LANES, (hd + 1) * LANES)
        r = _rope(dq[:, sl], c, s) * qscale
        dq_out[:, 2 * hd * LANES:(2 * hd + 1) * LANES] = jnp.where(is_m1, 0.0, r).astype(BF16)
        dq_out[:, (2 * hd + 1) * LANES:(2 * hd + 2) * LANES] = jnp.where(is_m1, r, 0.0).astype(BF16)
        dk_out[:, sl] = _rope(dk[:, sl], c, s).astype(BF16)
    _store_vt(dv_out, _dot_t(wdv[...], h))
    gq = _dot(h, wgq[...])
    gqgv = gqg[...]
    inv_dh = 1.0 / GQA_DH
    for hd in range(GQA_HEADS):
        sl = slice(hd * LANES, (hd + 1) * LANES)
        xh = gq[:, sl]
        ms = jnp.sum(xh * xh, axis=-1, keepdims=True) * inv_dh
        xn = xh * lax.rsqrt(ms + NORM_EPS) * gqgv[:, sl]
        gq_out[:, sl] = (_rope(xn, c, s) * qscale).astype(BF16)
    gk = _dot(h, wgk[...])
    sq = gk * gk
    ms0 = jnp.sum(jnp.where(is_m1, 0.0, sq), axis=-1, keepdims=True) * inv_dh
    ms1 = jnp.sum(jnp.where(is_m1, sq, 0.0), axis=-1, keepdims=True) * inv_dh
    rs = jnp.where(is_m1, lax.rsqrt(ms1 + NORM_EPS), lax.rsqrt(ms0 + NORM_EPS))
    gk_out[...] = _rope(gk * rs * gkg[...], c, s).astype(BF16)
    _store_vt(gv_out, _dot_t(wgv[...], h))


def _odd_proj(x, mod, g, tabs, ws, gains, qscale, tt, tile_off, bufs=None):
    b, t, d = x.shape
    tm = TOKEN_TILE
    tok = lambda w: pl.BlockSpec((None, tm, w), lambda bi, i: (bi, i, 0))
    tab = pl.BlockSpec((tm, LANES), lambda bi, i: (i, 0))
    outs = ((2 * DIFF_HEADS * LANES, BF16, "own"), (DIFF_HEADS * LANES, BF16, "keys"), (0, BF16, DIFF_HEADS),
            (GQA_HEADS * LANES, BF16, "own"), (LANES, BF16, "keys"), (0, BF16, 1))
    specs, shapes, shared = _proj_outputs(b, t, tt, tile_off, outs)
    args = [x, mod, g, *tabs, *ws, *gains]
    in_specs = ([tok(d), pl.BlockSpec((None, 6, d), lambda bi, i: (bi, 0, 0)), _full(g), tab, tab]
                + [_full(w) for w in ws] + [_full(w) for w in gains])
    aliases = {}
    if bufs is not None:
        aliases = {len(args) + n: o for n, o in enumerate(shared)}
        in_specs = in_specs + [pl.BlockSpec(memory_space=pl.ANY)] * len(bufs)
        args = args + list(bufs)
    return pl.pallas_call(
        functools.partial(_odd_proj_kernel, qscale=qscale),
        grid=(b, t // tm),
        in_specs=in_specs,
        out_specs=specs,
        out_shape=shapes,
        input_output_aliases=aliases,
        compiler_params=_cparams("parallel", "parallel"),
        name="odd_proj",
    )(*args)


def _mix_odd_kernel(x_ref, mod_ref, do_ref, go_ref, lam_ref, dn_ref, wd_ref, wg_ref, out_ref, *, lambda_init):
    lp = lam_ref[...]
    lam = (jnp.exp(jnp.sum(lp[0:1, :] * lp[1:2, :], axis=-1, keepdims=True))
           - jnp.exp(jnp.sum(lp[2:3, :] * lp[3:4, :], axis=-1, keepdims=True)) + lambda_init)
    dn = dn_ref[...]
    parts = []
    for hd in range(DIFF_HEADS):
        d0 = do_ref[:, 2 * hd * LANES:(2 * hd + 1) * LANES]
        d1 = do_ref[:, (2 * hd + 1) * LANES:(2 * hd + 2) * LANES]
        dd = d0 - lam * d1
        ms = jnp.mean(dd * dd, axis=-1, keepdims=True)
        parts.append(((dd * lax.rsqrt(ms + NORM_EPS) * dn) * (1.0 - lambda_init)).astype(BF16))
    dcat = jnp.concatenate(parts, axis=1)
    mix = _dot(dcat, wd_ref[...]) + _dot(go_ref[...], wg_ref[...])
    out_ref[...] = x_ref[...] + mod_ref[2:3, :] * mix


def _mix_odd(x, mod, do, go, lam_p, dn, wd, wg, lambda_init):
    b, t, d = x.shape
    tm = TOKEN_TILE
    tok = lambda w: pl.BlockSpec((None, tm, w), lambda bi, i: (bi, i, 0))
    return pl.pallas_call(
        functools.partial(_mix_odd_kernel, lambda_init=lambda_init),
        grid=(b, t // tm),
        in_specs=[tok(d), pl.BlockSpec((None, 6, d), lambda bi, i: (bi, 0, 0)), tok(do.shape[-1]),
                  tok(go.shape[-1]), _full(lam_p), _full(dn), _full(wd), _full(wg)],
        out_specs=tok(d),
        out_shape=jax.ShapeDtypeStruct(x.shape, F32),
        compiler_params=_cparams("parallel", "parallel"),
        name="mix_odd",
    )(x, mod, do, go, lam_p, dn, wd, wg)


def _router_kernel(x_ref, mod_ref, g_ref, rw_ref, h_out, route_out):
    h = _rms_mod(x_ref[...], g_ref[...], mod_ref[4:5, :], mod_ref[3:4, :])
    h_out[...] = h
    h1, h2, h3 = _split3(h)
    w1, w2, w3 = _split3(rw_ref[...])
    logits = (_dot(h1, w1) + _dot(h1, w2) + _dot(h2, w1) + _dot(h2, w2) + _dot(h1, w3) + _dot(h3, w1))
    lane = lax.broadcasted_iota(jnp.int32, logits.shape, 1).astype(F32)
    neg = -jnp.inf
    l1 = jnp.where(lane < N_EXPERTS, logits, neg)
    m1 = jnp.max(l1, axis=-1, keepdims=True)
    i1 = jnp.min(jnp.where(l1 == m1, lane, float(LANES)), axis=-1, keepdims=True)
    l2 = jnp.where(lane == i1, neg, l1)
    m2 = jnp.max(l2, axis=-1, keepdims=True)
    i2 = jnp.min(jnp.where(l2 == m2, lane, float(LANES)), axis=-1, keepdims=True)
    e = jnp.exp(m2 - m1)
    p1 = 1.0 / (1.0 + e)
    p2 = e / (1.0 + e)
    route = jnp.where(lane == 0.0, i1,
                      jnp.where(lane == 1.0, i2, jnp.where(lane == 2.0, p1, jnp.where(lane == 3.0, p2, 0.0))))
    route_out[...] = route[:, 0:SUBLANES]


def _router(x, mod, g, rw):
    b, t, d = x.shape
    tm = TOKEN_TILE
    tok = lambda w: pl.BlockSpec((None, tm, w), lambda bi, i: (bi, i, 0))
    return pl.pallas_call(
        _router_kernel,
        grid=(b, t // tm),
        in_specs=[tok(d), pl.BlockSpec((None, 6, d), lambda bi, i: (bi, 0, 0)), _full(g), _full(rw)],
        out_specs=[tok(d), tok(SUBLANES)],
        out_shape=[jax.ShapeDtypeStruct(x.shape, F32), jax.ShapeDtypeStruct((b, t, SUBLANES), F32)],
        compiler_params=_cparams("parallel", "parallel"),
        name="moe_router",
    )(x, mod, g, rw)


def _row_copies(idx_ref, n, src_of, dst_of, sem):
    def body(r, carry):
        i = idx_ref[0, r]
        pltpu.make_async_copy(src_of(r, i), dst_of(r, i), sem).start()
        return carry
    lax.fori_loop(0, n, body, 0)


def _moe_kernel(be_ref, tok_ref, tok_next_ref, dst_ref, dst_prev_ref, h_hbm, wg_ref, wu_ref, wd_ref, y_hbm,
                xbuf, xb_sc, acc_sc, ybuf, gsem, ssem, *, nf):
    i, j = pl.program_id(0), pl.program_id(1)
    nb = pl.num_programs(0)
    blk = acc_sc.shape[0]
    per_step = blk // nf
    nxt = (i + 1) % 2
    gather_wait = lambda s: pltpu.make_async_copy(h_hbm.at[pl.ds(0, blk), :], xbuf.at[s], gsem.at[s]).wait()
    scatter_wait = lambda: pltpu.make_async_copy(ybuf, y_hbm.at[pl.ds(0, blk), :], ssem).wait()
    gather_row = lambda idx_ref, r, s: pltpu.make_async_copy(
        h_hbm.at[pl.ds(idx_ref[0, r], 1), :], xbuf.at[s, pl.ds(r, 1), :], gsem.at[s]).start()
    scatter_row = lambda idx_ref, r: pltpu.make_async_copy(
        ybuf.at[pl.ds(r, 1), :], y_hbm.at[pl.ds(idx_ref[0, r], 1), :], ssem).start()

    @pl.when(j == 0)
    def _():
        @pl.when(i == 0)
        def _():
            ybuf[...] = jnp.zeros(ybuf.shape, F32)
            lax.fori_loop(0, blk, lambda r, c: (gather_row(tok_ref, r, 0), c)[1], 0)

        gather_wait(i % 2)
        xb_sc[...] = xbuf[i % 2].astype(BF16)
        acc_sc[...] = jnp.zeros(acc_sc.shape, F32)

    for r in range(per_step):
        gather_row(tok_next_ref, j * per_step + r, nxt)
        scatter_row(dst_prev_ref, j * per_step + r)

    x = xb_sc[...]
    act = (_silu(_dot(x, wg_ref[...])) * _dot(x, wu_ref[...])).astype(BF16)
    acc_sc[...] += _dot(act, wd_ref[...])

    @pl.when(j == nf - 1)
    def _():
        scatter_wait()
        ybuf[...] = acc_sc[...]

        @pl.when(i == nb - 1)
        def _():
            gather_wait(nxt)
            lax.fori_loop(0, blk, lambda r, c: (scatter_row(dst_ref, r), c)[1], 0)
            scatter_wait()


def _moe_experts(h, buf_tok, dest, block_e, w_gu, w_down):
    nb, blk = buf_tok.shape
    d = h.shape[1]
    f = w_down.shape[1]
    tf = _ffn_tile(f)
    nf = f // tf
    idx3 = lambda a: a.reshape(nb, 1, blk)
    smem = lambda imap: pl.BlockSpec((None, 1, blk), imap, memory_space=pltpu.SMEM)
    grid_spec = pltpu.PrefetchScalarGridSpec(
        num_scalar_prefetch=1,
        grid=(nb, nf),
        in_specs=[smem(lambda i, j, be: (i, 0, 0)),
                  smem(lambda i, j, be: (jnp.minimum(i + 1, nb - 1), 0, 0)),
                  smem(lambda i, j, be: (i, 0, 0)),
                  smem(lambda i, j, be: (jnp.maximum(i - 1, 0), 0, 0)),
                  pl.BlockSpec(memory_space=pl.ANY),
                  pl.BlockSpec((None, d, tf), lambda i, j, be: (be[i], 0, j)),
                  pl.BlockSpec((None, d, tf), lambda i, j, be: (be[i], 0, nf + j)),
                  pl.BlockSpec((None, tf, d), lambda i, j, be: (be[i], j, 0))],
        out_specs=pl.BlockSpec(memory_space=pl.ANY),
        scratch_shapes=[pltpu.VMEM((2, blk, d), F32), pltpu.VMEM((blk, d), BF16), pltpu.VMEM((blk, d), F32),
                        pltpu.VMEM((blk, d), F32), pltpu.SemaphoreType.DMA((2,)), pltpu.SemaphoreType.DMA(())],
    )
    return pl.pallas_call(
        functools.partial(_moe_kernel, nf=nf),
        grid_spec=grid_spec,
        out_shape=jax.ShapeDtypeStruct((nb * blk, d), F32),
        compiler_params=_cparams("arbitrary", "arbitrary"),
        name="moe_experts",
    )(block_e, idx3(buf_tok), idx3(buf_tok), idx3(dest), idx3(dest), h, w_gu, w_gu, w_down)


def _final_kernel(x_ref, mod_ref, rt_ref, fn_ref, y0_ref, y1_ref, out_ref):
    rt = rt_ref[...]
    y = rt[:, 2:3] * y0_ref[...] + rt[:, 3:4] * y1_ref[...]
    xo = x_ref[...] + mod_ref[5:6, :] * y
    ms = jnp.mean(xo * xo, axis=-1, keepdims=True)
    out_ref[...] = xo * lax.rsqrt(ms + NORM_EPS) * fn_ref[...]


def _moe_combine_final(x, mod, route, y, fn):
    b, t, d = x.shape
    tm = TOKEN_TILE
    nt = t // tm
    tok = lambda w: pl.BlockSpec((None, tm, w), lambda bi, i: (bi, i, 0))
    choice = lambda c: pl.BlockSpec((tm, d), lambda bi, i: (c * b * nt + bi * nt + i, 0))
    return pl.pallas_call(
        _final_kernel,
        grid=(b, nt),
        in_specs=[tok(d), pl.BlockSpec((None, 6, d), lambda bi, i: (bi, 0, 0)), tok(route.shape[-1]),
                  _full(fn), choice(0), choice(1)],
        out_specs=tok(d),
        out_shape=jax.ShapeDtypeStruct(x.shape, F32),
        compiler_params=_cparams("parallel", "parallel"),
        name="moe_combine_final",
    )(x, mod, route, fn, y, y)


def _moe_plan(top_e, n_tok):
    nk = n_tok * TOP_K
    flat_e = top_e.reshape(-1)
    onehot = (flat_e[:, None] == jnp.arange(N_EXPERTS, dtype=jnp.int32)[None, :]).astype(jnp.int32)
    rank = jnp.sum((jnp.cumsum(onehot, axis=0) - onehot) * onehot, axis=1)
    counts = jnp.sum(onehot, axis=0)
    padded = (counts + MOE_BLOCK - 1) // MOE_BLOCK * MOE_BLOCK
    pad_end = jnp.cumsum(padded)
    pad_start = pad_end - padded
    slot = (pad_start[flat_e] + rank).astype(jnp.int32)
    n_blocks = -(-nk // MOE_BLOCK) + N_EXPERTS
    rows = n_blocks * MOE_BLOCK
    assign = jnp.full((rows,), -1, jnp.int32).at[slot].set(jnp.arange(nk, dtype=jnp.int32))
    is_pad = assign < 0
    pad_rank = jnp.cumsum(is_pad.astype(jnp.int32)) - 1
    buf_tok = jnp.where(is_pad, 0, assign // TOP_K)
    dest = jnp.where(is_pad, nk + pad_rank, (assign % TOP_K) * n_tok + assign // TOP_K)
    block_e = jnp.minimum(
        jnp.searchsorted(pad_end, jnp.arange(n_blocks, dtype=jnp.int32) * MOE_BLOCK, side="right"),
        N_EXPERTS - 1).astype(jnp.int32)
    return buf_tok.reshape(n_blocks, MOE_BLOCK), dest.reshape(n_blocks, MOE_BLOCK), block_e


def _prep_even(w_in, w_uq, w_ukv, w_out):
    src, _, _, _ = _mla_layout()
    o = np.cumsum((0, MLA_Q_RANK, MLA_KV_RANK, MLA_ROPE, GDN_QKV, GDN_HEADS * GDN_DV, 2 * GDN_HEADS,
                   2 * GDN_HEADS))
    wcq, wckv = w_in[:, o[0]:o[1]], w_in[:, o[1]:o[2]]
    wkpe = _take_cols(w_in[:, o[2]:o[3]], np.where(src >= MLA_NOPE, src - MLA_NOPE, -1))
    wgqkv, wgz, wab = w_in[:, o[3]:o[4]], w_in[:, o[4]:o[5]], w_in[:, o[5]:o[7]]
    dq = MLA_NOPE + MLA_ROPE
    uq_idx = np.concatenate([np.where(src >= 0, h * dq + src, -1) for h in range(MLA_HEADS)])
    dkv = MLA_NOPE + MLA_V
    uk_idx = np.concatenate([np.where((src >= 0) & (src < MLA_NOPE), h * dkv + src, -1)
                             for h in range(MLA_HEADS)])
    uv_idx = np.concatenate([h * dkv + MLA_NOPE + np.arange(MLA_V) for h in range(MLA_HEADS)])
    wuq, wuk, wuv = _take_cols(w_uq, uq_idx), _take_cols(w_ukv, uk_idx), _take_cols(w_ukv, uv_idx)
    rows = -np.ones(MLA_HEADS * LANES, np.int64)
    for h in range(MLA_HEADS):
        base = h * LANES + (h % 2) * MLA_V
        rows[base:base + MLA_V] = h * MLA_V + np.arange(MLA_V)
    n_mla = MLA_HEADS * MLA_V
    wa = _take_cols(w_out[:n_mla].T, rows).T
    wg = w_out[n_mla:]
    ws = [w.astype(BF16) for w in (wcq, wckv, wkpe, wgqkv, wgz, wab)]
    return ws, [wuq.astype(BF16), wuk.astype(BF16), wuv.T.astype(BF16)], wa.astype(BF16), wg.astype(BF16)


def _prep_odd(w_in, q_norm, k_norm, w_out):
    m, src, _, _, _ = _pair_layout()
    nd = DIFF_HEADS * 2 * DIFF_DH
    o_dq, o_dk, o_dv, o_gq = 0, nd, 2 * nd, 3 * nd
    o_gk = o_gq + GQA_HEADS * GQA_DH
    o_gv = o_gk + GQA_KV_HEADS * GQA_DH
    pair = np.concatenate([h * 2 * DIFF_DH + m * DIFF_DH + src for h in range(DIFF_HEADS)])
    wdq = _take_cols(w_in, o_dq + pair)
    wdk = _take_cols(w_in, o_dk + pair)
    wdv = w_in[:, o_dv:o_gq].T
    grp = GQA_HEADS // GQA_KV_HEADS
    gq_idx = np.concatenate([np.where(m == h // grp, o_gq + h * GQA_DH + src, -1) for h in range(GQA_HEADS)])
    wgq = _take_cols(w_in, gq_idx)
    wgk = _take_cols(w_in, o_gk + m * GQA_DH + src)
    wgv = w_in[:, o_gv:o_gv + GQA_KV_HEADS * GQA_DH].T
    gqg = jnp.tile(q_norm[src], GQA_HEADS).reshape(1, -1)
    gkg = k_norm[src].reshape(1, -1)
    rows = -np.ones(GQA_HEADS * LANES, np.int64)
    for h in range(GQA_HEADS):
        base = h * LANES + (h // grp) * GQA_DH
        rows[base:base + GQA_DH] = h * GQA_DH + np.arange(GQA_DH)
    wd = w_out[:nd]
    wg = _take_cols(w_out[nd:].T, rows).T
    ws = [w.astype(BF16) for w in (wdq, wdk, wdv, wgq, wgk, wgv)]
    return ws, [gqg.astype(F32), gkg.astype(F32)], wd.astype(BF16), wg.astype(BF16)


def _layer_mods(m_layer, batch):
    d = m_layer.shape[1] // 6
    lat = m_layer[:batch].reshape(batch, 6, d)
    ctx = jnp.broadcast_to(m_layer[batch].reshape(1, 6, d), (batch, 6, d))
    return lat, ctx


def kernel(x, c, ctx, c_ctx, mod_w, mod_b, norm_g, ev_w_in, ev_mla_q_norm, ev_mla_kv_norm, ev_mla_w_uq,
           ev_mla_w_ukv, ev_gdn_conv, ev_gdn_a_log, ev_gdn_dt_bias, ev_gdn_out_norm, ev_w_out, ev_ffn_w_gu,
           ev_ffn_w_down, od_w_in, od_diff_lambda, od_diff_norm, od_gqa_q_norm, od_gqa_k_norm, od_w_out,
           od_router_w, od_moe_w_gu, od_moe_w_down, final_norm):
    batch, t_lat, d = x.shape
    t_ctx = ctx.shape[1]
    depth = mod_w.shape[0]
    assert depth == 2 and batch < 16

    cs = jnp.zeros((16, d), F32).at[:batch].set(c).at[batch].set(c_ctx)
    mods = _modulation(cs, mod_w, mod_b)

    j = 0
    mod_lat, mod_ctx = _layer_mods(mods[0], batch)
    g1 = norm_g[0, 0].reshape(1, d)
    g2 = norm_g[0, 1].reshape(1, d)
    ws, ups, wa, wg = _prep_even(ev_w_in[j], ev_mla_w_uq[j], ev_mla_w_ukv[j], ev_w_out[j])
    ws = ws + [ev_mla_q_norm[j].reshape(1, -1), ev_mla_kv_norm[j].reshape(1, -1)] + ups
    _, kind, freq, sign = _mla_layout()
    qscale = (MLA_NOPE + MLA_ROPE) ** -0.5 * LOG2E
    nfq = MLA_ROPE // 4
    tabs_lat = (_rope_tables(t_lat, kind, freq, sign, nfq, qscale, True)
                + _rope_tables(t_lat, kind, freq, sign, nfq, 1.0, True))
    tabs_ctx = (_rope_tables(t_ctx, kind, freq, sign, nfq, qscale, False)
                + _rope_tables(t_ctx, kind, freq, sign, nfq, 1.0, False))
    t_all = t_ctx + t_lat
    ctx_tiles = t_ctx // TOKEN_TILE
    qc, k_all, v_all, gqkv_c, gz_c, ab_c = _even_proj(ctx, mod_ctx, g1, tabs_ctx, ws, t_all, 0)
    ql, k_all, v_all, gqkv_l, gz_l, ab_l = _even_proj(x, mod_lat, g1, tabs_lat, ws, t_all, ctx_tiles,
                                                      bufs=(k_all, v_all))

    same = lambda hg: hg
    mla_kw = dict(groups=MLA_HEADS, g=1, kmap=same, vmap=lambda hg: hg // 2, out_dtype=BF16)
    ao_l = _flash(ql, k_all, v_all, **mla_kw)
    ao_c = _flash(qc, k_all, v_all, n_keys=t_ctx, **mla_kw)

    nh2 = 2 * GDN_HEADS
    prm = jnp.zeros((2, 2 * nh2), F32)
    prm = prm.at[0, :nh2].set(ev_gdn_a_log[j].reshape(-1)).at[1, :nh2].set(ev_gdn_dt_bias[j].reshape(-1))
    eseg = _seg_ones(GDN_HEADS * GDN_DK, GDN_DK)
    prep = _gdn_prep(gqkv_c, ab_c, ev_gdn_conv[j], prm, eseg, t_all, 0)
    prep = _gdn_prep(gqkv_l, ab_l, ev_gdn_conv[j], prm, eseg, t_all, ctx_tiles, bufs=prep)
    go_f, go_b = _gdn_scan(prep, t_ctx)

    on = jnp.tile(ev_gdn_out_norm[j], GDN_HEADS).reshape(1, -1)
    x = _mix_even(x, mod_lat, ao_l, go_f, go_b, ctx_tiles, gz_l, on, eseg, wa, wg)
    ctx = _mix_even(ctx, mod_ctx, ao_c, go_f, go_b, 0, gz_c, on, eseg, wa, wg)
    w_gu, w_dn = ev_ffn_w_gu[j].astype(BF16), ev_ffn_w_down[j].astype(BF16)
    x = _ffn(x, mod_lat, g2, w_gu, w_dn)
    ctx = _ffn(ctx, mod_ctx, g2, w_gu, w_dn)

    lambda_init = 0.8 - 0.6 * math.exp(-0.3 * 1)
    mod_lat, mod_ctx = _layer_mods(mods[1], batch)
    g1 = norm_g[1, 0].reshape(1, d)
    g2 = norm_g[1, 1].reshape(1, d)
    ws, gains, wd, wgx = _prep_odd(od_w_in[j], od_gqa_q_norm[j], od_gqa_k_norm[j], od_w_out[j])
    _, _, kind, freq, sign = _pair_layout()
    nfq = DIFF_DH // 4
    tabs_lat = _rope_tables(t_lat, kind, freq, sign, nfq, 1.0, True)
    tabs_ctx = _rope_tables(t_ctx, kind, freq, sign, nfq, 1.0, False)
    qscale = DIFF_DH ** -0.5 * LOG2E
    _, dk, dvt, _, gk, gvt = _odd_proj(ctx, mod_ctx, g1, tabs_ctx, ws, gains, qscale, t_all, 0)
    dq, dk, dvt, gq, gk, gvt = _odd_proj(x, mod_lat, g1, tabs_lat, ws, gains, qscale, t_all, ctx_tiles,
                                         bufs=(dk, dvt, gk, gvt))
    do = _flash(dq, dk, dvt, groups=DIFF_HEADS, g=2, kmap=same, vmap=same, out_dtype=F32)
    zero = lambda hg: 0
    go = _flash(gq, gk, gvt, groups=1, g=GQA_HEADS, kmap=zero, vmap=zero, out_dtype=BF16)
    x = _mix_odd(x, mod_lat, do, go, od_diff_lambda[j], od_diff_norm[j].reshape(1, -1), wd, wgx, lambda_init)

    rw = jnp.zeros((d, LANES), F32).at[:, :N_EXPERTS].set(od_router_w[j])
    h2, route = _router(x, mod_lat, g2, rw)
    n_tok = batch * t_lat
    top_e = route[..., 0:TOP_K].astype(jnp.int32).reshape(n_tok, TOP_K)
    buf_tok, dest, block_e = _moe_plan(top_e, n_tok)
    y = _moe_experts(h2.reshape(n_tok, d), buf_tok, dest, block_e, od_moe_w_gu[j].astype(BF16),
                     od_moe_w_down[j].astype(BF16))
    return _moe_combine_final(x, mod_lat, route, y, final_norm.reshape(1, d))
```

```python
import functools
import math

import numpy as np
import jax
import jax.numpy as jnp
from jax import lax
from jax.experimental import pallas as pl
from jax.experimental.pallas import tpu as pltpu

F32 = jnp.float32
BF16 = jnp.bfloat16

GRID_W = 64
ROPE_BASE = 10000.0
NORM_EPS = 1e-6
MLA_HEADS, MLA_NOPE, MLA_ROPE, MLA_V = 8, 64, 32, 64
MLA_Q_RANK, MLA_KV_RANK = 384, 256
GDN_HEADS, GDN_DK, GDN_DV, GDN_CONV, GDN_CHUNK = 8, 64, 64, 5, 64
GDN_QKV = GDN_HEADS * (2 * GDN_DK + GDN_DV)
DIFF_HEADS, DIFF_DH = 4, 64
GQA_HEADS, GQA_KV_HEADS, GQA_DH = 8, 2, 64
N_EXPERTS, TOP_K, MOE_BLOCK = 8, 2, 512

LANES = 128
SUBLANES = 8
VMEM_LIMIT = 48 * 1024 * 1024
LOG2E = 1.4426950408889634

TOKEN_TILE = 256
ATTN_ROWS = 1024
ATTN_KEYS = 256
ATTN_UNROLL = 16


def _cparams(*sem):
    return pltpu.CompilerParams(dimension_semantics=sem, vmem_limit_bytes=VMEM_LIMIT)


def _dot(a, b):
    return jnp.dot(a, b, preferred_element_type=F32)


def _dot_t(a, b):
    return lax.dot_general(a, b, (((1,), (1,)), ((), ())), preferred_element_type=F32)


def _split3(x):
    h1 = x.astype(BF16)
    r1 = x - h1.astype(F32)
    h2 = r1.astype(BF16)
    h3 = (r1 - h2.astype(F32)).astype(BF16)
    return h1, h2, h3


def _silu(x):
    return x * jax.nn.sigmoid(x)


def _rms_mod(x, g, sc, sh):
    ms = jnp.mean(x * x, axis=-1, keepdims=True)
    return (x * lax.rsqrt(ms + NORM_EPS) * g) * (1.0 + sc) + sh


def _seg_sumsq(x, eseg):
    sq = x * x
    hi = sq.astype(BF16)
    lo = (sq - hi.astype(F32)).astype(BF16)
    return _dot(hi, eseg) + _dot(lo, eseg)


def _rope(x, c, s):
    return x * c + pltpu.roll(x, LANES // 2, 1) * s


def _mla_layout():
    src = -np.ones(LANES, np.int64)
    kind = np.zeros(LANES, np.int64)
    freq = np.zeros(LANES, np.int64)
    sign = np.zeros(LANES, np.float32)
    nf = MLA_ROPE // 4
    for f in range(nf):
        src[f] = MLA_NOPE + f; kind[f] = 1; freq[f] = f; sign[f] = -1.0
        src[nf + f] = MLA_NOPE + 2 * nf + f; kind[nf + f] = 2; freq[nf + f] = f; sign[nf + f] = -1.0
        src[64 + f] = MLA_NOPE + nf + f; kind[64 + f] = 1; freq[64 + f] = f; sign[64 + f] = 1.0
        src[64 + nf + f] = MLA_NOPE + 3 * nf + f; kind[64 + nf + f] = 2; freq[64 + nf + f] = f
        sign[64 + nf + f] = 1.0
    src[16:64] = np.arange(0, 48)
    src[80:96] = np.arange(48, 64)
    return src, kind, freq, sign


def _pair_layout():
    lane = np.arange(LANES)
    region = lane // 32
    m = region % 2
    is_b = region // 2
    within = lane % 32
    is_col = within // 16
    f = within % 16
    src = is_col * 32 + is_b * 16 + f
    kind = 1 + is_col
    sign = np.where(is_b == 0, -1.0, 1.0).astype(np.float32)
    return m, src, kind, f, sign


def _rope_tables(t_len, kind, freq, sign, n_freq, scale, with_pos):
    kind_j = jnp.asarray(kind)
    if not with_pos:
        c = jnp.full((t_len, LANES), scale, F32)
        return c, jnp.zeros((t_len, LANES), F32)
    n_rows = t_len // GRID_W
    inv = 1.0 / (ROPE_BASE ** (jnp.arange(n_freq, dtype=F32) / n_freq))
    rows = jnp.repeat(jnp.arange(n_rows, dtype=F32), GRID_W)
    cols = jnp.tile(jnp.arange(GRID_W, dtype=F32), n_rows)
    ang_r = (rows[:, None] * inv)[:, freq]
    ang_c = (cols[:, None] * inv)[:, freq]
    ang = jnp.where(kind_j[None, :] == 1, ang_r, ang_c)
    has = (kind_j > 0)[None, :]
    c = jnp.where(has, jnp.cos(ang), 1.0) * scale
    s = jnp.where(has, jnp.sin(ang) * jnp.asarray(sign)[None, :], 0.0) * scale
    return c.astype(F32), s.astype(F32)


def _take_cols(w, idx):
    idx = np.asarray(idx)
    wz = jnp.concatenate([w, jnp.zeros((w.shape[0], 1), w.dtype)], axis=1)
    return wz[:, np.where(idx < 0, w.shape[1], idx)]


def _seg_ones(width, seg):
    r = np.arange(width)
    return jnp.asarray((r[:, None] // seg) == (r[None, :] // seg), dtype=BF16)


def _mod_kernel(c_ref, w_ref, b_ref, o_ref):
    c = c_ref[...]
    s1, s2, s3 = _split3(_silu(c))
    w1, w2, _ = _split3(w_ref[...])
    acc = _dot(s1, w1) + _dot(s1, w2) + _dot(s2, w1) + _dot(s2, w2) + _dot(s3, w1)
    o_ref[...] = acc + b_ref[...]


def _modulation(cs, mod_w, mod_b):
    depth, d, n = mod_w.shape
    tn = 512
    return pl.pallas_call(
        _mod_kernel,
        grid=(depth, n // tn),
        in_specs=[
            pl.BlockSpec(cs.shape, lambda l, j: (0, 0)),
            pl.BlockSpec((None, d, tn), lambda l, j: (l, 0, j)),
            pl.BlockSpec((None, 1, tn), lambda l, j: (l, 0, j)),
        ],
        out_specs=pl.BlockSpec((None, cs.shape[0], tn), lambda l, j: (l, 0, j)),
        out_shape=jax.ShapeDtypeStruct((depth, cs.shape[0], n), F32),
        compiler_params=_cparams("parallel", "parallel"),
        name="modulation",
    )(cs, mod_w, mod_b.reshape(depth, 1, n))


def _even_proj_kernel(x_ref, mod_ref, g_ref, cq_t, sq_t, ck_t, sk_t,
                      wcq, wckv, wkpe, wgqkv, wgz, wab, qn, kvn, wuq, wuk, wuv, *rest):
    q_out, k_out, v_out, gqkv_out, gz_out, ab_out = rest[-6:]
    h = _rms_mod(x_ref[...], g_ref[...], mod_ref[1:2, :], mod_ref[0:1, :]).astype(BF16)
    cq = _dot(h, wcq[...])
    ckv = _dot(h, wckv[...])
    kpe = _dot(h, wkpe[...])
    nq = (cq * lax.rsqrt(jnp.mean(cq * cq, axis=-1, keepdims=True) + NORM_EPS) * qn[...]).astype(BF16)
    nkv = (ckv * lax.rsqrt(jnp.mean(ckv * ckv, axis=-1, keepdims=True) + NORM_EPS) * kvn[...]).astype(BF16)
    q = _dot(nq, wuq[...])
    kn = _dot(nkv, wuk[...])
    kper = _rope(kpe, ck_t[...], sk_t[...])
    cqv, sqv = cq_t[...], sq_t[...]
    for hd in range(MLA_HEADS):
        sl = slice(hd * LANES, (hd + 1) * LANES)
        q_out[:, sl] = _rope(q[:, sl], cqv, sqv).astype(BF16)
        k_out[:, sl] = (kn[:, sl] + kper).astype(BF16)
    _store_vt(v_out, _dot_t(wuv[...], nkv))
    gqkv_out[...] = _dot(h, wgqkv[...])
    gz_out[...] = _dot(h, wgz[...])
    ab_out[...] = _dot(h, wab[...])


def _full(a):
    nd = a.ndim
    return pl.BlockSpec(a.shape, lambda *_: (0,) * nd)


def _alias_shared(args, in_specs, shapes, shared, bufs):
    shared = list(shared)
    if bufs is None:
        bufs = [jnp.zeros(shapes[o].shape, shapes[o].dtype) for o in shared]
    aliases = {len(args) + n: o for n, o in enumerate(shared)}
    return args + list(bufs), in_specs + [pl.BlockSpec(memory_space=pl.ANY)] * len(shared), aliases


def _store_vt(vt_out, vt):
    for n in range(vt_out.shape[0]):
        vt_out[n] = vt[n * LANES:(n + 1) * LANES, :].astype(vt_out.dtype)


def _proj_outputs(b, t, tt, tile_off, outs):
    tm = TOKEN_TILE
    assert tm == ATTN_KEYS
    specs, shapes = [], []
    for w, dt, kind in outs:
        if kind == "own":
            specs.append(pl.BlockSpec((None, tm, w), lambda bi, i: (bi, i, 0)))
            shapes.append(jax.ShapeDtypeStruct((b, t, w), dt))
        elif kind == "keys":
            specs.append(pl.BlockSpec((None, tm, w), lambda bi, i: (bi, i + tile_off, 0)))
            shapes.append(jax.ShapeDtypeStruct((b, tt, w), dt))
        else:
            specs.append(pl.BlockSpec((None, kind, None, LANES, tm), lambda bi, i: (bi, 0, i + tile_off, 0, 0)))
            shapes.append(jax.ShapeDtypeStruct((b, kind, tt // tm, LANES, tm), dt))
    shared = [n for n, (_, _, kind) in enumerate(outs) if kind != "own"]
    return specs, shapes, shared


def _even_proj(x, mod, g, tabs, ws, tt, tile_off, bufs=None):
    b, t, d = x.shape
    tm = TOKEN_TILE
    tok = lambda w: pl.BlockSpec((None, tm, w), lambda bi, i: (bi, i, 0))
    tab = pl.BlockSpec((tm, LANES), lambda bi, i: (i, 0))
    outs = ((MLA_HEADS * LANES, BF16, "own"), (MLA_HEADS * LANES, BF16, "keys"),
            (0, BF16, MLA_HEADS * MLA_V // LANES), (GDN_QKV, F32, "own"), (GDN_HEADS * GDN_DV, F32, "own"),
            (4 * GDN_HEADS, F32, "own"))
    specs, shapes, shared = _proj_outputs(b, t, tt, tile_off, outs)
    args = [x, mod, g, *tabs, *ws]
    in_specs = [tok(d), pl.BlockSpec((None, 6, d), lambda bi, i: (bi, 0, 0)), _full(g),
                tab, tab, tab, tab] + [_full(w) for w in ws]
    args, in_specs, aliases = _alias_shared(args, in_specs, shapes, shared, bufs)
    return pl.pallas_call(
        _even_proj_kernel,
        grid=(b, t // tm),
        in_specs=in_specs,
        out_specs=specs,
        out_shape=shapes,
        input_output_aliases=aliases,
        compiler_params=_cparams("parallel", "parallel"),
        name="even_proj",
    )(*args)


def _flash_kernel(q_ref, k_ref, vt_ref, o_ref, acc_sc, s_a, s_b, *, g, tq, tk, nsub):
    q = jnp.concatenate([q_ref[:, i * LANES:(i + 1) * LANES] for i in range(g)], axis=0)
    rows = g * tq
    acc_sc[...] = jnp.zeros(acc_sc.shape, F32)
    ones = jnp.ones((2 * SUBLANES, tk), BF16)

    def scores(jj):
        return _dot_t(k_ref[pl.ds(pl.multiple_of(jj * tk, tk), tk), :], q)

    def update(st, jj, m):
        m_new = jnp.maximum(m, jnp.max(st, axis=0, keepdims=True))
        p = jnp.exp2(st - m_new).astype(BF16)
        lhs = jnp.concatenate([vt_ref[jj], ones], axis=0)
        acc_sc[...] = jnp.exp2(m - m_new) * acc_sc[...] + _dot(lhs, p)
        return m_new

    bufs = (s_a, s_b)
    s_a[...] = scores(0)

    def run(j0, count, m, prefetch_last):
        for u in range(count):
            if u + 1 < count or prefetch_last:
                bufs[(u + 1) % 2][...] = scores(jnp.minimum(j0 + u + 1, nsub - 1))
            m = update(bufs[u % 2][...], j0 + u, m)
        return m

    trips, tail = divmod(nsub, ATTN_UNROLL)
    m = lax.fori_loop(0, trips, lambda i, m: run(i * ATTN_UNROLL, ATTN_UNROLL, m, True),
                      jnp.full((1, rows), -jnp.inf, F32))
    run(trips * ATTN_UNROLL, tail, m, False)
    on = acc_sc[0:LANES, :] * (1.0 / acc_sc[LANES:LANES + 1, :])
    o = on.T
    for i in range(g):
        o_ref[:, i * LANES:(i + 1) * LANES] = o[i * tq:(i + 1) * tq, :].astype(o_ref.dtype)


def _flash(q, k, vt, *, groups, g, kmap, vmap, out_dtype, n_keys=None):
    b, tq_total, _ = q.shape
    tk_total = k.shape[1] if n_keys is None else n_keys
    tq = min(ATTN_ROWS // g, tq_total)
    tk = ATTN_KEYS
    nsub = tk_total // tk
    assert vt.shape[3:] == (LANES, tk) and vt.shape[2] >= nsub
    rows = g * tq
    kern = functools.partial(_flash_kernel, g=g, tq=tq, tk=tk, nsub=nsub)
    return pl.pallas_call(
        kern,
        grid=(b, groups, tq_total // tq),
        in_specs=[
            pl.BlockSpec((None, tq, g * LANES), lambda bi, hg, i: (bi, i, hg)),
            pl.BlockSpec((None, tk_total, LANES), lambda bi, hg, i: (bi, 0, kmap(hg))),
            pl.BlockSpec((None, None, nsub, LANES, tk), lambda bi, hg, i: (bi, vmap(hg), 0, 0, 0)),
        ],
        out_specs=pl.BlockSpec((None, tq, g * LANES), lambda bi, hg, i: (bi, i, hg)),
        out_shape=jax.ShapeDtypeStruct(q.shape, out_dtype),
        scratch_shapes=[pltpu.VMEM((LANES + 2 * SUBLANES, rows), F32), pltpu.VMEM((tk, rows), F32),
                        pltpu.VMEM((tk, rows), F32)],
        compiler_params=_cparams("parallel", "parallel", "parallel"),
        name="flash_attention",
    )(q, k, vt)


def _gdn_prep_kernel(x_ref, prev_ref, next_ref, ab_ref, cw_ref, prm_ref, eseg_ref,
                     lf_ref, lb_ref, la_ref, *rest):
    q_out, k_out, v_out, s1_out, s2_out, xe_sc = rest[-6:]
    i = pl.program_id(1)
    tm = x_ref.shape[0]
    halo = SUBLANES
    xe_sc[0:halo, :] = prev_ref[...] * (i > 0).astype(F32)
    xe_sc[halo:halo + tm, :] = x_ref[...]
    xe_sc[halo + tm:2 * halo + tm, :] = next_ref[...] * (i < pl.num_programs(1) - 1).astype(F32)
    acc = jnp.zeros(x_ref.shape, F32)
    for j in range(GDN_CONV):
        acc = acc + cw_ref[j:j + 1, :] * xe_sc[pl.ds(halo - GDN_CONV // 2 + j, tm), :]
    y = _silu(acc)
    hw = GDN_HEADS * GDN_DK
    eseg = eseg_ref[...]
    q = y[:, 0:hw]
    k = y[:, hw:2 * hw]
    qn = q * lax.rsqrt(_seg_sumsq(q, eseg) + NORM_EPS) * (GDN_DK ** -0.5)
    kn = k * lax.rsqrt(_seg_sumsq(k, eseg) + NORM_EPS)
    v = y[:, 2 * hw:]
    for hd in range(GDN_HEADS):
        q_out[hd] = qn[:, hd * GDN_DK:(hd + 1) * GDN_DK]
        k_out[hd] = kn[:, hd * GDN_DK:(hd + 1) * GDN_DK]
        v_out[hd] = v[:, hd * GDN_DV:(hd + 1) * GDN_DV]

    ab = ab_ref[...]
    nh2 = 2 * GDN_HEADS
    lane = lax.broadcasted_iota(jnp.int32, ab.shape, 1)
    z = ab + prm_ref[1:2, :]
    softplus = jnp.maximum(z, 0.0) + jnp.log1p(jnp.exp(-jnp.abs(z)))
    gate = jnp.where(lane < nh2, -jnp.exp(prm_ref[0:1, :]) * softplus, 0.0)
    g1, g2, g3 = _split3(gate)
    lf, lb, la = lf_ref[...], lb_ref[...], la_ref[...]
    cum_f = _dot(lf, g1) + _dot(lf, g2) + _dot(lf, g3)
    cum_b = _dot(lb, g1) + _dot(lb, g2) + _dot(lb, g3)
    tot = _dot(la, g1) + _dot(la, g2) + _dot(la, g3)
    cum = jnp.where(lane < GDN_HEADS, cum_f, cum_b)
    s1_out[...] = jnp.where(lane < nh2, cum, jax.nn.sigmoid(ab))
    s2_out[...] = tot


def _gdn_prep(gqkv, ab, conv_w, prm, eseg, tt, tile_off, bufs=None):
    b, t, w = gqkv.shape
    tm = TOKEN_TILE
    nt8 = t // SUBLANES
    per = tm // SUBLANES
    r = np.arange(tm)
    same = (r[:, None] // GDN_CHUNK) == (r[None, :] // GDN_CHUNK)
    lf = jnp.asarray(same & (r[None, :] <= r[:, None]), dtype=BF16)
    lb = jnp.asarray(same & (r[None, :] >= r[:, None]), dtype=BF16)
    la = jnp.asarray(same, dtype=BF16)
    halo = gqkv.reshape(b, nt8, SUBLANES, w)
    tok = lambda wd: pl.BlockSpec((None, tm, wd), lambda bi, i: (bi, i, 0))
    hw = GDN_HEADS * GDN_DK
    nab = ab.shape[-1]
    head = lambda wd: pl.BlockSpec((None, GDN_HEADS, tm, wd), lambda bi, i: (bi, 0, i + tile_off, 0))
    sca = pl.BlockSpec((None, tm, nab), lambda bi, i: (bi, i + tile_off, 0))
    in_specs = [
        tok(w),
        pl.BlockSpec((None, None, SUBLANES, w), lambda bi, i: (bi, jnp.maximum(i * per - 1, 0), 0, 0)),
        pl.BlockSpec((None, None, SUBLANES, w), lambda bi, i: (bi, jnp.minimum((i + 1) * per, nt8 - 1), 0, 0)),
        tok(nab), _full(conv_w), _full(prm), _full(eseg), _full(lf), _full(lb), _full(la),
    ]
    args = [gqkv, halo, halo, ab, conv_w, prm, eseg, lf, lb, la]
    shapes = ([jax.ShapeDtypeStruct((b, GDN_HEADS, tt, GDN_DK), F32)] * 2
              + [jax.ShapeDtypeStruct((b, GDN_HEADS, tt, GDN_DV), F32)]
              + [jax.ShapeDtypeStruct((b, tt, nab), F32)] * 2)
    args, in_specs, aliases = _alias_shared(args, in_specs, shapes, range(len(shapes)), bufs)
    return pl.pallas_call(
        _gdn_prep_kernel,
        grid=(b, t // tm),
        in_specs=in_specs,
        out_specs=[head(GDN_DK), head(GDN_DK), head(GDN_DV), sca, sca],
        out_shape=shapes,
        input_output_aliases=aliases,
        scratch_shapes=[pltpu.VMEM((tm + 2 * SUBLANES, w), F32)],
        compiler_params=_cparams("parallel", "parallel"),
        name="gdn_prep",
    )(*args)


def _bmm(a, b):
    return jnp.einsum("nij,njk->nik", a, b, preferred_element_type=F32)


def _bmm_t(a, b):
    return jnp.einsum("nik,njk->nij", a, b, preferred_element_type=F32)


def _gdn_intra_kernel(q_ref, k_ref, v_ref, gr_ref, a_out, b_out, qe_out, o0_out, *, hb, cb):
    c_len = GDN_CHUNK
    n = hb * cb
    sgn = 1 - 2 * pl.program_id(0)
    row = lax.broadcasted_iota(jnp.int32, (c_len, c_len), 0)
    col = lax.broadcasted_iota(jnp.int32, (c_len, c_len), 1)
    ahead = (row - col) * sgn
    eye_f = (row == col).astype(F32)
    eye_b = jnp.broadcast_to(eye_f.astype(BF16), (n, c_len, c_len))
    q = q_ref[...].reshape(n, c_len, GDN_DK)
    k = k_ref[...].reshape(n, c_len, GDN_DK)
    v = v_ref[...].reshape(n, c_len, GDN_DV)
    g_cols = gr_ref[:, :, 0:1, :].reshape(n, 1, c_len)
    beta_row = gr_ref[:, :, 1:2, :].reshape(n, 1, c_len)
    gl = gr_ref[:, :, 2:3, :].reshape(n, 1, c_len)

    ones_b = jnp.broadcast_to(jnp.ones((c_len, c_len), BF16), (n, c_len, c_len))

    def on_sublanes(r, pieces):
        parts = _split3(eye_f * r)[:pieces]
        return functools.reduce(lambda a, b: a + b, [_bmm_t(p, ones_b) for p in parts])

    gc = on_sublanes(g_cols, 2)
    beta = on_sublanes(beta_row, 1)
    eg, ek, cd = jnp.exp(gc), jnp.exp(gl - gc), jnp.exp(gl)
    decay = jnp.where(ahead >= 0, jnp.exp(jnp.minimum(gc - g_cols, 0.0)), 0.0)
    kb = k * beta
    kbf = k.astype(BF16)
    low = jnp.where(ahead > 0, _bmm_t(kb.astype(BF16), kbf) * decay, 0.0)
    qk = _bmm_t(q.astype(BF16), kbf) * decay
    mpow = jnp.where((row // 8) == (col // 8), -low, 0.0)
    tinv = eye_f + mpow
    for _ in range(2):
        mb = mpow.astype(BF16)
        mpow = _bmm(mb, mb)
        tinv = tinv + _bmm(tinv.astype(BF16), mpow.astype(BF16))
    for s in (8, 16, 32):
        off = ((row // (2 * s)) == (col // (2 * s))) & ((row // s) != (col // s))
        tb = tinv.astype(BF16)
        tinv = tinv - _bmm(_bmm(tb, jnp.where(off, low, 0.0).astype(BF16)).astype(BF16), tb)
    tb = tinv.astype(BF16)
    ub = _bmm(tb, (v * beta).astype(BF16)).astype(BF16)
    wb = _bmm(tb, (kb * eg).astype(BF16)).astype(BF16)
    kdt = _bmm_t(eye_b, (k * ek).astype(BF16)).astype(BF16)
    a_out[...] = (cd * eye_f - _bmm(kdt, wb)).reshape(a_out.shape)
    b_out[...] = _bmm(kdt, ub).reshape(b_out.shape)
    qkb = qk.astype(BF16)
    qe_out[...] = (q * eg - _bmm(qkb, wb)).reshape(qe_out.shape)
    o0_out[...] = _bmm(qkb, ub).reshape(o0_out.shape)


def _gdn_intra(q, k, v, gr):
    b, h, tt, dk = q.shape
    assert dk == GDN_CHUNK and v.shape[-1] == GDN_CHUNK
    nc = tt // GDN_CHUNK
    hb, cb = h, 4
    rows = cb * GDN_CHUNK
    qkv = pl.BlockSpec((None, hb, rows, dk), lambda d, bi, c: (bi, 0, c, 0))
    tokb = pl.BlockSpec((None, None, hb, rows, dk), lambda d, bi, c: (d, bi, 0, c, 0))
    matb = pl.BlockSpec((None, None, hb, cb, dk, dk), lambda d, bi, c: (d, bi, 0, c, 0, 0))
    return pl.pallas_call(
        functools.partial(_gdn_intra_kernel, hb=hb, cb=cb),
        grid=(2, b, nc // cb),
        in_specs=[qkv, qkv, qkv,
                  pl.BlockSpec((None, None, hb, cb, SUBLANES, GDN_CHUNK), lambda d, bi, c: (d, bi, 0, c, 0, 0))],
        out_specs=[matb, matb, tokb, tokb],
        out_shape=[jax.ShapeDtypeStruct((2, b, h, nc, dk, dk), F32)] * 2
                  + [jax.ShapeDtypeStruct((2, b, h, tt, dk), F32)] * 2,
        compiler_params=_cparams("parallel", "parallel", "parallel"),
        name="gdn_intra",
    )(q, k, v, gr)


def _gdn_inter_kernel(af, bf, qf, of, ab, bb, qb, ob, o_f, o_b, s_sc):
    @pl.when(pl.program_id(1) == 0)
    def _():
        s_sc[...] = jnp.zeros(s_sc.shape, F32)

    kc = af.shape[1]
    c_len = GDN_CHUNK
    s_f, s_b = s_sc[0], s_sc[1]
    for i in range(kc):
        r = kc - 1 - i
        sf, sb = s_f.astype(BF16), s_b.astype(BF16)
        o_f[:, i * c_len:(i + 1) * c_len, :] = (
            _bmm(qf[:, i * c_len:(i + 1) * c_len, :].astype(BF16), sf) + of[:, i * c_len:(i + 1) * c_len, :])
        o_b[:, r * c_len:(r + 1) * c_len, :] = (
            _bmm(qb[:, r * c_len:(r + 1) * c_len, :].astype(BF16), sb) + ob[:, r * c_len:(r + 1) * c_len, :])
        s_f = _bmm(af[:, i].astype(BF16), sf) + bf[:, i]
        s_b = _bmm(ab[:, r].astype(BF16), sb) + bb[:, r]
    s_sc[0] = s_f
    s_sc[1] = s_b


def _gdn_inter(a, bm, qe, o0, nc_ctx):
    _, b, h, nc, dk, _ = a.shape
    kc = math.gcd(nc_ctx, nc - nc_ctx, 4)
    nb, nb_ctx = nc // kc, nc_ctx // kc
    rev = lambda c: jnp.where(c < nb_ctx, nb_ctx - 1 - c, nb - 1 - (c - nb_ctx))
    mat_f = pl.BlockSpec((None, None, h, kc, dk, dk), lambda bi, c: (0, bi, 0, c, 0, 0))
    mat_b = pl.BlockSpec((None, None, h, kc, dk, dk), lambda bi, c: (1, bi, 0, rev(c), 0, 0))
    tok_f = pl.BlockSpec((None, None, h, kc * GDN_CHUNK, dk), lambda bi, c: (0, bi, 0, c, 0))
    tok_b = pl.BlockSpec((None, None, h, kc * GDN_CHUNK, dk), lambda bi, c: (1, bi, 0, rev(c), 0))
    out_f = pl.BlockSpec((None, h, kc * GDN_CHUNK, dk), lambda bi, c: (bi, 0, c, 0))
    out_b = pl.BlockSpec((None, h, kc * GDN_CHUNK, dk), lambda bi, c: (bi, 0, rev(c), 0))
    osd = jax.ShapeDtypeStruct(qe.shape[1:], F32)
    return pl.pallas_call(
        _gdn_inter_kernel,
        grid=(b, nb),
        in_specs=[mat_f, mat_f, tok_f, tok_f, mat_b, mat_b, tok_b, tok_b],
        out_specs=[out_f, out_b],
        out_shape=[osd, osd],
        scratch_shapes=[pltpu.VMEM((2, h, dk, dk), F32)],
        compiler_params=_cparams("parallel", "arbitrary"),
        name="gdn_inter",
    )(a, bm, qe, o0, a, bm, qe, o0)


def _gdn_scan(prep, tc):
    q, k, v, s1, s2 = prep
    b, nh, tt, _ = q.shape
    s1t, s2t = s1.transpose(0, 2, 1), s2.transpose(0, 2, 1)
    nc = tt // GDN_CHUNK
    zero = jnp.zeros((b, nh, nc, GDN_CHUNK), F32)
    gr = jnp.stack([
        jnp.stack([x.reshape(b, nh, nc, GDN_CHUNK) for x in
                   (s1t[:, d * nh:(d + 1) * nh], s1t[:, (2 + d) * nh:(3 + d) * nh], s2t[:, d * nh:(d + 1) * nh])]
                  + [zero] * (SUBLANES - 3), axis=3)
        for d in range(2)])
    a, bm, qe, o0 = _gdn_intra(q, k, v, gr)
    return _gdn_inter(a, bm, qe, o0, tc // GDN_CHUNK)


def _mix_even_kernel(x_ref, mod_ref, ao_ref, gf_ref, gb_ref, gz_ref, on_ref, eseg_ref, wa_ref, wg_ref, out_ref):
    o = jnp.concatenate([gf_ref[hd] + gb_ref[hd] for hd in range(GDN_HEADS)], axis=-1)
    ms = _seg_sumsq(o, eseg_ref[...]) * (1.0 / GDN_DV)
    y = o * lax.rsqrt(ms + NORM_EPS) * on_ref[...] * _silu(gz_ref[...])
    mix = _dot(ao_ref[...], wa_ref[...]) + _dot(y.astype(BF16), wg_ref[...])
    out_ref[...] = x_ref[...] + mod_ref[2:3, :] * mix


def _mix_even(x, mod, ao, go_f, go_b, tile_off, gz, on, eseg, wa, wg):
    b, t, d = x.shape
    tm = TOKEN_TILE
    tok = lambda w: pl.BlockSpec((None, tm, w), lambda bi, i: (bi, i, 0))
    head = pl.BlockSpec((None, GDN_HEADS, tm, GDN_DV), lambda bi, i: (bi, 0, i + tile_off, 0))
    return pl.pallas_call(
        _mix_even_kernel,
        grid=(b, t // tm),
        in_specs=[tok(d), pl.BlockSpec((None, 6, d), lambda bi, i: (bi, 0, 0)), tok(ao.shape[-1]),
                  head, head, tok(gz.shape[-1]), _full(on), _full(eseg), _full(wa), _full(wg)],
        out_specs=tok(d),
        out_shape=jax.ShapeDtypeStruct(x.shape, F32),
        compiler_params=_cparams("parallel", "parallel"),
        name="mix_even",
    )(x, mod, ao, go_f, go_b, gz, on, eseg, wa, wg)


def _ffn_kernel(x_ref, mod_ref, g_ref, wg_ref, wu_ref, wd_ref, out_ref, h_sc, acc_sc):
    j = pl.program_id(2)

    @pl.when(j == 0)
    def _():
        h_sc[...] = _rms_mod(x_ref[...], g_ref[...], mod_ref[4:5, :], mod_ref[3:4, :]).astype(BF16)
        acc_sc[...] = jnp.zeros(acc_sc.shape, F32)

    h = h_sc[...]
    act = (_silu(_dot(h, wg_ref[...])) * _dot(h, wu_ref[...])).astype(BF16)
    acc_sc[...] += _dot(act, wd_ref[...])

    @pl.when(j == pl.num_programs(2) - 1)
    def _():
        out_ref[...] = x_ref[...] + mod_ref[5:6, :] * acc_sc[...]


def _ffn_tile(f):
    for cand in (1408, 1792, 1024, 896, 768, 512, 256, 128):
        if f % cand == 0:
            return cand
    raise ValueError(f"ffn width {f} is not a multiple of 128")


def _ffn(x, mod, g, w_gu, w_down):
    b, t, d = x.shape
    f = w_down.shape[0]
    tm = 512 if t % 512 == 0 else TOKEN_TILE
    tf = _ffn_tile(f)
    nf = f // tf
    tok = pl.BlockSpec((None, tm, d), lambda bi, i, j: (bi, i, 0))
    return pl.pallas_call(
        _ffn_kernel,
        grid=(b, t // tm, nf),
        in_specs=[tok, pl.BlockSpec((None, 6, d), lambda bi, i, j: (bi, 0, 0)),
                  pl.BlockSpec(g.shape, lambda bi, i, j: (0, 0)),
                  pl.BlockSpec((d, tf), lambda bi, i, j: (0, j)),
                  pl.BlockSpec((d, tf), lambda bi, i, j: (0, nf + j)),
                  pl.BlockSpec((tf, d), lambda bi, i, j: (j, 0))],
        out_specs=tok,
        out_shape=jax.ShapeDtypeStruct(x.shape, F32),
        scratch_shapes=[pltpu.VMEM((tm, d), BF16), pltpu.VMEM((tm, d), F32)],
        compiler_params=_cparams("parallel", "parallel", "arbitrary"),
        name="ffn",
    )(x, mod, g, w_gu, w_gu, w_down)


def _odd_proj_kernel(x_ref, mod_ref, g_ref, c_t, s_t, wdq, wdk, wdv, wgq, wgk, wgv, gqg, gkg, *rest, qscale):
    dq_out, dk_out, dv_out, gq_out, gk_out, gv_out = rest[-6:]
    h = _rms_mod(x_ref[...], g_ref[...], mod_ref[1:2, :], mod_ref[0:1, :]).astype(BF16)
    c, s = c_t[...], s_t[...]
    lane = lax.broadcasted_iota(jnp.int32, c.shape, 1)
    is_m1 = ((lane // 32) % 2) == 1
    dq = _dot(h, wdq[...])
    dk = _dot(h, wdk[...])
    for hd in range(DIFF_HEADS):
        sl = slice(hd * LANES, (hd + 1) * LANES)
        r = _rope(dq[:, sl], c, s) * qscale
        dq_out[:, 2 * hd * LANES:(2 * hd + 1) * LANES] = jnp.where(is_m1, 0.0, r).astype(BF16)
        dq_out[:, (2 * hd + 1) * LANES:(2 * hd + 2) * LANES] = jnp.where(is_m1, r, 0.0).astype(BF16)
        dk_out[:, sl] = _rope(dk[:, sl], c, s).astype(BF16)
    _store_vt(dv_out, _dot_t(wdv[...], h))
    gq = _dot(h, wgq[...])
    gqgv = gqg[...]
    inv_dh = 1.0 / GQA_DH
    for hd in range(GQA_HEADS):
        sl = slice(hd * LANES, (hd + 1) * LANES)
        xh = gq[:, sl]
        ms = jnp.sum(xh * xh, axis=-1, keepdims=True) * inv_dh
        xn = xh * lax.rsqrt(ms + NORM_EPS) * gqgv[:, sl]
        gq_out[:, sl] = (_rope(xn, c, s) * qscale).astype(BF16)
    gk = _dot(h, wgk[...])
    sq = gk * gk
    ms0 = jnp.sum(jnp.where(is_m1, 0.0, sq), axis=-1, keepdims=True) * inv_dh
    ms1 = jnp.sum(jnp.where(is_m1, sq, 0.0), axis=-1, keepdims=True) * inv_dh
    rs = jnp.where(is_m1, lax.rsqrt(ms1 + NORM_EPS), lax.rsqrt(ms0 + NORM_EPS))
    gk_out[...] = _rope(gk * rs * gkg[...], c, s).astype(BF16)
    _store_vt(gv_out, _dot_t(wgv[...], h))


def _odd_proj(x, mod, g, tabs, ws, gains, qscale, tt, tile_off, bufs=None):
    b, t, d = x.shape
    tm = TOKEN_TILE
    tok = lambda w: pl.BlockSpec((None, tm, w), lambda bi, i: (bi, i, 0))
    tab = pl.BlockSpec((tm, LANES), lambda bi, i: (i, 0))
    outs = ((2 * DIFF_HEADS * LANES, BF16, "own"), (DIFF_HEADS * LANES, BF16, "keys"), (0, BF16, DIFF_HEADS),
            (GQA_HEADS * LANES, BF16, "own"), (LANES, BF16, "keys"), (0, BF16, 1))
    specs, shapes, shared = _proj_outputs(b, t, tt, tile_off, outs)
    args = [x, mod, g, *tabs, *ws, *gains]
    in_specs = ([tok(d), pl.BlockSpec((None, 6, d), lambda bi, i: (bi, 0, 0)), _full(g), tab, tab]
                + [_full(w) for w in ws] + [_full(w) for w in gains])
    args, in_specs, aliases = _alias_shared(args, in_specs, shapes, shared, bufs)
    return pl.pallas_call(
        functools.partial(_odd_proj_kernel, qscale=qscale),
        grid=(b, t // tm),
        in_specs=in_specs,
        out_specs=specs,
        out_shape=shapes,
        input_output_aliases=aliases,
        compiler_params=_cparams("parallel", "parallel"),
        name="odd_proj",
    )(*args)


def _mix_odd_kernel(x_ref, mod_ref, do_ref, go_ref, lam_ref, dn_ref, wd_ref, wg_ref, out_ref, *, lambda_init):
    lp = lam_ref[...]
    lam = (jnp.exp(jnp.sum(lp[0:1, :] * lp[1:2, :], axis=-1, keepdims=True))
           - jnp.exp(jnp.sum(lp[2:3, :] * lp[3:4, :], axis=-1, keepdims=True)) + lambda_init)
    dn = dn_ref[...]
    parts = []
    for hd in range(DIFF_HEADS):
        d0 = do_ref[:, 2 * hd * LANES:(2 * hd + 1) * LANES]
        d1 = do_ref[:, (2 * hd + 1) * LANES:(2 * hd + 2) * LANES]
        dd = d0 - lam * d1
        ms = jnp.mean(dd * dd, axis=-1, keepdims=True)
        parts.append(((dd * lax.rsqrt(ms + NORM_EPS) * dn) * (1.0 - lambda_init)).astype(BF16))
    dcat = jnp.concatenate(parts, axis=1)
    mix = _dot(dcat, wd_ref[...]) + _dot(go_ref[...], wg_ref[...])
    out_ref[...] = x_ref[...] + mod_ref[2:3, :] * mix


def _mix_odd(x, mod, do, go, lam_p, dn, wd, wg, lambda_init):
    b, t, d = x.shape
    tm = TOKEN_TILE
    tok = lambda w: pl.BlockSpec((None, tm, w), lambda bi, i: (bi, i, 0))
    return pl.pallas_call(
        functools.partial(_mix_odd_kernel, lambda_init=lambda_init),
        grid=(b, t // tm),
        in_specs=[tok(d), pl.BlockSpec((None, 6, d), lambda bi, i: (bi, 0, 0)), tok(do.shape[-1]),
                  tok(go.shape[-1]), _full(lam_p), _full(dn), _full(wd), _full(wg)],
        out_specs=tok(d),
        out_shape=jax.ShapeDtypeStruct(x.shape, F32),
        compiler_params=_cparams("parallel", "parallel"),
        name="mix_odd",
    )(x, mod, do, go, lam_p, dn, wd, wg)


def _router_kernel(x_ref, mod_ref, g_ref, rw_ref, h_out, route_out):
    h = _rms_mod(x_ref[...], g_ref[...], mod_ref[4:5, :], mod_ref[3:4, :])
    h_out[...] = h
    h1, h2, h3 = _split3(h)
    w1, w2, w3 = _split3(rw_ref[...])
    logits = (_dot(h1, w1) + _dot(h1, w2) + _dot(h2, w1) + _dot(h2, w2) + _dot(h1, w3) + _dot(h3, w1))
    lane = lax.broadcasted_iota(jnp.int32, logits.shape, 1).astype(F32)
    neg = -jnp.inf
    l1 = jnp.where(lane < N_EXPERTS, logits, neg)
    m1 = jnp.max(l1, axis=-1, keepdims=True)
    i1 = jnp.min(jnp.where(l1 == m1, lane, float(LANES)), axis=-1, keepdims=True)
    l2 = jnp.where(lane == i1, neg, l1)
    m2 = jnp.max(l2, axis=-1, keepdims=True)
    i2 = jnp.min(jnp.where(l2 == m2, lane, float(LANES)), axis=-1, keepdims=True)
    e = jnp.exp(m2 - m1)
    p1 = 1.0 / (1.0 + e)
    p2 = e / (1.0 + e)
    route = jnp.where(lane == 0.0, i1,
                      jnp.where(lane == 1.0, i2, jnp.where(lane == 2.0, p1, jnp.where(lane == 3.0, p2, 0.0))))
    route_out[...] = route[:, 0:SUBLANES]


def _router(x, mod, g, rw):
    b, t, d = x.shape
    tm = TOKEN_TILE
    tok = lambda w: pl.BlockSpec((None, tm, w), lambda bi, i: (bi, i, 0))
    return pl.pallas_call(
        _router_kernel,
        grid=(b, t // tm),
        in_specs=[tok(d), pl.BlockSpec((None, 6, d), lambda bi, i: (bi, 0, 0)), _full(g), _full(rw)],
        out_specs=[tok(d), tok(SUBLANES)],
        out_shape=[jax.ShapeDtypeStruct(x.shape, F32), jax.ShapeDtypeStruct((b, t, SUBLANES), F32)],
        compiler_params=_cparams("parallel", "parallel"),
        name="moe_router",
    )(x, mod, g, rw)


def _row_copies(idx_ref, n, src_of, dst_of, sem):
    def body(r, carry):
        i = idx_ref[0, r]
        pltpu.make_async_copy(src_of(r, i), dst_of(r, i), sem).start()
        return carry
    lax.fori_loop(0, n, body, 0)


def _moe_kernel(be_ref, tok_ref, tok_next_ref, dst_ref, dst_prev_ref, h_hbm, wg_ref, wu_ref, wd_ref, y_hbm,
                xbuf, xb_sc, acc_sc, ybuf, gsem, ssem, *, nf):
    i, j = pl.program_id(0), pl.program_id(1)
    nb = pl.num_programs(0)
    blk = acc_sc.shape[0]
    per_step = blk // nf
    nxt = (i + 1) % 2
    gather_wait = lambda s: pltpu.make_async_copy(h_hbm.at[pl.ds(0, blk), :], xbuf.at[s], gsem.at[s]).wait()
    scatter_wait = lambda: pltpu.make_async_copy(ybuf, y_hbm.at[pl.ds(0, blk), :], ssem).wait()
    gather_row = lambda idx_ref, r, s: pltpu.make_async_copy(
        h_hbm.at[pl.ds(idx_ref[0, r], 1), :], xbuf.at[s, pl.ds(r, 1), :], gsem.at[s]).start()
    scatter_row = lambda idx_ref, r: pltpu.make_async_copy(
        ybuf.at[pl.ds(r, 1), :], y_hbm.at[pl.ds(idx_ref[0, r], 1), :], ssem).start()

    @pl.when(j == 0)
    def _():
        @pl.when(i == 0)
        def _():
            ybuf[...] = jnp.zeros(ybuf.shape, F32)
            lax.fori_loop(0, blk, lambda r, c: (gather_row(tok_ref, r, 0), c)[1], 0)

        gather_wait(i % 2)
        xb_sc[...] = xbuf[i % 2].astype(BF16)
        acc_sc[...] = jnp.zeros(acc_sc.shape, F32)

    for r in range(per_step):
        gather_row(tok_next_ref, j * per_step + r, nxt)
        scatter_row(dst_prev_ref, j * per_step + r)

    x = xb_sc[...]
    act = (_silu(_dot(x, wg_ref[...])) * _dot(x, wu_ref[...])).astype(BF16)
    acc_sc[...] += _dot(act, wd_ref[...])

    @pl.when(j == nf - 1)
    def _():
        scatter_wait()
        ybuf[...] = acc_sc[...]

        @pl.when(i == nb - 1)
        def _():
            gather_wait(nxt)
            lax.fori_loop(0, blk, lambda r, c: (scatter_row(dst_ref, r), c)[1], 0)
            scatter_wait()


def _moe_experts(h, buf_tok, dest, block_e, w_gu, w_down):
    nb, blk = buf_tok.shape
    d = h.shape[1]
    f = w_down.shape[1]
    tf = _ffn_tile(f)
    nf = f // tf
    idx3 = lambda a: a.reshape(nb, 1, blk)
    smem = lambda imap: pl.BlockSpec((None, 1, blk), imap, memory_space=pltpu.SMEM)
    grid_spec = pltpu.PrefetchScalarGridSpec(
        num_scalar_prefetch=1,
        grid=(nb, nf),
        in_specs=[smem(lambda i, j, be: (i, 0, 0)),
                  smem(lambda i, j, be: (jnp.minimum(i + 1, nb - 1), 0, 0)),
                  smem(lambda i, j, be: (i, 0, 0)),
                  smem(lambda i, j, be: (jnp.maximum(i - 1, 0), 0, 0)),
                  pl.BlockSpec(memory_space=pl.ANY),
                  pl.BlockSpec((None, d, tf), lambda i, j, be: (be[i], 0, j)),
                  pl.BlockSpec((None, d, tf), lambda i, j, be: (be[i], 0, nf + j)),
                  pl.BlockSpec((None, tf, d), lambda i, j, be: (be[i], j, 0))],
        out_specs=pl.BlockSpec(memory_space=pl.ANY),
        scratch_shapes=[pltpu.VMEM((2, blk, d), F32), pltpu.VMEM((blk, d), BF16), pltpu.VMEM((blk, d), F32),
                        pltpu.VMEM((blk, d), F32), pltpu.SemaphoreType.DMA((2,)), pltpu.SemaphoreType.DMA(())],
    )
    return pl.pallas_call(
        functools.partial(_moe_kernel, nf=nf),
        grid_spec=grid_spec,
        out_shape=jax.ShapeDtypeStruct((nb * blk, d), F32),
        compiler_params=_cparams("arbitrary", "arbitrary"),
        name="moe_experts",
    )(block_e, idx3(buf_tok), idx3(buf_tok), idx3(dest), idx3(dest), h, w_gu, w_gu, w_down)


def _final_kernel(x_ref, mod_ref, rt_ref, fn_ref, y0_ref, y1_ref, out_ref):
    rt = rt_ref[...]
    y = rt[:, 2:3] * y0_ref[...] + rt[:, 3:4] * y1_ref[...]
    xo = x_ref[...] + mod_ref[5:6, :] * y
    ms = jnp.mean(xo * xo, axis=-1, keepdims=True)
    out_ref[...] = xo * lax.rsqrt(ms + NORM_EPS) * fn_ref[...]


def _moe_combine_final(x, mod, route, y, fn):
    b, t, d = x.shape
    tm = TOKEN_TILE
    nt = t // tm
    tok = lambda w: pl.BlockSpec((None, tm, w), lambda bi, i: (bi, i, 0))
    choice = lambda c: pl.BlockSpec((tm, d), lambda bi, i: (c * b * nt + bi * nt + i, 0))
    return pl.pallas_call(
        _final_kernel,
        grid=(b, nt),
        in_specs=[tok(d), pl.BlockSpec((None, 6, d), lambda bi, i: (bi, 0, 0)), tok(route.shape[-1]),
                  _full(fn), choice(0), choice(1)],
        out_specs=tok(d),
        out_shape=jax.ShapeDtypeStruct(x.shape, F32),
        compiler_params=_cparams("parallel", "parallel"),
        name="moe_combine_final",
    )(x, mod, route, fn, y, y)


def _moe_plan(top_e, n_tok):
    nk = n_tok * TOP_K
    flat_e = top_e.reshape(-1)
    onehot = (flat_e[:, None] == jnp.arange(N_EXPERTS, dtype=jnp.int32)[None, :]).astype(jnp.int32)
    rank = jnp.sum((jnp.cumsum(onehot, axis=0) - onehot) * onehot, axis=1)
    counts = jnp.sum(onehot, axis=0)
    padded = (counts + MOE_BLOCK - 1) // MOE_BLOCK * MOE_BLOCK
    pad_end = jnp.cumsum(padded)
    pad_start = pad_end - padded
    slot = (pad_start[flat_e] + rank).astype(jnp.int32)
    n_blocks = -(-nk // MOE_BLOCK) + N_EXPERTS
    rows = n_blocks * MOE_BLOCK
    assign = jnp.full((rows,), -1, jnp.int32).at[slot].set(jnp.arange(nk, dtype=jnp.int32))
    is_pad = assign < 0
    pad_rank = jnp.cumsum(is_pad.astype(jnp.int32)) - 1
    buf_tok = jnp.where(is_pad, 0, assign // TOP_K)
    dest = jnp.where(is_pad, nk + pad_rank, (assign % TOP_K) * n_tok + assign // TOP_K)
    block_e = jnp.minimum(
        jnp.searchsorted(pad_end, jnp.arange(n_blocks, dtype=jnp.int32) * MOE_BLOCK, side="right"),
        N_EXPERTS - 1).astype(jnp.int32)
    return buf_tok.reshape(n_blocks, MOE_BLOCK), dest.reshape(n_blocks, MOE_BLOCK), block_e


def _prep_even(w_in, w_uq, w_ukv, w_out):
    src, _, _, _ = _mla_layout()
    o = np.cumsum((0, MLA_Q_RANK, MLA_KV_RANK, MLA_ROPE, GDN_QKV, GDN_HEADS * GDN_DV, 2 * GDN_HEADS,
                   2 * GDN_HEADS))
    wcq, wckv = w_in[:, o[0]:o[1]], w_in[:, o[1]:o[2]]
    wkpe = _take_cols(w_in[:, o[2]:o[3]], np.where(src >= MLA_NOPE, src - MLA_NOPE, -1))
    wgqkv, wgz, wab = w_in[:, o[3]:o[4]], w_in[:, o[4]:o[5]], w_in[:, o[5]:o[7]]
    dq = MLA_NOPE + MLA_ROPE
    uq_idx = np.concatenate([np.where(src >= 0, h * dq + src, -1) for h in range(MLA_HEADS)])
    dkv = MLA_NOPE + MLA_V
    uk_idx = np.concatenate([np.where((src >= 0) & (src < MLA_NOPE), h * dkv + src, -1)
                             for h in range(MLA_HEADS)])
    uv_idx = np.concatenate([h * dkv + MLA_NOPE + np.arange(MLA_V) for h in range(MLA_HEADS)])
    wuq, wuk, wuv = _take_cols(w_uq, uq_idx), _take_cols(w_ukv, uk_idx), _take_cols(w_ukv, uv_idx)
    rows = -np.ones(MLA_HEADS * LANES, np.int64)
    for h in range(MLA_HEADS):
        base = h * LANES + (h % 2) * MLA_V
        rows[base:base + MLA_V] = h * MLA_V + np.arange(MLA_V)
    n_mla = MLA_HEADS * MLA_V
    wa = _take_cols(w_out[:n_mla].T, rows).T
    wg = w_out[n_mla:]
    ws = [w.astype(BF16) for w in (wcq, wckv, wkpe, wgqkv, wgz, wab)]
    return ws, [wuq.astype(BF16), wuk.astype(BF16), wuv.T.astype(BF16)], wa.astype(BF16), wg.astype(BF16)


def _prep_odd(w_in, q_norm, k_norm, w_out):
    m, src, _, _, _ = _pair_layout()
    nd = DIFF_HEADS * 2 * DIFF_DH
    o_dq, o_dk, o_dv, o_gq = 0, nd, 2 * nd, 3 * nd
    o_gk = o_gq + GQA_HEADS * GQA_DH
    o_gv = o_gk + GQA_KV_HEADS * GQA_DH
    pair = np.concatenate([h * 2 * DIFF_DH + m * DIFF_DH + src for h in range(DIFF_HEADS)])
    wdq = _take_cols(w_in, o_dq + pair)
    wdk = _take_cols(w_in, o_dk + pair)
    wdv = w_in[:, o_dv:o_gq].T
    grp = GQA_HEADS // GQA_KV_HEADS
    gq_idx = np.concatenate([np.where(m == h // grp, o_gq + h * GQA_DH + src, -1) for h in range(GQA_HEADS)])
    wgq = _take_cols(w_in, gq_idx)
    wgk = _take_cols(w_in, o_gk + m * GQA_DH + src)
    wgv = w_in[:, o_gv:o_gv + GQA_KV_HEADS * GQA_DH].T
    gqg = jnp.tile(q_norm[src], GQA_HEADS).reshape(1, -1)
    gkg = k_norm[src].reshape(1, -1)
    rows = -np.ones(GQA_HEADS * LANES, np.int64)
    for h in range(GQA_HEADS):
        base = h * LANES + (h // grp) * GQA_DH
        rows[base:base + GQA_DH] = h * GQA_DH + np.arange(GQA_DH)
    wd = w_out[:nd]
    wg = _take_cols(w_out[nd:].T, rows).T
    ws = [w.astype(BF16) for w in (wdq, wdk, wdv, wgq, wgk, wgv)]
    return ws, [gqg.astype(F32), gkg.astype(F32)], wd.astype(BF16), wg.astype(BF16)


def _layer_mods(m_layer, batch):
    d = m_layer.shape[1] // 6
    lat = m_layer[:batch].reshape(batch, 6, d)
    ctx = jnp.broadcast_to(m_layer[batch].reshape(1, 6, d), (batch, 6, d))
    return lat, ctx


def kernel(x, c, ctx, c_ctx, mod_w, mod_b, norm_g, ev_w_in, ev_mla_q_norm, ev_mla_kv_norm, ev_mla_w_uq,
           ev_mla_w_ukv, ev_gdn_conv, ev_gdn_a_log, ev_gdn_dt_bias, ev_gdn_out_norm, ev_w_out, ev_ffn_w_gu,
           ev_ffn_w_down, od_w_in, od_diff_lambda, od_diff_norm, od_gqa_q_norm, od_gqa_k_norm, od_w_out,
           od_router_w, od_moe_w_gu, od_moe_w_down, final_norm):
    batch, t_lat, d = x.shape
    t_ctx = ctx.shape[1]
    depth = mod_w.shape[0]
    assert depth == 2 and batch < 16

    cs = jnp.zeros((16, d), F32).at[:batch].set(c).at[batch].set(c_ctx)
    mods = _modulation(cs, mod_w, mod_b)

    j = 0
    mod_lat, mod_ctx = _layer_mods(mods[0], batch)
    g1 = norm_g[0, 0].reshape(1, d)
    g2 = norm_g[0, 1].reshape(1, d)
    ws, ups, wa, wg = _prep_even(ev_w_in[j], ev_mla_w_uq[j], ev_mla_w_ukv[j], ev_w_out[j])
    ws = ws + [ev_mla_q_norm[j].reshape(1, -1), ev_mla_kv_norm[j].reshape(1, -1)] + ups
    _, kind, freq, sign = _mla_layout()
    qscale = (MLA_NOPE + MLA_ROPE) ** -0.5 * LOG2E
    nfq = MLA_ROPE // 4
    tabs_lat = (_rope_tables(t_lat, kind, freq, sign, nfq, qscale, True)
                + _rope_tables(t_lat, kind, freq, sign, nfq, 1.0, True))
    tabs_ctx = (_rope_tables(t_ctx, kind, freq, sign, nfq, qscale, False)
                + _rope_tables(t_ctx, kind, freq, sign, nfq, 1.0, False))
    t_all = t_ctx + t_lat
    ctx_tiles = t_ctx // TOKEN_TILE
    qc, k_all, v_all, gqkv_c, gz_c, ab_c = _even_proj(ctx, mod_ctx, g1, tabs_ctx, ws, t_all, 0)
    ql, k_all, v_all, gqkv_l, gz_l, ab_l = _even_proj(x, mod_lat, g1, tabs_lat, ws, t_all, ctx_tiles,
                                                      bufs=(k_all, v_all))

    same = lambda hg: hg
    mla_kw = dict(groups=MLA_HEADS, g=1, kmap=same, vmap=lambda hg: hg // 2, out_dtype=BF16)
    ao_l = _flash(ql, k_all, v_all, **mla_kw)
    ao_c = _flash(qc, k_all, v_all, n_keys=t_ctx, **mla_kw)

    nh2 = 2 * GDN_HEADS
    prm = jnp.zeros((2, 2 * nh2), F32)
    prm = prm.at[0, :nh2].set(ev_gdn_a_log[j].reshape(-1)).at[1, :nh2].set(ev_gdn_dt_bias[j].reshape(-1))
    eseg = _seg_ones(GDN_HEADS * GDN_DK, GDN_DK)
    prep = _gdn_prep(gqkv_c, ab_c, ev_gdn_conv[j], prm, eseg, t_all, 0)
    prep = _gdn_prep(gqkv_l, ab_l, ev_gdn_conv[j], prm, eseg, t_all, ctx_tiles, bufs=prep)
    go_f, go_b = _gdn_scan(prep, t_ctx)

    on = jnp.tile(ev_gdn_out_norm[j], GDN_HEADS).reshape(1, -1)
    x = _mix_even(x, mod_lat, ao_l, go_f, go_b, ctx_tiles, gz_l, on, eseg, wa, wg)
    ctx = _mix_even(ctx, mod_ctx, ao_c, go_f, go_b, 0, gz_c, on, eseg, wa, wg)
    w_gu, w_dn = ev_ffn_w_gu[j].astype(BF16), ev_ffn_w_down[j].astype(BF16)
    x = _ffn(x, mod_lat, g2, w_gu, w_dn)
    ctx = _ffn(ctx, mod_ctx, g2, w_gu, w_dn)

    lambda_init = 0.8 - 0.6 * math.exp(-0.3 * 1)
    mod_lat, mod_ctx = _layer_mods(mods[1], batch)
    g1 = norm_g[1, 0].reshape(1, d)
    g2 = norm_g[1, 1].reshape(1, d)
    ws, gains, wd, wgx = _prep_odd(od_w_in[j], od_gqa_q_norm[j], od_gqa_k_norm[j], od_w_out[j])
    _, _, kind, freq, sign = _pair_layout()
    nfq = DIFF_DH // 4
    tabs_lat = _rope_tables(t_lat, kind, freq, sign, nfq, 1.0, True)
    tabs_ctx = _rope_tables(t_ctx, kind, freq, sign, nfq, 1.0, False)
    qscale = DIFF_DH ** -0.5 * LOG2E
    _, dk, dvt, _, gk, gvt = _odd_proj(ctx, mod_ctx, g1, tabs_ctx, ws, gains, qscale, t_all, 0)
    dq, dk, dvt, gq, gk, gvt = _odd_proj(x, mod_lat, g1, tabs_lat, ws, gains, qscale, t_all, ctx_tiles,
                                         bufs=(dk, dvt, gk, gvt))
    do = _flash(dq, dk, dvt, groups=DIFF_HEADS, g=2, kmap=same, vmap=same, out_dtype=F32)
    zero = lambda hg: 0
    go = _flash(gq, gk, gvt, groups=1, g=GQA_HEADS, kmap=zero, vmap=zero, out_dtype=BF16)
    x = _mix_odd(x, mod_lat, do, go, od_diff_lambda[j], od_diff_norm[j].reshape(1, -1), wd, wgx, lambda_init)

    rw = jnp.zeros((d, LANES), F32).at[:, :N_EXPERTS].set(od_router_w[j])
    h2, route = _router(x, mod_lat, g2, rw)
    n_tok = batch * t_lat
    top_e = route[..., 0:TOP_K].astype(jnp.int32).reshape(n_tok, TOP_K)
    buf_tok, dest, block_e = _moe_plan(top_e, n_tok)
    y = _moe_experts(h2.reshape(n_tok, d), buf_tok, dest, block_e, od_moe_w_gu[j].astype(BF16),
                     od_moe_w_down[j].astype(BF16))
    return _moe_combine_final(x, mod_lat, route, y, final_norm.reshape(1, d))
```

```python
import functools
import math

import numpy as np
import jax
import jax.numpy as jnp
from jax import lax
from jax.experimental import pallas as pl
from jax.experimental.pallas import tpu as pltpu

F32 = jnp.float32
BF16 = jnp.bfloat16

GRID_W = 64
ROPE_BASE = 10000.0
NORM_EPS = 1e-6
MLA_HEADS, MLA_NOPE, MLA_ROPE, MLA_V = 8, 64, 32, 64
MLA_Q_RANK, MLA_KV_RANK = 384, 256
GDN_HEADS, GDN_DK, GDN_DV, GDN_CONV, GDN_CHUNK = 8, 64, 64, 5, 64
GDN_QKV = GDN_HEADS * (2 * GDN_DK + GDN_DV)
DIFF_HEADS, DIFF_DH = 4, 64
GQA_HEADS, GQA_KV_HEADS, GQA_DH = 8, 2, 64
N_EXPERTS, TOP_K, MOE_BLOCK = 8, 2, 512

LANES = 128
SUBLANES = 8
VMEM_LIMIT = 48 * 1024 * 1024
LOG2E = 1.4426950408889634

TOKEN_TILE = 256
ATTN_ROWS = 1024
ATTN_KEYS = 256
ATTN_TILES = 2
ATTN_UNROLL = 6


def _cparams(*sem):
    return pltpu.CompilerParams(dimension_semantics=sem, vmem_limit_bytes=VMEM_LIMIT)


def _dot(a, b):
    return jnp.dot(a, b, preferred_element_type=F32)


def _dot_t(a, b):
    return lax.dot_general(a, b, (((1,), (1,)), ((), ())), preferred_element_type=F32)


def _split3(x):
    h1 = x.astype(BF16)
    r1 = x - h1.astype(F32)
    h2 = r1.astype(BF16)
    h3 = (r1 - h2.astype(F32)).astype(BF16)
    return h1, h2, h3


def _silu(x):
    return x * jax.nn.sigmoid(x)


def _rms_mod(x, g, sc, sh):
    ms = jnp.mean(x * x, axis=-1, keepdims=True)
    return (x * lax.rsqrt(ms + NORM_EPS) * g) * (1.0 + sc) + sh


def _seg_sumsq(x, eseg):
    sq = x * x
    hi = sq.astype(BF16)
    lo = (sq - hi.astype(F32)).astype(BF16)
    return _dot(hi, eseg) + _dot(lo, eseg)


def _rope(x, c, s):
    return x * c + pltpu.roll(x, LANES // 2, 1) * s


def _mla_layout():
    src = -np.ones(LANES, np.int64)
    kind = np.zeros(LANES, np.int64)
    freq = np.zeros(LANES, np.int64)
    sign = np.zeros(LANES, np.float32)
    nf = MLA_ROPE // 4
    for f in range(nf):
        src[f] = MLA_NOPE + f; kind[f] = 1; freq[f] = f; sign[f] = -1.0
        src[nf + f] = MLA_NOPE + 2 * nf + f; kind[nf + f] = 2; freq[nf + f] = f; sign[nf + f] = -1.0
        src[64 + f] = MLA_NOPE + nf + f; kind[64 + f] = 1; freq[64 + f] = f; sign[64 + f] = 1.0
        src[64 + nf + f] = MLA_NOPE + 3 * nf + f; kind[64 + nf + f] = 2; freq[64 + nf + f] = f
        sign[64 + nf + f] = 1.0
    src[16:64] = np.arange(0, 48)
    src[80:96] = np.arange(48, 64)
    return src, kind, freq, sign


def _pair_layout():
    lane = np.arange(LANES)
    region = lane // 32
    m = region % 2
    is_b = region // 2
    within = lane % 32
    is_col = within // 16
    f = within % 16
    src = is_col * 32 + is_b * 16 + f
    kind = 1 + is_col
    sign = np.where(is_b == 0, -1.0, 1.0).astype(np.float32)
    return m, src, kind, f, sign


def _rope_tables(t_len, kind, freq, sign, n_freq, scale, with_pos):
    kind_j = jnp.asarray(kind)
    if not with_pos:
        c = jnp.full((t_len, LANES), scale, F32)
        return c, jnp.zeros((t_len, LANES), F32)
    n_rows = t_len // GRID_W
    inv = 1.0 / (ROPE_BASE ** (jnp.arange(n_freq, dtype=F32) / n_freq))
    rows = jnp.repeat(jnp.arange(n_rows, dtype=F32), GRID_W)
    cols = jnp.tile(jnp.arange(GRID_W, dtype=F32), n_rows)
    ang_r = (rows[:, None] * inv)[:, freq]
    ang_c = (cols[:, None] * inv)[:, freq]
    ang = jnp.where(kind_j[None, :] == 1, ang_r, ang_c)
    has = (kind_j > 0)[None, :]
    c = jnp.where(has, jnp.cos(ang), 1.0) * scale
    s = jnp.where(has, jnp.sin(ang) * jnp.asarray(sign)[None, :], 0.0) * scale
    return c.astype(F32), s.astype(F32)


def _take_cols(w, idx):
    idx = np.asarray(idx)
    wz = jnp.concatenate([w, jnp.zeros((w.shape[0], 1), w.dtype)], axis=1)
    return wz[:, np.where(idx < 0, w.shape[1], idx)]


def _seg_ones(width, seg):
    r = np.arange(width)
    return jnp.asarray((r[:, None] // seg) == (r[None, :] // seg), dtype=BF16)


def _mod_kernel(c_ref, w_ref, b_ref, o_ref):
    c = c_ref[...]
    s1, s2, s3 = _split3(_silu(c))
    w1, w2, _ = _split3(w_ref[...])
    acc = _dot(s1, w1) + _dot(s1, w2) + _dot(s2, w1) + _dot(s2, w2) + _dot(s3, w1)
    o_ref[...] = acc + b_ref[...]


def _modulation(cs, mod_w, mod_b):
    depth, d, n = mod_w.shape
    tn = 512
    return pl.pallas_call(
        _mod_kernel,
        grid=(depth, n // tn),
        in_specs=[
            pl.BlockSpec(cs.shape, lambda l, j: (0, 0)),
            pl.BlockSpec((None, d, tn), lambda l, j: (l, 0, j)),
            pl.BlockSpec((None, 1, tn), lambda l, j: (l, 0, j)),
        ],
        out_specs=pl.BlockSpec((None, cs.shape[0], tn), lambda l, j: (l, 0, j)),
        out_shape=jax.ShapeDtypeStruct((depth, cs.shape[0], n), F32),
        compiler_params=_cparams("parallel", "parallel"),
        name="modulation",
    )(cs, mod_w, mod_b.reshape(depth, 1, n))


def _even_proj_kernel(x_ref, mod_ref, g_ref, cq_t, sq_t, ck_t, sk_t,
                      wcq, wckv, wkpe, wgqkv, wgz, wab, qn, kvn, wuq, wuk, wuv, *rest):
    q_out, k_out, v_out, gqkv_out, gz_out, ab_out = rest[-6:]
    h = _rms_mod(x_ref[...], g_ref[...], mod_ref[1:2, :], mod_ref[0:1, :]).astype(BF16)
    cq = _dot(h, wcq[...])
    ckv = _dot(h, wckv[...])
    kpe = _dot(h, wkpe[...])
    nq = (cq * lax.rsqrt(jnp.mean(cq * cq, axis=-1, keepdims=True) + NORM_EPS) * qn[...]).astype(BF16)
    nkv = (ckv * lax.rsqrt(jnp.mean(ckv * ckv, axis=-1, keepdims=True) + NORM_EPS) * kvn[...]).astype(BF16)
    q = _dot(nq, wuq[...])
    kn = _dot(nkv, wuk[...])
    kper = _rope(kpe, ck_t[...], sk_t[...])
    cqv, sqv = cq_t[...], sq_t[...]
    for hd in range(MLA_HEADS):
        sl = slice(hd * LANES, (hd + 1) * LANES)
        q_out[:, sl] = _rope(q[:, sl], cqv, sqv).astype(BF16)
        k_out[:, sl] = (kn[:, sl] + kper).astype(BF16)
    _store_vt(v_out, _dot_t(wuv[...], nkv))
    gqkv_out[...] = _dot(h, wgqkv[...])
    gz_out[...] = _dot(h, wgz[...])
    ab_out[...] = _dot(h, wab[...])


def _full(a):
    nd = a.ndim
    return pl.BlockSpec(a.shape, lambda *_: (0,) * nd)


def _alias_shared(args, in_specs, shapes, shared, bufs):
    shared = list(shared)
    if bufs is None:
        bufs = [jnp.zeros(shapes[o].shape, shapes[o].dtype) for o in shared]
    aliases = {len(args) + n: o for n, o in enumerate(shared)}
    return args + list(bufs), in_specs + [pl.BlockSpec(memory_space=pl.ANY)] * len(shared), aliases


def _store_vt(vt_out, vt):
    for n in range(vt_out.shape[0]):
        vt_out[n] = vt[n * LANES:(n + 1) * LANES, :].astype(vt_out.dtype)


def _proj_outputs(b, t, tt, tile_off, outs):
    tm = TOKEN_TILE
    assert tm == ATTN_KEYS
    specs, shapes = [], []
    for w, dt, kind in outs:
        if kind == "own":
            specs.append(pl.BlockSpec((None, tm, w), lambda bi, i: (bi, i, 0)))
            shapes.append(jax.ShapeDtypeStruct((b, t, w), dt))
        elif kind == "keys":
            specs.append(pl.BlockSpec((None, tm, w), lambda bi, i: (bi, i + tile_off, 0)))
            shapes.append(jax.ShapeDtypeStruct((b, tt, w), dt))
        else:
            specs.append(pl.BlockSpec((None, kind, None, LANES, tm), lambda bi, i: (bi, 0, i + tile_off, 0, 0)))
            shapes.append(jax.ShapeDtypeStruct((b, kind, tt // tm, LANES, tm), dt))
    shared = [n for n, (_, _, kind) in enumerate(outs) if kind != "own"]
    return specs, shapes, shared


def _even_proj(x, mod, g, tabs, ws, tt, tile_off, bufs=None):
    b, t, d = x.shape
    tm = TOKEN_TILE
    tok = lambda w: pl.BlockSpec((None, tm, w), lambda bi, i: (bi, i, 0))
    tab = pl.BlockSpec((tm, LANES), lambda bi, i: (i, 0))
    outs = ((MLA_HEADS * LANES, BF16, "own"), (MLA_HEADS * LANES, BF16, "keys"),
            (0, BF16, MLA_HEADS * MLA_V // LANES), (GDN_QKV, F32, "own"), (GDN_HEADS * GDN_DV, F32, "own"),
            (4 * GDN_HEADS, F32, "own"))
    specs, shapes, shared = _proj_outputs(b, t, tt, tile_off, outs)
    args = [x, mod, g, *tabs, *ws]
    in_specs = [tok(d), pl.BlockSpec((None, 6, d), lambda bi, i: (bi, 0, 0)), _full(g),
                tab, tab, tab, tab] + [_full(w) for w in ws]
    args, in_specs, aliases = _alias_shared(args, in_specs, shapes, shared, bufs)
    return pl.pallas_call(
        _even_proj_kernel,
        grid=(b, t // tm),
        in_specs=in_specs,
        out_specs=specs,
        out_shape=shapes,
        input_output_aliases=aliases,
        compiler_params=_cparams("parallel", "parallel"),
        name="even_proj",
    )(*args)


def _flash_kernel(q_ref, k_ref, vt_ref, o_ref, acc_sc, s_a, s_b, *, g, tq, tk, nsub):
    q = jnp.concatenate([q_ref[:, i * LANES:(i + 1) * LANES] for i in range(g)], axis=0)
    rows = g * tq
    acc_sc[...] = jnp.zeros(acc_sc.shape, F32)
    nt = ATTN_TILES

    def scores(t0, n):
        return _dot_t(k_ref[pl.ds(pl.multiple_of(t0 * tk, tk), n * tk), :], q)

    def update(st, t0, n, m):
        m_new = jnp.maximum(m, jnp.max(st, axis=0, keepdims=True))
        p = jnp.exp2(st - m_new).astype(BF16)
        vt = jnp.concatenate([vt_ref[t0 + u] for u in range(n)] , axis=1)
        lhs = jnp.concatenate([vt, jnp.ones((2 * SUBLANES, n * tk), BF16)], axis=0)
        acc_sc[...] = jnp.exp2(m - m_new) * acc_sc[...] + _dot(lhs, p)
        return m_new

    bufs = (s_a, s_b)
    n_full, rem = divmod(nsub, nt)
    loop_units = ((n_full - 1) // ATTN_UNROLL) * ATTN_UNROLL if n_full else 0
    tail = [(u * nt, nt) for u in range(loop_units, n_full)] + ([(n_full * nt, rem)] if rem else [])
    first = (0, nt) if n_full else tail[0]
    s_a[0:first[1] * tk, :] = scores(*first)

    def body(i, m):
        for c in range(ATTN_UNROLL):
            u = i * ATTN_UNROLL + c
            bufs[(c + 1) % 2][...] = scores((u + 1) * nt, nt)
            m = update(bufs[c % 2][...], u * nt, nt, m)
        return m

    m = lax.fori_loop(0, loop_units // ATTN_UNROLL, body, jnp.full((1, rows), -jnp.inf, F32))
    for c, (t0, n) in enumerate(tail):
        if c + 1 < len(tail):
            t1, n1 = tail[c + 1]
            bufs[(c + 1) % 2][0:n1 * tk, :] = scores(t1, n1)
        m = update(bufs[c % 2][0:n * tk, :], t0, n, m)
    on = acc_sc[0:LANES, :] * (1.0 / acc_sc[LANES:LANES + 1, :])
    o = on.T
    for i in range(g):
        o_ref[:, i * LANES:(i + 1) * LANES] = o[i * tq:(i + 1) * tq, :].astype(o_ref.dtype)


def _flash(q, k, vt, *, groups, g, kmap, vmap, out_dtype, n_keys=None):
    b, tq_total, _ = q.shape
    tk_total = k.shape[1] if n_keys is None else n_keys
    tq = min(ATTN_ROWS // g, tq_total)
    tk = ATTN_KEYS
    nsub = tk_total // tk
    assert vt.shape[3:] == (LANES, tk) and vt.shape[2] >= nsub
    rows = g * tq
    kern = functools.partial(_flash_kernel, g=g, tq=tq, tk=tk, nsub=nsub)
    return pl.pallas_call(
        kern,
        grid=(b, groups, tq_total // tq),
        in_specs=[
            pl.BlockSpec((None, tq, g * LANES), lambda bi, hg, i: (bi, i, hg)),
            pl.BlockSpec((None, tk_total, LANES), lambda bi, hg, i: (bi, 0, kmap(hg))),
            pl.BlockSpec((None, None, nsub, LANES, tk), lambda bi, hg, i: (bi, vmap(hg), 0, 0, 0)),
        ],
        out_specs=pl.BlockSpec((None, tq, g * LANES), lambda bi, hg, i: (bi, i, hg)),
        out_shape=jax.ShapeDtypeStruct(q.shape, out_dtype),
        scratch_shapes=[pltpu.VMEM((LANES + 2 * SUBLANES, rows), F32), pltpu.VMEM((ATTN_TILES * tk, rows), F32),
                        pltpu.VMEM((ATTN_TILES * tk, rows), F32)],
        compiler_params=_cparams("parallel", "parallel", "parallel"),
        name="flash_attention",
    )(q, k, vt)


def _gdn_prep_kernel(x_ref, prev_ref, next_ref, ab_ref, cw_ref, prm_ref, eseg_ref,
                     lf_ref, lb_ref, la_ref, *rest):
    q_out, k_out, v_out, s1_out, s2_out, xe_sc = rest[-6:]
    i = pl.program_id(1)
    tm = x_ref.shape[0]
    halo = SUBLANES
    xe_sc[0:halo, :] = prev_ref[...] * (i > 0).astype(F32)
    xe_sc[halo:halo + tm, :] = x_ref[...]
    xe_sc[halo + tm:2 * halo + tm, :] = next_ref[...] * (i < pl.num_programs(1) - 1).astype(F32)
    acc = jnp.zeros(x_ref.shape, F32)
    for j in range(GDN_CONV):
        acc = acc + cw_ref[j:j + 1, :] * xe_sc[pl.ds(halo - GDN_CONV // 2 + j, tm), :]
    y = _silu(acc)
    hw = GDN_HEADS * GDN_DK
    eseg = eseg_ref[...]
    q = y[:, 0:hw]
    k = y[:, hw:2 * hw]
    qn = q * lax.rsqrt(_seg_sumsq(q, eseg) + NORM_EPS) * (GDN_DK ** -0.5)
    kn = k * lax.rsqrt(_seg_sumsq(k, eseg) + NORM_EPS)
    v = y[:, 2 * hw:]
    for hd in range(GDN_HEADS):
        q_out[hd] = qn[:, hd * GDN_DK:(hd + 1) * GDN_DK]
        k_out[hd] = kn[:, hd * GDN_DK:(hd + 1) * GDN_DK]
        v_out[hd] = v[:, hd * GDN_DV:(hd + 1) * GDN_DV]

    ab = ab_ref[...]
    nh2 = 2 * GDN_HEADS
    lane = lax.broadcasted_iota(jnp.int32, ab.shape, 1)
    z = ab + prm_ref[1:2, :]
    softplus = jnp.maximum(z, 0.0) + jnp.log1p(jnp.exp(-jnp.abs(z)))
    gate = jnp.where(lane < nh2, -jnp.exp(prm_ref[0:1, :]) * softplus, 0.0)
    g1, g2, g3 = _split3(gate)
    lf, lb, la = lf_ref[...], lb_ref[...], la_ref[...]
    cum_f = _dot(lf, g1) + _dot(lf, g2) + _dot(lf, g3)
    cum_b = _dot(lb, g1) + _dot(lb, g2) + _dot(lb, g3)
    tot = _dot(la, g1) + _dot(la, g2) + _dot(la, g3)
    cum = jnp.where(lane < GDN_HEADS, cum_f, cum_b)
    s1_out[...] = jnp.where(lane < nh2, cum, jax.nn.sigmoid(ab))
    s2_out[...] = tot


def _gdn_prep(gqkv, ab, conv_w, prm, eseg, tt, tile_off, bufs=None):
    b, t, w = gqkv.shape
    tm = TOKEN_TILE
    nt8 = t // SUBLANES
    per = tm // SUBLANES
    r = np.arange(tm)
    same = (r[:, None] // GDN_CHUNK) == (r[None, :] // GDN_CHUNK)
    lf = jnp.asarray(same & (r[None, :] <= r[:, None]), dtype=BF16)
    lb = jnp.asarray(same & (r[None, :] >= r[:, None]), dtype=BF16)
    la = jnp.asarray(same, dtype=BF16)
    halo = gqkv.reshape(b, nt8, SUBLANES, w)
    tok = lambda wd: pl.BlockSpec((None, tm, wd), lambda bi, i: (bi, i, 0))
    hw = GDN_HEADS * GDN_DK
    nab = ab.shape[-1]
    head = lambda wd: pl.BlockSpec((None, GDN_HEADS, tm, wd), lambda bi, i: (bi, 0, i + tile_off, 0))
    sca = pl.BlockSpec((None, tm, nab), lambda bi, i: (bi, i + tile_off, 0))
    in_specs = [
        tok(w),
        pl.BlockSpec((None, None, SUBLANES, w), lambda bi, i: (bi, jnp.maximum(i * per - 1, 0), 0, 0)),
        pl.BlockSpec((None, None, SUBLANES, w), lambda bi, i: (bi, jnp.minimum((i + 1) * per, nt8 - 1), 0, 0)),
        tok(nab), _full(conv_w), _full(prm), _full(eseg), _full(lf), _full(lb), _full(la),
    ]
    args = [gqkv, halo, halo, ab, conv_w, prm, eseg, lf, lb, la]
    shapes = ([jax.ShapeDtypeStruct((b, GDN_HEADS, tt, GDN_DK), F32)] * 2
              + [jax.ShapeDtypeStruct((b, GDN_HEADS, tt, GDN_DV), F32)]
              + [jax.ShapeDtypeStruct((b, tt, nab), F32)] * 2)
    args, in_specs, aliases = _alias_shared(args, in_specs, shapes, range(len(shapes)), bufs)
    return pl.pallas_call(
        _gdn_prep_kernel,
        grid=(b, t // tm),
        in_specs=in_specs,
        out_specs=[head(GDN_DK), head(GDN_DK), head(GDN_DV), sca, sca],
        out_shape=shapes,
        input_output_aliases=aliases,
        scratch_shapes=[pltpu.VMEM((tm + 2 * SUBLANES, w), F32)],
        compiler_params=_cparams("parallel", "parallel"),
        name="gdn_prep",
    )(*args)


def _bmm(a, b):
    return jnp.einsum("nij,njk->nik", a, b, preferred_element_type=F32)


def _bmm_t(a, b):
    return jnp.einsum("nik,njk->nij", a, b, preferred_element_type=F32)


def _gdn_intra_kernel(q_ref, k_ref, v_ref, gr_ref, a_out, b_out, qe_out, o0_out, *, hb, cb):
    c_len = GDN_CHUNK
    n = hb * cb
    sgn = 1 - 2 * pl.program_id(0)
    row = lax.broadcasted_iota(jnp.int32, (c_len, c_len), 0)
    col = lax.broadcasted_iota(jnp.int32, (c_len, c_len), 1)
    ahead = (row - col) * sgn
    eye_f = (row == col).astype(F32)
    eye_b = jnp.broadcast_to(eye_f.astype(BF16), (n, c_len, c_len))
    q = q_ref[...].reshape(n, c_len, GDN_DK)
    k = k_ref[...].reshape(n, c_len, GDN_DK)
    v = v_ref[...].reshape(n, c_len, GDN_DV)
    g_cols = gr_ref[:, :, 0:1, :].reshape(n, 1, c_len)
    beta_row = gr_ref[:, :, 1:2, :].reshape(n, 1, c_len)
    gl = gr_ref[:, :, 2:3, :].reshape(n, 1, c_len)

    ones_b = jnp.broadcast_to(jnp.ones((c_len, c_len), BF16), (n, c_len, c_len))

    def on_sublanes(r, pieces):
        parts = _split3(eye_f * r)[:pieces]
        return functools.reduce(lambda a, b: a + b, [_bmm_t(p, ones_b) for p in parts])

    gc = on_sublanes(g_cols, 2)
    beta = on_sublanes(beta_row, 1)
    eg, ek, cd = jnp.exp(gc), jnp.exp(gl - gc), jnp.exp(gl)
    decay = jnp.where(ahead >= 0, jnp.exp(jnp.minimum(gc - g_cols, 0.0)), 0.0)
    kb = k * beta
    kbf = k.astype(BF16)
    low = jnp.where(ahead > 0, _bmm_t(kb.astype(BF16), kbf) * decay, 0.0)
    qk = _bmm_t(q.astype(BF16), kbf) * decay
    mpow = jnp.where((row // 8) == (col // 8), -low, 0.0)
    tinv = eye_f + mpow
    for _ in range(2):
        mb = mpow.astype(BF16)
        mpow = _bmm(mb, mb)
        tinv = tinv + _bmm(tinv.astype(BF16), mpow.astype(BF16))
    for s in (8, 16, 32):
        off = ((row // (2 * s)) == (col // (2 * s))) & ((row // s) != (col // s))
        tb = tinv.astype(BF16)
        tinv = tinv - _bmm(_bmm(tb, jnp.where(off, low, 0.0).astype(BF16)).astype(BF16), tb)
    tb = tinv.astype(BF16)
    wu = _bmm(tb, jnp.concatenate([kb * eg, v * beta], axis=-1).astype(BF16)).astype(BF16)
    kdt = _bmm_t(eye_b, (k * ek).astype(BF16)).astype(BF16)
    ab = _bmm(kdt, wu)
    qo = _bmm(qk.astype(BF16), wu)
    a_out[...] = (cd * eye_f - ab[:, :, 0:GDN_DK]).reshape(a_out.shape)
    b_out[...] = ab[:, :, GDN_DK:].reshape(b_out.shape)
    qe_out[...] = (q * eg - qo[:, :, 0:GDN_DK]).reshape(qe_out.shape)
    o0_out[...] = qo[:, :, GDN_DK:].reshape(o0_out.shape)


def _gdn_intra(q, k, v, gr):
    b, h, tt, dk = q.shape
    assert dk == GDN_CHUNK and v.shape[-1] == GDN_CHUNK
    nc = tt // GDN_CHUNK
    hb, cb = h, 4
    rows = cb * GDN_CHUNK
    qkv = pl.BlockSpec((None, hb, rows, dk), lambda d, bi, c: (bi, 0, c, 0))
    tokb = pl.BlockSpec((None, None, hb, rows, dk), lambda d, bi, c: (d, bi, 0, c, 0))
    matb = pl.BlockSpec((None, None, hb, cb, dk, dk), lambda d, bi, c: (d, bi, 0, c, 0, 0))
    return pl.pallas_call(
        functools.partial(_gdn_intra_kernel, hb=hb, cb=cb),
        grid=(2, b, nc // cb),
        in_specs=[qkv, qkv, qkv,
                  pl.BlockSpec((None, None, hb, cb, SUBLANES, GDN_CHUNK), lambda d, bi, c: (d, bi, 0, c, 0, 0))],
        out_specs=[matb, matb, tokb, tokb],
        out_shape=[jax.ShapeDtypeStruct((2, b, h, nc, dk, dk), F32)] * 2
                  + [jax.ShapeDtypeStruct((2, b, h, tt, dk), F32)] * 2,
        compiler_params=_cparams("parallel", "parallel", "parallel"),
        name="gdn_intra",
    )(q, k, v, gr)


def _gdn_inter_kernel(af, bf, qf, of, ab, bb, qb, ob, o_f, o_b, s_sc):
    @pl.when(pl.program_id(1) == 0)
    def _():
        s_sc[...] = jnp.zeros(s_sc.shape, F32)

    nbat, nh, kc, dk, _ = af.shape
    n = nbat * nh
    c_len = GDN_CHUNK
    mat = lambda ref, i: ref[:, :, i].reshape(n, dk, dk)
    tok = lambda ref, i: ref[:, :, i * c_len:(i + 1) * c_len, :].reshape(n, c_len, dk)
    s_f, s_b = s_sc[0], s_sc[1]
    for i in range(kc):
        r = kc - 1 - i
        sf, sb = s_f.astype(BF16), s_b.astype(BF16)
        o_f[:, :, i * c_len:(i + 1) * c_len, :] = (
            _bmm(tok(qf, i).astype(BF16), sf) + tok(of, i)).reshape(nbat, nh, c_len, dk)
        o_b[:, :, r * c_len:(r + 1) * c_len, :] = (
            _bmm(tok(qb, r).astype(BF16), sb) + tok(ob, r)).reshape(nbat, nh, c_len, dk)
        s_f = _bmm(mat(af, i).astype(BF16), sf) + mat(bf, i)
        s_b = _bmm(mat(ab, r).astype(BF16), sb) + mat(bb, r)
    s_sc[0] = s_f
    s_sc[1] = s_b


def _gdn_inter(a, bm, qe, o0, nc_ctx):
    _, b, h, nc, dk, _ = a.shape
    kc = math.gcd(nc_ctx, nc - nc_ctx, 4)
    nbat = math.gcd(b, 2)
    nb, nb_ctx = nc // kc, nc_ctx // kc
    rev = lambda c: jnp.where(c < nb_ctx, nb_ctx - 1 - c, nb - 1 - (c - nb_ctx))
    mat_f = pl.BlockSpec((None, nbat, h, kc, dk, dk), lambda bi, c: (0, bi, 0, c, 0, 0))
    mat_b = pl.BlockSpec((None, nbat, h, kc, dk, dk), lambda bi, c: (1, bi, 0, rev(c), 0, 0))
    tok_f = pl.BlockSpec((None, nbat, h, kc * GDN_CHUNK, dk), lambda bi, c: (0, bi, 0, c, 0))
    tok_b = pl.BlockSpec((None, nbat, h, kc * GDN_CHUNK, dk), lambda bi, c: (1, bi, 0, rev(c), 0))
    out_f = pl.BlockSpec((nbat, h, kc * GDN_CHUNK, dk), lambda bi, c: (bi, 0, c, 0))
    out_b = pl.BlockSpec((nbat, h, kc * GDN_CHUNK, dk), lambda bi, c: (bi, 0, rev(c), 0))
    osd = jax.ShapeDtypeStruct(qe.shape[1:], F32)
    return pl.pallas_call(
        _gdn_inter_kernel,
        grid=(b // nbat, nb),
        in_specs=[mat_f, mat_f, tok_f, tok_f, mat_b, mat_b, tok_b, tok_b],
        out_specs=[out_f, out_b],
        out_shape=[osd, osd],
        scratch_shapes=[pltpu.VMEM((2, nbat * h, dk, dk), F32)],
        compiler_params=_cparams("parallel", "arbitrary"),
        name="gdn_inter",
    )(a, bm, qe, o0, a, bm, qe, o0)


def _gdn_scan(prep, tc):
    q, k, v, s1, s2 = prep
    b, nh, tt, _ = q.shape
    s1t, s2t = s1.transpose(0, 2, 1), s2.transpose(0, 2, 1)
    nc = tt // GDN_CHUNK
    zero = jnp.zeros((b, nh, nc, GDN_CHUNK), F32)
    gr = jnp.stack([
        jnp.stack([x.reshape(b, nh, nc, GDN_CHUNK) for x in
                   (s1t[:, d * nh:(d + 1) * nh], s1t[:, (2 + d) * nh:(3 + d) * nh], s2t[:, d * nh:(d + 1) * nh])]
                  + [zero] * (SUBLANES - 3), axis=3)
        for d in range(2)])
    a, bm, qe, o0 = _gdn_intra(q, k, v, gr)
    return _gdn_inter(a, bm, qe, o0, tc // GDN_CHUNK)


def _mix_even_kernel(x_ref, mod_ref, ao_ref, gf_ref, gb_ref, gz_ref, on_ref, eseg_ref, wa_ref, wg_ref, out_ref):
    o = jnp.concatenate([gf_ref[hd] + gb_ref[hd] for hd in range(GDN_HEADS)], axis=-1)
    ms = _seg_sumsq(o, eseg_ref[...]) * (1.0 / GDN_DV)
    y = o * lax.rsqrt(ms + NORM_EPS) * on_ref[...] * _silu(gz_ref[...])
    mix = _dot(ao_ref[...], wa_ref[...]) + _dot(y.astype(BF16), wg_ref[...])
    out_ref[...] = x_ref[...] + mod_ref[2:3, :] * mix


def _mix_even(x, mod, ao, go_f, go_b, tile_off, gz, on, eseg, wa, wg):
    b, t, d = x.shape
    tm = TOKEN_TILE
    tok = lambda w: pl.BlockSpec((None, tm, w), lambda bi, i: (bi, i, 0))
    head = pl.BlockSpec((None, GDN_HEADS, tm, GDN_DV), lambda bi, i: (bi, 0, i + tile_off, 0))
    return pl.pallas_call(
        _mix_even_kernel,
        grid=(b, t // tm),
        in_specs=[tok(d), pl.BlockSpec((None, 6, d), lambda bi, i: (bi, 0, 0)), tok(ao.shape[-1]),
                  head, head, tok(gz.shape[-1]), _full(on), _full(eseg), _full(wa), _full(wg)],
        out_specs=tok(d),
        out_shape=jax.ShapeDtypeStruct(x.shape, F32),
        compiler_params=_cparams("parallel", "parallel"),
        name="mix_even",
    )(x, mod, ao, go_f, go_b, gz, on, eseg, wa, wg)


def _ffn_kernel(x_ref, mod_ref, g_ref, wg_ref, wu_ref, wd_ref, out_ref, h_sc, acc_sc):
    j = pl.program_id(2)

    @pl.when(j == 0)
    def _():
        h_sc[...] = _rms_mod(x_ref[...], g_ref[...], mod_ref[4:5, :], mod_ref[3:4, :]).astype(BF16)
        acc_sc[...] = jnp.zeros(acc_sc.shape, F32)

    h = h_sc[...]
    act = (_silu(_dot(h, wg_ref[...])) * _dot(h, wu_ref[...])).astype(BF16)
    acc_sc[...] += _dot(act, wd_ref[...])

    @pl.when(j == pl.num_programs(2) - 1)
    def _():
        out_ref[...] = x_ref[...] + mod_ref[5:6, :] * acc_sc[...]


def _ffn_tile(f):
    for cand in (1408, 1792, 1024, 896, 768, 512, 256, 128):
        if f % cand == 0:
            return cand
    raise ValueError(f"ffn width {f} is not a multiple of 128")


def _ffn(x, mod, g, w_gu, w_down):
    b, t, d = x.shape
    f = w_down.shape[0]
    tm = 512 if t % 512 == 0 else TOKEN_TILE
    tf = _ffn_tile(f)
    nf = f // tf
    tok = pl.BlockSpec((None, tm, d), lambda bi, i, j: (bi, i, 0))
    return pl.pallas_call(
        _ffn_kernel,
        grid=(b, t // tm, nf),
        in_specs=[tok, pl.BlockSpec((None, 6, d), lambda bi, i, j: (bi, 0, 0)),
                  pl.BlockSpec(g.shape, lambda bi, i, j: (0, 0)),
                  pl.BlockSpec((d, tf), lambda bi, i, j: (0, j)),
                  pl.BlockSpec((d, tf), lambda bi, i, j: (0, nf + j)),
                  pl.BlockSpec((tf, d), lambda bi, i, j: (j, 0))],
        out_specs=tok,
        out_shape=jax.ShapeDtypeStruct(x.shape, F32),
        scratch_shapes=[pltpu.VMEM((tm, d), BF16), pltpu.VMEM((tm, d), F32)],
        compiler_params=_cparams("parallel", "parallel", "arbitrary"),
        name="ffn",
    )(x, mod, g, w_gu, w_gu, w_down)


def _odd_proj_kernel(x_ref, mod_ref, g_ref, c_t, s_t, wdq, wdk, wdv, wgq, wgk, wgv, gqg, gkg, *rest, qscale):
    dq_out, dk_out, dv_out, gq_out, gk_out, gv_out = rest[-6:]
    h = _rms_mod(x_ref[...], g_ref[...], mod_ref[1:2, :], mod_ref[0:1, :]).astype(BF16)
    c, s = c_t[...], s_t[...]
    lane = lax.broadcasted_iota(jnp.int32, c.shape, 1)
    is_m1 = ((lane // 32) % 2) == 1
    dq = _dot(h, wdq[...])
    dk = _dot(h, wdk[...])
    for hd in range(DIFF_HEADS):
        sl = slice(hd * LANES, (hd + 1) * LANES)
        r = _rope(dq[:, sl], c, s) * qscale
        dq_out[:, 2 * hd * LANES:(2 * hd + 1) * LANES] = jnp.where(is_m1, 0.0, r).astype(BF16)
        dq_out[:, (2 * hd + 1) * LANES:(2 * hd + 2) * LANES] = jnp.where(is_m1, r, 0.0).astype(BF16)
        dk_out[:, sl] = _rope(dk[:, sl], c, s).astype(BF16)
    _store_vt(dv_out, _dot_t(wdv[...], h))
    gq = _dot(h, wgq[...])
    gqgv = gqg[...]
    inv_dh = 1.0 / GQA_DH
    for hd in range(GQA_HEADS):
        sl = slice(hd * LANES, (hd + 1) * LANES)
        xh = gq[:, sl]
        ms = jnp.sum(xh * xh, axis=-1, keepdims=True) * inv_dh
        xn = xh * lax.rsqrt(ms + NORM_EPS) * gqgv[:, sl]
        gq_out[:, sl] = (_rope(xn, c, s) * qscale).astype(BF16)
    gk = _dot(h, wgk[...])
    sq = gk * gk
    ms0 = jnp.sum(jnp.where(is_m1, 0.0, sq), axis=-1, keepdims=True) * inv_dh
    ms1 = jnp.sum(jnp.where(is_m1, sq, 0.0), axis=-1, keepdims=True) * inv_dh
    rs = jnp.where(is_m1, lax.rsqrt(ms1 + NORM_EPS), lax.rsqrt(ms0 + NORM_EPS))
    gk_out[...] = _rope(gk * rs * gkg[...], c, s).astype(BF16)
    _store_vt(gv_out, _dot_t(wgv[...], h))


def _odd_proj(x, mod, g, tabs, ws, gains, qscale, tt, tile_off, bufs=None):
    b, t, d = x.shape
    tm = TOKEN_TILE
    tok = lambda w: pl.BlockSpec((None, tm, w), lambda bi, i: (bi, i, 0))
    tab = pl.BlockSpec((tm, LANES), lambda bi, i: (i, 0))
    outs = ((2 * DIFF_HEADS * LANES, BF16, "own"), (DIFF_HEADS * LANES, BF16, "keys"), (0, BF16, DIFF_HEADS),
            (GQA_HEADS * LANES, BF16, "own"), (LANES, BF16, "keys"), (0, BF16, 1))
    specs, shapes, shared = _proj_outputs(b, t, tt, tile_off, outs)
    args = [x, mod, g, *tabs, *ws, *gains]
    in_specs = ([tok(d), pl.BlockSpec((None, 6, d), lambda bi, i: (bi, 0, 0)), _full(g), tab, tab]
                + [_full(w) for w in ws] + [_full(w) for w in gains])
    args, in_specs, aliases = _alias_shared(args, in_specs, shapes, shared, bufs)
    return pl.pallas_call(
        functools.partial(_odd_proj_kernel, qscale=qscale),
        grid=(b, t // tm),
        in_specs=in_specs,
        out_specs=specs,
        out_shape=shapes,
        input_output_aliases=aliases,
        compiler_params=_cparams("parallel", "parallel"),
        name="odd_proj",
    )(*args)


def _mix_odd_kernel(x_ref, mod_ref, do_ref, go_ref, lam_ref, dn_ref, wd_ref, wg_ref, out_ref, *, lambda_init):
    lp = lam_ref[...]
    lam = (jnp.exp(jnp.sum(lp[0:1, :] * lp[1:2, :], axis=-1, keepdims=True))
           - jnp.exp(jnp.sum(lp[2:3, :] * lp[3:4, :], axis=-1, keepdims=True)) + lambda_init)
    dn = dn_ref[...]
    parts = []
    for hd in range(DIFF_HEADS):
        d0 = do_ref[:, 2 * hd * LANES:(2 * hd + 1) * LANES]
        d1 = do_ref[:, (2 * hd + 1) * LANES:(2 * hd + 2) * LANES]
        dd = d0 - lam * d1
        ms = jnp.mean(dd * dd, axis=-1, keepdims=True)
        parts.append(((dd * lax.rsqrt(ms + NORM_EPS) * dn) * (1.0 - lambda_init)).astype(BF16))
    dcat = jnp.concatenate(parts, axis=1)
    mix = _dot(dcat, wd_ref[...]) + _dot(go_ref[...], wg_ref[...])
    out_ref[...] = x_ref[...] + mod_ref[2:3, :] * mix


def _mix_odd(x, mod, do, go, lam_p, dn, wd, wg, lambda_init):
    b, t, d = x.shape
    tm = TOKEN_TILE
    tok = lambda w: pl.BlockSpec((None, tm, w), lambda bi, i: (bi, i, 0))
    return pl.pallas_call(
        functools.partial(_mix_odd_kernel, lambda_init=lambda_init),
        grid=(b, t // tm),
        in_specs=[tok(d), pl.BlockSpec((None, 6, d), lambda bi, i: (bi, 0, 0)), tok(do.shape[-1]),
                  tok(go.shape[-1]), _full(lam_p), _full(dn), _full(wd), _full(wg)],
        out_specs=tok(d),
        out_shape=jax.ShapeDtypeStruct(x.shape, F32),
        compiler_params=_cparams("parallel", "parallel"),
        name="mix_odd",
    )(x, mod, do, go, lam_p, dn, wd, wg)


def _router_kernel(x_ref, mod_ref, g_ref, rw_ref, h_out, route_out):
    h = _rms_mod(x_ref[...], g_ref[...], mod_ref[4:5, :], mod_ref[3:4, :])
    h_out[...] = h
    h1, h2, h3 = _split3(h)
    w1, w2, w3 = _split3(rw_ref[...])
    logits = (_dot(h1, w1) + _dot(h1, w2) + _dot(h2, w1) + _dot(h2, w2) + _dot(h1, w3) + _dot(h3, w1))
    lane = lax.broadcasted_iota(jnp.int32, logits.shape, 1).astype(F32)
    neg = -jnp.inf
    l1 = jnp.where(lane < N_EXPERTS, logits, neg)
    m1 = jnp.max(l1, axis=-1, keepdims=True)
    i1 = jnp.min(jnp.where(l1 == m1, lane, float(LANES)), axis=-1, keepdims=True)
    l2 = jnp.where(lane == i1, neg, l1)
    m2 = jnp.max(l2, axis=-1, keepdims=True)
    i2 = jnp.min(jnp.where(l2 == m2, lane, float(LANES)), axis=-1, keepdims=True)
    e = jnp.exp(m2 - m1)
    p1 = 1.0 / (1.0 + e)
    p2 = e / (1.0 + e)
    route = jnp.where(lane == 0.0, i1,
                      jnp.where(lane == 1.0, i2, jnp.where(lane == 2.0, p1, jnp.where(lane == 3.0, p2, 0.0))))
    route_out[...] = route[:, 0:SUBLANES]


def _router(x, mod, g, rw):
    b, t, d = x.shape
    tm = TOKEN_TILE
    tok = lambda w: pl.BlockSpec((None, tm, w), lambda bi, i: (bi, i, 0))
    return pl.pallas_call(
        _router_kernel,
        grid=(b, t // tm),
        in_specs=[tok(d), pl.BlockSpec((None, 6, d), lambda bi, i: (bi, 0, 0)), _full(g), _full(rw)],
        out_specs=[tok(d), tok(SUBLANES)],
        out_shape=[jax.ShapeDtypeStruct(x.shape, F32), jax.ShapeDtypeStruct((b, t, SUBLANES), F32)],
        compiler_params=_cparams("parallel", "parallel"),
        name="moe_router",
    )(x, mod, g, rw)


def _row_copies(idx_ref, n, src_of, dst_of, sem):
    def body(r, carry):
        i = idx_ref[0, r]
        pltpu.make_async_copy(src_of(r, i), dst_of(r, i), sem).start()
        return carry
    lax.fori_loop(0, n, body, 0)


def _moe_kernel(be_ref, tok_ref, tok_next_ref, dst_ref, dst_prev_ref, h_hbm, wg_ref, wu_ref, wd_ref, y_hbm,
                xbuf, xb_sc, acc_sc, ybuf, gsem, ssem, *, nf):
    i, j = pl.program_id(0), pl.program_id(1)
    nb = pl.num_programs(0)
    blk = acc_sc.shape[0]
    per_step = blk // nf
    nxt = (i + 1) % 2
    gather_wait = lambda s: pltpu.make_async_copy(h_hbm.at[pl.ds(0, blk), :], xbuf.at[s], gsem.at[s]).wait()
    scatter_wait = lambda: pltpu.make_async_copy(ybuf, y_hbm.at[pl.ds(0, blk), :], ssem).wait()
    gather_row = lambda idx_ref, r, s: pltpu.make_async_copy(
        h_hbm.at[pl.ds(idx_ref[0, r], 1), :], xbuf.at[s, pl.ds(r, 1), :], gsem.at[s]).start()
    scatter_row = lambda idx_ref, r: pltpu.make_async_copy(
        ybuf.at[pl.ds(r, 1), :], y_hbm.at[pl.ds(idx_ref[0, r], 1), :], ssem).start()

    @pl.when(j == 0)
    def _():
        @pl.when(i == 0)
        def _():
            ybuf[...] = jnp.zeros(ybuf.shape, F32)
            lax.fori_loop(0, blk, lambda r, c: (gather_row(tok_ref, r, 0), c)[1], 0)

        gather_wait(i % 2)
        xb_sc[...] = xbuf[i % 2].astype(BF16)
        acc_sc[...] = jnp.zeros(acc_sc.shape, F32)

    for r in range(per_step):
        gather_row(tok_next_ref, j * per_step + r, nxt)
        scatter_row(dst_prev_ref, j * per_step + r)

    x = xb_sc[...]
    act = (_silu(_dot(x, wg_ref[...])) * _dot(x, wu_ref[...])).astype(BF16)
    acc_sc[...] += _dot(act, wd_ref[...])

    @pl.when(j == nf - 1)
    def _():
        scatter_wait()
        ybuf[...] = acc_sc[...]

        @pl.when(i == nb - 1)
        def _():
            gather_wait(nxt)
            lax.fori_loop(0, blk, lambda r, c: (scatter_row(dst_ref, r), c)[1], 0)
            scatter_wait()


def _moe_experts(h, buf_tok, dest, block_e, w_gu, w_down):
    nb, blk = buf_tok.shape
    d = h.shape[1]
    f = w_down.shape[1]
    tf = _ffn_tile(f)
    nf = f // tf
    idx3 = lambda a: a.reshape(nb, 1, blk)
    smem = lambda imap: pl.BlockSpec((None, 1, blk), imap, memory_space=pltpu.SMEM)
    grid_spec = pltpu.PrefetchScalarGridSpec(
        num_scalar_prefetch=1,
        grid=(nb, nf),
        in_specs=[smem(lambda i, j, be: (i, 0, 0)),
                  smem(lambda i, j, be: (jnp.minimum(i + 1, nb - 1), 0, 0)),
                  smem(lambda i, j, be: (i, 0, 0)),
                  smem(lambda i, j, be: (jnp.maximum(i - 1, 0), 0, 0)),
                  pl.BlockSpec(memory_space=pl.ANY),
                  pl.BlockSpec((None, d, tf), lambda i, j, be: (be[i], 0, j)),
                  pl.BlockSpec((None, d, tf), lambda i, j, be: (be[i], 0, nf + j)),
                  pl.BlockSpec((None, tf, d), lambda i, j, be: (be[i], j, 0))],
        out_specs=pl.BlockSpec(memory_space=pl.ANY),
        scratch_shapes=[pltpu.VMEM((2, blk, d), F32), pltpu.VMEM((blk, d), BF16), pltpu.VMEM((blk, d), F32),
                        pltpu.VMEM((blk, d), F32), pltpu.SemaphoreType.DMA((2,)), pltpu.SemaphoreType.DMA(())],
    )
    return pl.pallas_call(
        functools.partial(_moe_kernel, nf=nf),
        grid_spec=grid_spec,
        out_shape=jax.ShapeDtypeStruct((nb * blk, d), F32),
        compiler_params=_cparams("arbitrary", "arbitrary"),
        name="moe_experts",
    )(block_e, idx3(buf_tok), idx3(buf_tok), idx3(dest), idx3(dest), h, w_gu, w_gu, w_down)


def _final_kernel(x_ref, mod_ref, rt_ref, fn_ref, y0_ref, y1_ref, out_ref):
    rt = rt_ref[...]
    y = rt[:, 2:3] * y0_ref[...] + rt[:, 3:4] * y1_ref[...]
    xo = x_ref[...] + mod_ref[5:6, :] * y
    ms = jnp.mean(xo * xo, axis=-1, keepdims=True)
    out_ref[...] = xo * lax.rsqrt(ms + NORM_EPS) * fn_ref[...]


def _moe_combine_final(x, mod, route, y, fn):
    b, t, d = x.shape
    tm = TOKEN_TILE
    nt = t // tm
    tok = lambda w: pl.BlockSpec((None, tm, w), lambda bi, i: (bi, i, 0))
    choice = lambda c: pl.BlockSpec((tm, d), lambda bi, i: (c * b * nt + bi * nt + i, 0))
    return pl.pallas_call(
        _final_kernel,
        grid=(b, nt),
        in_specs=[tok(d), pl.BlockSpec((None, 6, d), lambda bi, i: (bi, 0, 0)), tok(route.shape[-1]),
                  _full(fn), choice(0), choice(1)],
        out_specs=tok(d),
        out_shape=jax.ShapeDtypeStruct(x.shape, F32),
        compiler_params=_cparams("parallel", "parallel"),
        name="moe_combine_final",
    )(x, mod, route, fn, y, y)


def _moe_plan(top_e, n_tok):
    nk = n_tok * TOP_K
    flat_e = top_e.reshape(-1)
    onehot = (flat_e[:, None] == jnp.arange(N_EXPERTS, dtype=jnp.int32)[None, :]).astype(jnp.int32)
    rank = jnp.sum((jnp.cumsum(onehot, axis=0) - onehot) * onehot, axis=1)
    counts = jnp.sum(onehot, axis=0)
    padded = (counts + MOE_BLOCK - 1) // MOE_BLOCK * MOE_BLOCK
    pad_end = jnp.cumsum(padded)
    pad_start = pad_end - padded
    slot = (pad_start[flat_e] + rank).astype(jnp.int32)
    n_blocks = -(-nk // MOE_BLOCK) + N_EXPERTS
    rows = n_blocks * MOE_BLOCK
    assign = jnp.full((rows,), -1, jnp.int32).at[slot].set(jnp.arange(nk, dtype=jnp.int32))
    is_pad = assign < 0
    pad_rank = jnp.cumsum(is_pad.astype(jnp.int32)) - 1
    buf_tok = jnp.where(is_pad, 0, assign // TOP_K)
    dest = jnp.where(is_pad, nk + pad_rank, (assign % TOP_K) * n_tok + assign // TOP_K)
    block_e = jnp.minimum(
        jnp.searchsorted(pad_end, jnp.arange(n_blocks, dtype=jnp.int32) * MOE_BLOCK, side="right"),
        N_EXPERTS - 1).astype(jnp.int32)
    return buf_tok.reshape(n_blocks, MOE_BLOCK), dest.reshape(n_blocks, MOE_BLOCK), block_e


def _prep_even(w_in, w_uq, w_ukv, w_out):
    src, _, _, _ = _mla_layout()
    o = np.cumsum((0, MLA_Q_RANK, MLA_KV_RANK, MLA_ROPE, GDN_QKV, GDN_HEADS * GDN_DV, 2 * GDN_HEADS,
                   2 * GDN_HEADS))
    wcq, wckv = w_in[:, o[0]:o[1]], w_in[:, o[1]:o[2]]
    wkpe = _take_cols(w_in[:, o[2]:o[3]], np.where(src >= MLA_NOPE, src - MLA_NOPE, -1))
    wgqkv, wgz, wab = w_in[:, o[3]:o[4]], w_in[:, o[4]:o[5]], w_in[:, o[5]:o[7]]
    dq = MLA_NOPE + MLA_ROPE
    uq_idx = np.concatenate([np.where(src >= 0, h * dq + src, -1) for h in range(MLA_HEADS)])
    dkv = MLA_NOPE + MLA_V
    uk_idx = np.concatenate([np.where((src >= 0) & (src < MLA_NOPE), h * dkv + src, -1)
                             for h in range(MLA_HEADS)])
    uv_idx = np.concatenate([h * dkv + MLA_NOPE + np.arange(MLA_V) for h in range(MLA_HEADS)])
    wuq, wuk, wuv = _take_cols(w_uq, uq_idx), _take_cols(w_ukv, uk_idx), _take_cols(w_ukv, uv_idx)
    rows = -np.ones(MLA_HEADS * LANES, np.int64)
    for h in range(MLA_HEADS):
        base = h * LANES + (h % 2) * MLA_V
        rows[base:base + MLA_V] = h * MLA_V + np.arange(MLA_V)
    n_mla = MLA_HEADS * MLA_V
    wa = _take_cols(w_out[:n_mla].T, rows).T
    wg = w_out[n_mla:]
    ws = [w.astype(BF16) for w in (wcq, wckv, wkpe, wgqkv, wgz, wab)]
    return ws, [wuq.astype(BF16), wuk.astype(BF16), wuv.T.astype(BF16)], wa.astype(BF16), wg.astype(BF16)


def _prep_odd(w_in, q_norm, k_norm, w_out):
    m, src, _, _, _ = _pair_layout()
    nd = DIFF_HEADS * 2 * DIFF_DH
    o_dq, o_dk, o_dv, o_gq = 0, nd, 2 * nd, 3 * nd
    o_gk = o_gq + GQA_HEADS * GQA_DH
    o_gv = o_gk + GQA_KV_HEADS * GQA_DH
    pair = np.concatenate([h * 2 * DIFF_DH + m * DIFF_DH + src for h in range(DIFF_HEADS)])
    wdq = _take_cols(w_in, o_dq + pair)
    wdk = _take_cols(w_in, o_dk + pair)
    wdv = w_in[:, o_dv:o_gq].T
    grp = GQA_HEADS // GQA_KV_HEADS
    gq_idx = np.concatenate([np.where(m == h // grp, o_gq + h * GQA_DH + src, -1) for h in range(GQA_HEADS)])
    wgq = _take_cols(w_in, gq_idx)
    wgk = _take_cols(w_in, o_gk + m * GQA_DH + src)
    wgv = w_in[:, o_gv:o_gv + GQA_KV_HEADS * GQA_DH].T
    gqg = jnp.tile(q_norm[src], GQA_HEADS).reshape(1, -1)
    gkg = k_norm[src].reshape(1, -1)
    rows = -np.ones(GQA_HEADS * LANES, np.int64)
    for h in range(GQA_HEADS):
        base = h * LANES + (h // grp) * GQA_DH
        rows[base:base + GQA_DH] = h * GQA_DH + np.arange(GQA_DH)
    wd = w_out[:nd]
    wg = _take_cols(w_out[nd:].T, rows).T
    ws = [w.astype(BF16) for w in (wdq, wdk, wdv, wgq, wgk, wgv)]
    return ws, [gqg.astype(F32), gkg.astype(F32)], wd.astype(BF16), wg.astype(BF16)


def _layer_mods(m_layer, batch):
    d = m_layer.shape[1] // 6
    lat = m_layer[:batch].reshape(batch, 6, d)
    ctx = jnp.broadcast_to(m_layer[batch].reshape(1, 6, d), (batch, 6, d))
    return lat, ctx


def kernel(x, c, ctx, c_ctx, mod_w, mod_b, norm_g, ev_w_in, ev_mla_q_norm, ev_mla_kv_norm, ev_mla_w_uq,
           ev_mla_w_ukv, ev_gdn_conv, ev_gdn_a_log, ev_gdn_dt_bias, ev_gdn_out_norm, ev_w_out, ev_ffn_w_gu,
           ev_ffn_w_down, od_w_in, od_diff_lambda, od_diff_norm, od_gqa_q_norm, od_gqa_k_norm, od_w_out,
           od_router_w, od_moe_w_gu, od_moe_w_down, final_norm):
    batch, t_lat, d = x.shape
    t_ctx = ctx.shape[1]
    depth = mod_w.shape[0]
    assert depth == 2 and batch < 16

    cs = jnp.zeros((16, d), F32).at[:batch].set(c).at[batch].set(c_ctx)
    mods = _modulation(cs, mod_w, mod_b)

    j = 0
    mod_lat, mod_ctx = _layer_mods(mods[0], batch)
    g1 = norm_g[0, 0].reshape(1, d)
    g2 = norm_g[0, 1].reshape(1, d)
    ws, ups, wa, wg = _prep_even(ev_w_in[j], ev_mla_w_uq[j], ev_mla_w_ukv[j], ev_w_out[j])
    ws = ws + [ev_mla_q_norm[j].reshape(1, -1), ev_mla_kv_norm[j].reshape(1, -1)] + ups
    _, kind, freq, sign = _mla_layout()
    qscale = (MLA_NOPE + MLA_ROPE) ** -0.5 * LOG2E
    nfq = MLA_ROPE // 4
    tabs_lat = (_rope_tables(t_lat, kind, freq, sign, nfq, qscale, True)
                + _rope_tables(t_lat, kind, freq, sign, nfq, 1.0, True))
    tabs_ctx = (_rope_tables(t_ctx, kind, freq, sign, nfq, qscale, False)
                + _rope_tables(t_ctx, kind, freq, sign, nfq, 1.0, False))
    t_all = t_ctx + t_lat
    ctx_tiles = t_ctx // TOKEN_TILE
    qc, k_all, v_all, gqkv_c, gz_c, ab_c = _even_proj(ctx, mod_ctx, g1, tabs_ctx, ws, t_all, 0)
    ql, k_all, v_all, gqkv_l, gz_l, ab_l = _even_proj(x, mod_lat, g1, tabs_lat, ws, t_all, ctx_tiles,
                                                      bufs=(k_all, v_all))

    same = lambda hg: hg
    mla_kw = dict(groups=MLA_HEADS, g=1, kmap=same, vmap=lambda hg: hg // 2, out_dtype=BF16)
    ao_l = _flash(ql, k_all, v_all, **mla_kw)
    ao_c = _flash(qc, k_all, v_all, n_keys=t_ctx, **mla_kw)

    nh2 = 2 * GDN_HEADS
    prm = jnp.zeros((2, 2 * nh2), F32)
    prm = prm.at[0, :nh2].set(ev_gdn_a_log[j].reshape(-1)).at[1, :nh2].set(ev_gdn_dt_bias[j].reshape(-1))
    eseg = _seg_ones(GDN_HEADS * GDN_DK, GDN_DK)
    prep = _gdn_prep(gqkv_c, ab_c, ev_gdn_conv[j], prm, eseg, t_all, 0)
    prep = _gdn_prep(gqkv_l, ab_l, ev_gdn_conv[j], prm, eseg, t_all, ctx_tiles, bufs=prep)
    go_f, go_b = _gdn_scan(prep, t_ctx)

    on = jnp.tile(ev_gdn_out_norm[j], GDN_HEADS).reshape(1, -1)
    x = _mix_even(x, mod_lat, ao_l, go_f, go_b, ctx_tiles, gz_l, on, eseg, wa, wg)
    ctx = _mix_even(ctx, mod_ctx, ao_c, go_f, go_b, 0, gz_c, on, eseg, wa, wg)
    w_gu, w_dn = ev_ffn_w_gu[j].astype(BF16), ev_ffn_w_down[j].astype(BF16)
    x = _ffn(x, mod_lat, g2, w_gu, w_dn)
    ctx = _ffn(ctx, mod_ctx, g2, w_gu, w_dn)

    lambda_init = 0.8 - 0.6 * math.exp(-0.3 * 1)
    mod_lat, mod_ctx = _layer_mods(mods[1], batch)
    g1 = norm_g[1, 0].reshape(1, d)
    g2 = norm_g[1, 1].reshape(1, d)
    ws, gains, wd, wgx = _prep_odd(od_w_in[j], od_gqa_q_norm[j], od_gqa_k_norm[j], od_w_out[j])
    _, _, kind, freq, sign = _pair_layout()
    nfq = DIFF_DH // 4
    tabs_lat = _rope_tables(t_lat, kind, freq, sign, nfq, 1.0, True)
    tabs_ctx = _rope_tables(t_ctx, kind, freq, sign, nfq, 1.0, False)
    qscale = DIFF_DH ** -0.5 * LOG2E
    _, dk, dvt, _, gk, gvt = _odd_proj(ctx, mod_ctx, g1, tabs_ctx, ws, gains, qscale, t_all, 0)
    dq, dk, dvt, gq, gk, gvt = _odd_proj(x, mod_lat, g1, tabs_lat, ws, gains, qscale, t_all, ctx_tiles,
                                         bufs=(dk, dvt, gk, gvt))
    do = _flash(dq, dk, dvt, groups=DIFF_HEADS, g=2, kmap=same, vmap=same, out_dtype=F32)
    zero = lambda hg: 0
    go = _flash(gq, gk, gvt, groups=1, g=GQA_HEADS, kmap=zero, vmap=zero, out_dtype=BF16)
    x = _mix_odd(x, mod_lat, do, go, od_diff_lambda[j], od_diff_norm[j].reshape(1, -1), wd, wgx, lambda_init)

    rw = jnp.zeros((d, LANES), F32).at[:, :N_EXPERTS].set(od_router_w[j])
    h2, route = _router(x, mod_lat, g2, rw)
    n_tok = batch * t_lat
    top_e = route[..., 0:TOP_K].astype(jnp.int32).reshape(n_tok, TOP_K)
    buf_tok, dest, block_e = _moe_plan(top_e, n_tok)
    y = _moe_experts(h2.reshape(n_tok, d), buf_tok, dest, block_e, od_moe_w_gu[j].astype(BF16),
                     od_moe_w_down[j].astype(BF16))
    return _moe_combine_final(x, mod_lat, route, y, final_norm.reshape(1, d))
```

```python
import functools
import math

import numpy as np
import jax
import jax.numpy as jnp
from jax import lax
from jax.experimental import pallas as pl
from jax.experimental.pallas import tpu as pltpu

F32 = jnp.float32
BF16 = jnp.bfloat16

GRID_W = 64
ROPE_BASE = 10000.0
NORM_EPS = 1e-6
MLA_HEADS, MLA_NOPE, MLA_ROPE, MLA_V = 8, 64, 32, 64
MLA_Q_RANK, MLA_KV_RANK = 384, 256
GDN_HEADS, GDN_DK, GDN_DV, GDN_CONV, GDN_CHUNK = 8, 64, 64, 5, 64
GDN_QKV = GDN_HEADS * (2 * GDN_DK + GDN_DV)
DIFF_HEADS, DIFF_DH = 4, 64
GQA_HEADS, GQA_KV_HEADS, GQA_DH = 8, 2, 64
N_EXPERTS, TOP_K, MOE_BLOCK = 8, 2, 512

LANES = 128
SUBLANES = 8
VMEM_LIMIT = 48 * 1024 * 1024
LOG2E = 1.4426950408889634

TOKEN_TILE = 256
ATTN_ROWS = 1024
ATTN_KEYS = 256
ATTN_TILES = 2
ATTN_UNROLL = 4


def _cparams(*sem):
    return pltpu.CompilerParams(dimension_semantics=sem, vmem_limit_bytes=VMEM_LIMIT)


def _dot(a, b):
    return jnp.dot(a, b, preferred_element_type=F32)


def _dot_t(a, b):
    return lax.dot_general(a, b, (((1,), (1,)), ((), ())), preferred_element_type=F32)


def _split3(x):
    h1 = x.astype(BF16)
    r1 = x - h1.astype(F32)
    h2 = r1.astype(BF16)
    h3 = (r1 - h2.astype(F32)).astype(BF16)
    return h1, h2, h3


def _silu(x):
    return x * jax.nn.sigmoid(x)


def _rms_mod(x, g, sc, sh):
    ms = jnp.mean(x * x, axis=-1, keepdims=True)
    return (x * lax.rsqrt(ms + NORM_EPS) * g) * (1.0 + sc) + sh


def _seg_sumsq(x, eseg):
    sq = x * x
    hi = sq.astype(BF16)
    lo = (sq - hi.astype(F32)).astype(BF16)
    return _dot(hi, eseg) + _dot(lo, eseg)


def _rope(x, c, s):
    return x * c + pltpu.roll(x, LANES // 2, 1) * s


def _mla_layout():
    src = -np.ones(LANES, np.int64)
    kind = np.zeros(LANES, np.int64)
    freq = np.zeros(LANES, np.int64)
    sign = np.zeros(LANES, np.float32)
    nf = MLA_ROPE // 4
    for f in range(nf):
        src[f] = MLA_NOPE + f; kind[f] = 1; freq[f] = f; sign[f] = -1.0
        src[nf + f] = MLA_NOPE + 2 * nf + f; kind[nf + f] = 2; freq[nf + f] = f; sign[nf + f] = -1.0
        src[64 + f] = MLA_NOPE + nf + f; kind[64 + f] = 1; freq[64 + f] = f; sign[64 + f] = 1.0
        src[64 + nf + f] = MLA_NOPE + 3 * nf + f; kind[64 + nf + f] = 2; freq[64 + nf + f] = f
        sign[64 + nf + f] = 1.0
    src[16:64] = np.arange(0, 48)
    src[80:96] = np.arange(48, 64)
    return src, kind, freq, sign


def _pair_layout():
    lane = np.arange(LANES)
    region = lane // 32
    m = region % 2
    is_b = region // 2
    within = lane % 32
    is_col = within // 16
    f = within % 16
    src = is_col * 32 + is_b * 16 + f
    kind = 1 + is_col
    sign = np.where(is_b == 0, -1.0, 1.0).astype(np.float32)
    return m, src, kind, f, sign


def _rope_tables(t_len, kind, freq, sign, n_freq, scale, with_pos):
    kind_j = jnp.asarray(kind)
    if not with_pos:
        c = jnp.full((t_len, LANES), scale, F32)
        return c, jnp.zeros((t_len, LANES), F32)
    n_rows = t_len // GRID_W
    inv = 1.0 / (ROPE_BASE ** (jnp.arange(n_freq, dtype=F32) / n_freq))
    rows = jnp.repeat(jnp.arange(n_rows, dtype=F32), GRID_W)
    cols = jnp.tile(jnp.arange(GRID_W, dtype=F32), n_rows)
    ang_r = (rows[:, None] * inv)[:, freq]
    ang_c = (cols[:, None] * inv)[:, freq]
    ang = jnp.where(kind_j[None, :] == 1, ang_r, ang_c)
    has = (kind_j > 0)[None, :]
    c = jnp.where(has, jnp.cos(ang), 1.0) * scale
    s = jnp.where(has, jnp.sin(ang) * jnp.asarray(sign)[None, :], 0.0) * scale
    return c.astype(F32), s.astype(F32)


def _take_cols(w, idx):
    idx = np.asarray(idx)
    wz = jnp.concatenate([w, jnp.zeros((w.shape[0], 1), w.dtype)], axis=1)
    return wz[:, np.where(idx < 0, w.shape[1], idx)]


def _seg_ones(width, seg):
    r = np.arange(width)
    return jnp.asarray((r[:, None] // seg) == (r[None, :] // seg), dtype=BF16)


def _mod_kernel(c_ref, w_ref, b_ref, o_ref):
    c = c_ref[...]
    s1, s2, s3 = _split3(_silu(c))
    w1, w2, _ = _split3(w_ref[...])
    acc = _dot(s1, w1) + _dot(s1, w2) + _dot(s2, w1) + _dot(s2, w2) + _dot(s3, w1)
    o_ref[...] = acc + b_ref[...]


def _modulation(cs, mod_w, mod_b):
    depth, d, n = mod_w.shape
    tn = 512
    return pl.pallas_call(
        _mod_kernel,
        grid=(depth, n // tn),
        in_specs=[
            pl.BlockSpec(cs.shape, lambda l, j: (0, 0)),
            pl.BlockSpec((None, d, tn), lambda l, j: (l, 0, j)),
            pl.BlockSpec((None, 1, tn), lambda l, j: (l, 0, j)),
        ],
        out_specs=pl.BlockSpec((None, cs.shape[0], tn), lambda l, j: (l, 0, j)),
        out_shape=jax.ShapeDtypeStruct((depth, cs.shape[0], n), F32),
        compiler_params=_cparams("parallel", "parallel"),
        name="modulation",
    )(cs, mod_w, mod_b.reshape(depth, 1, n))


def _even_proj_kernel(x_ref, mod_ref, g_ref, cq_t, sq_t, ck_t, sk_t,
                      wcq, wckv, wkpe, wgqkv, wgz, wab, qn, kvn, wuq, wuk, wuv, *rest):
    q_out, k_out, v_out, gqkv_out, gz_out, ab_out = rest[-6:]
    h = _rms_mod(x_ref[...], g_ref[...], mod_ref[1:2, :], mod_ref[0:1, :]).astype(BF16)
    cq = _dot(h, wcq[...])
    ckv = _dot(h, wckv[...])
    kpe = _dot(h, wkpe[...])
    nq = (cq * lax.rsqrt(jnp.mean(cq * cq, axis=-1, keepdims=True) + NORM_EPS) * qn[...]).astype(BF16)
    nkv = (ckv * lax.rsqrt(jnp.mean(ckv * ckv, axis=-1, keepdims=True) + NORM_EPS) * kvn[...]).astype(BF16)
    q = _dot(nq, wuq[...])
    kn = _dot(nkv, wuk[...])
    kper = _rope(kpe, ck_t[...], sk_t[...])
    cqv, sqv = cq_t[...], sq_t[...]
    for hd in range(MLA_HEADS):
        sl = slice(hd * LANES, (hd + 1) * LANES)
        q_out[:, sl] = _rope(q[:, sl], cqv, sqv).astype(BF16)
        k_out[:, sl] = (kn[:, sl] + kper).astype(BF16)
    _store_vt(v_out, _dot_t(wuv[...], nkv))
    gqkv_out[...] = _dot(h, wgqkv[...])
    gz_out[...] = _dot(h, wgz[...])
    ab_out[...] = _dot(h, wab[...])


def _full(a):
    nd = a.ndim
    return pl.BlockSpec(a.shape, lambda *_: (0,) * nd)


def _alias_shared(args, in_specs, shapes, shared, bufs):
    shared = list(shared)
    if bufs is None:
        bufs = [jnp.zeros(shapes[o].shape, shapes[o].dtype) for o in shared]
    aliases = {len(args) + n: o for n, o in enumerate(shared)}
    return args + list(bufs), in_specs + [pl.BlockSpec(memory_space=pl.ANY)] * len(shared), aliases


def _store_vt(vt_out, vt):
    for n in range(vt_out.shape[0]):
        vt_out[n] = vt[n * LANES:(n + 1) * LANES, :].astype(vt_out.dtype)


def _proj_outputs(b, t, tt, tile_off, outs):
    tm = TOKEN_TILE
    assert tm == ATTN_KEYS
    specs, shapes = [], []
    for w, dt, kind in outs:
        if kind == "own":
            specs.append(pl.BlockSpec((None, tm, w), lambda bi, i: (bi, i, 0)))
            shapes.append(jax.ShapeDtypeStruct((b, t, w), dt))
        elif kind == "keys":
            specs.append(pl.BlockSpec((None, tm, w), lambda bi, i: (bi, i + tile_off, 0)))
            shapes.append(jax.ShapeDtypeStruct((b, tt, w), dt))
        else:
            specs.append(pl.BlockSpec((None, kind, None, LANES, tm), lambda bi, i: (bi, 0, i + tile_off, 0, 0)))
            shapes.append(jax.ShapeDtypeStruct((b, kind, tt // tm, LANES, tm), dt))
    shared = [n for n, (_, _, kind) in enumerate(outs) if kind != "own"]
    return specs, shapes, shared


def _even_proj(x, mod, g, tabs, ws, tt, tile_off, bufs=None):
    b, t, d = x.shape
    tm = TOKEN_TILE
    tok = lambda w: pl.BlockSpec((None, tm, w), lambda bi, i: (bi, i, 0))
    tab = pl.BlockSpec((tm, LANES), lambda bi, i: (i, 0))
    outs = ((MLA_HEADS * LANES, BF16, "own"), (MLA_HEADS * LANES, BF16, "keys"),
            (0, BF16, MLA_HEADS * MLA_V // LANES), (GDN_QKV, F32, "own"), (GDN_HEADS * GDN_DV, F32, "own"),
            (4 * GDN_HEADS, F32, "own"))
    specs, shapes, shared = _proj_outputs(b, t, tt, tile_off, outs)
    args = [x, mod, g, *tabs, *ws]
    in_specs = [tok(d), pl.BlockSpec((None, 6, d), lambda bi, i: (bi, 0, 0)), _full(g),
                tab, tab, tab, tab] + [_full(w) for w in ws]
    args, in_specs, aliases = _alias_shared(args, in_specs, shapes, shared, bufs)
    return pl.pallas_call(
        _even_proj_kernel,
        grid=(b, t // tm),
        in_specs=in_specs,
        out_specs=specs,
        out_shape=shapes,
        input_output_aliases=aliases,
        compiler_params=_cparams("parallel", "parallel"),
        name="even_proj",
    )(*args)


def _flash_kernel(q_ref, k_ref, vt_ref, o_ref, acc_sc, s_a, s_b, *, g, tq, tk, nsub):
    q = jnp.concatenate([q_ref[:, i * LANES:(i + 1) * LANES] for i in range(g)], axis=0)
    rows = g * tq
    acc_sc[...] = jnp.zeros(acc_sc.shape, F32)
    nt = ATTN_TILES

    def scores(t0, n):
        return _dot_t(k_ref[pl.ds(pl.multiple_of(t0 * tk, tk), n * tk), :], q)

    def update(st, t0, n, m):
        m_new = jnp.maximum(m, jnp.max(st, axis=0, keepdims=True))
        p = jnp.exp2(st - m_new).astype(BF16)
        vt = jnp.concatenate([vt_ref[t0 + u] for u in range(n)] , axis=1)
        lhs = jnp.concatenate([vt, jnp.ones((2 * SUBLANES, n * tk), BF16)], axis=0)
        acc_sc[...] = jnp.exp2(m - m_new) * acc_sc[...] + _dot(lhs, p)
        return m_new

    bufs = (s_a, s_b)
    n_full, rem = divmod(nsub, nt)
    loop_units = ((n_full - 1) // ATTN_UNROLL) * ATTN_UNROLL if n_full else 0
    tail = [(u * nt, nt) for u in range(loop_units, n_full)] + ([(n_full * nt, rem)] if rem else [])
    first = (0, nt) if n_full else tail[0]
    s_a[0:first[1] * tk, :] = scores(*first)

    def body(i, m):
        for c in range(ATTN_UNROLL):
            u = i * ATTN_UNROLL + c
            bufs[(c + 1) % 2][...] = scores((u + 1) * nt, nt)
            m = update(bufs[c % 2][...], u * nt, nt, m)
        return m

    m = lax.fori_loop(0, loop_units // ATTN_UNROLL, body, jnp.full((1, rows), -jnp.inf, F32))
    for c, (t0, n) in enumerate(tail):
        if c + 1 < len(tail):
            t1, n1 = tail[c + 1]
            bufs[(c + 1) % 2][0:n1 * tk, :] = scores(t1, n1)
        m = update(bufs[c % 2][0:n * tk, :], t0, n, m)
    on = acc_sc[0:LANES, :] * (1.0 / acc_sc[LANES:LANES + 1, :])
    o = on.T
    for i in range(g):
        o_ref[:, i * LANES:(i + 1) * LANES] = o[i * tq:(i + 1) * tq, :].astype(o_ref.dtype)


def _flash(q, k, vt, *, groups, g, kmap, vmap, out_dtype, n_keys=None):
    b, tq_total, _ = q.shape
    tk_total = k.shape[1] if n_keys is None else n_keys
    tq = min(ATTN_ROWS // g, tq_total)
    tk = ATTN_KEYS
    nsub = tk_total // tk
    assert vt.shape[3:] == (LANES, tk) and vt.shape[2] >= nsub
    rows = g * tq
    kern = functools.partial(_flash_kernel, g=g, tq=tq, tk=tk, nsub=nsub)
    return pl.pallas_call(
        kern,
        grid=(b, groups, tq_total // tq),
        in_specs=[
            pl.BlockSpec((None, tq, g * LANES), lambda bi, hg, i: (bi, i, hg)),
            pl.BlockSpec((None, tk_total, LANES), lambda bi, hg, i: (bi, 0, kmap(hg))),
            pl.BlockSpec((None, None, nsub, LANES, tk), lambda bi, hg, i: (bi, vmap(hg), 0, 0, 0)),
        ],
        out_specs=pl.BlockSpec((None, tq, g * LANES), lambda bi, hg, i: (bi, i, hg)),
        out_shape=jax.ShapeDtypeStruct(q.shape, out_dtype),
        scratch_shapes=[pltpu.VMEM((LANES + 2 * SUBLANES, rows), F32), pltpu.VMEM((ATTN_TILES * tk, rows), F32),
                        pltpu.VMEM((ATTN_TILES * tk, rows), F32)],
        compiler_params=_cparams("parallel", "parallel", "parallel"),
        name="flash_attention",
    )(q, k, vt)


def _gdn_prep_kernel(x_ref, prev_ref, next_ref, ab_ref, cw_ref, prm_ref, eseg_ref,
                     lf_ref, lb_ref, la_ref, *rest):
    q_out, k_out, v_out, s1_out, s2_out, xe_sc = rest[-6:]
    i = pl.program_id(1)
    tm = x_ref.shape[0]
    halo = SUBLANES
    xe_sc[0:halo, :] = prev_ref[...] * (i > 0).astype(F32)
    xe_sc[halo:halo + tm, :] = x_ref[...]
    xe_sc[halo + tm:2 * halo + tm, :] = next_ref[...] * (i < pl.num_programs(1) - 1).astype(F32)
    acc = jnp.zeros(x_ref.shape, F32)
    for j in range(GDN_CONV):
        acc = acc + cw_ref[j:j + 1, :] * xe_sc[pl.ds(halo - GDN_CONV // 2 + j, tm), :]
    y = _silu(acc)
    hw = GDN_HEADS * GDN_DK
    eseg = eseg_ref[...]
    q = y[:, 0:hw]
    k = y[:, hw:2 * hw]
    qn = q * lax.rsqrt(_seg_sumsq(q, eseg) + NORM_EPS) * (GDN_DK ** -0.5)
    kn = k * lax.rsqrt(_seg_sumsq(k, eseg) + NORM_EPS)
    v = y[:, 2 * hw:]
    for hd in range(GDN_HEADS):
        q_out[hd] = qn[:, hd * GDN_DK:(hd + 1) * GDN_DK]
        k_out[hd] = kn[:, hd * GDN_DK:(hd + 1) * GDN_DK]
        v_out[hd] = v[:, hd * GDN_DV:(hd + 1) * GDN_DV]

    ab = ab_ref[...]
    nh2 = 2 * GDN_HEADS
    lane = lax.broadcasted_iota(jnp.int32, ab.shape, 1)
    z = ab + prm_ref[1:2, :]
    softplus = jnp.maximum(z, 0.0) + jnp.log1p(jnp.exp(-jnp.abs(z)))
    gate = jnp.where(lane < nh2, -jnp.exp(prm_ref[0:1, :]) * softplus, 0.0)
    g1, g2, g3 = _split3(gate)
    lf, lb, la = lf_ref[...], lb_ref[...], la_ref[...]
    cum_f = _dot(lf, g1) + _dot(lf, g2) + _dot(lf, g3)
    cum_b = _dot(lb, g1) + _dot(lb, g2) + _dot(lb, g3)
    tot = _dot(la, g1) + _dot(la, g2) + _dot(la, g3)
    cum = jnp.where(lane < GDN_HEADS, cum_f, cum_b)
    s1_out[...] = jnp.where(lane < nh2, cum, jax.nn.sigmoid(ab))
    s2_out[...] = tot


def _gdn_prep(gqkv, ab, conv_w, prm, eseg, tt, tile_off, bufs=None):
    b, t, w = gqkv.shape
    tm = TOKEN_TILE
    nt8 = t // SUBLANES
    per = tm // SUBLANES
    r = np.arange(tm)
    same = (r[:, None] // GDN_CHUNK) == (r[None, :] // GDN_CHUNK)
    lf = jnp.asarray(same & (r[None, :] <= r[:, None]), dtype=BF16)
    lb = jnp.asarray(same & (r[None, :] >= r[:, None]), dtype=BF16)
    la = jnp.asarray(same, dtype=BF16)
    halo = gqkv.reshape(b, nt8, SUBLANES, w)
    tok = lambda wd: pl.BlockSpec((None, tm, wd), lambda bi, i: (bi, i, 0))
    hw = GDN_HEADS * GDN_DK
    nab = ab.shape[-1]
    head = lambda wd: pl.BlockSpec((None, GDN_HEADS, tm, wd), lambda bi, i: (bi, 0, i + tile_off, 0))
    sca = pl.BlockSpec((None, tm, nab), lambda bi, i: (bi, i + tile_off, 0))
    in_specs = [
        tok(w),
        pl.BlockSpec((None, None, SUBLANES, w), lambda bi, i: (bi, jnp.maximum(i * per - 1, 0), 0, 0)),
        pl.BlockSpec((None, None, SUBLANES, w), lambda bi, i: (bi, jnp.minimum((i + 1) * per, nt8 - 1), 0, 0)),
        tok(nab), _full(conv_w), _full(prm), _full(eseg), _full(lf), _full(lb), _full(la),
    ]
    args = [gqkv, halo, halo, ab, conv_w, prm, eseg, lf, lb, la]
    shapes = ([jax.ShapeDtypeStruct((b, GDN_HEADS, tt, GDN_DK), F32)] * 2
              + [jax.ShapeDtypeStruct((b, GDN_HEADS, tt, GDN_DV), F32)]
              + [jax.ShapeDtypeStruct((b, tt, nab), F32)] * 2)
    args, in_specs, aliases = _alias_shared(args, in_specs, shapes, range(len(shapes)), bufs)
    return pl.pallas_call(
        _gdn_prep_kernel,
        grid=(b, t // tm),
        in_specs=in_specs,
        out_specs=[head(GDN_DK), head(GDN_DK), head(GDN_DV), sca, sca],
        out_shape=shapes,
        input_output_aliases=aliases,
        scratch_shapes=[pltpu.VMEM((tm + 2 * SUBLANES, w), F32)],
        compiler_params=_cparams("parallel", "parallel"),
        name="gdn_prep",
    )(*args)


def _bmm(a, b):
    return jnp.einsum("nij,njk->nik", a, b, preferred_element_type=F32)


def _bmm_t(a, b):
    return jnp.einsum("nik,njk->nij", a, b, preferred_element_type=F32)


def _gdn_intra_kernel(q_ref, k_ref, v_ref, gr_ref, a_out, b_out, qe_out, o0_out, *, hb, cb):
    c_len = GDN_CHUNK
    n = hb * cb
    sgn = 1 - 2 * pl.program_id(0)
    row = lax.broadcasted_iota(jnp.int32, (c_len, c_len), 0)
    col = lax.broadcasted_iota(jnp.int32, (c_len, c_len), 1)
    ahead = (row - col) * sgn
    eye_f = (row == col).astype(F32)
    eye_b = jnp.broadcast_to(eye_f.astype(BF16), (n, c_len, c_len))
    q = q_ref[...].reshape(n, c_len, GDN_DK)
    k = k_ref[...].reshape(n, c_len, GDN_DK)
    v = v_ref[...].reshape(n, c_len, GDN_DV)
    g_cols = gr_ref[:, :, 0:1, :].reshape(n, 1, c_len)
    beta_row = gr_ref[:, :, 1:2, :].reshape(n, 1, c_len)
    gl = gr_ref[:, :, 2:3, :].reshape(n, 1, c_len)

    ones_b = jnp.broadcast_to(jnp.ones((c_len, c_len), BF16), (n, c_len, c_len))

    def on_sublanes(r, pieces):
        parts = _split3(eye_f * r)[:pieces]
        return functools.reduce(lambda a, b: a + b, [_bmm_t(p, ones_b) for p in parts])

    gc = on_sublanes(g_cols, 2)
    beta = on_sublanes(beta_row, 1)
    eg, ek, cd = jnp.exp(gc), jnp.exp(gl - gc), jnp.exp(gl)
    decay = jnp.where(ahead >= 0, jnp.exp(jnp.minimum(gc - g_cols, 0.0)), 0.0)
    kb = k * beta
    kbf = k.astype(BF16)
    low = jnp.where(ahead > 0, _bmm_t(kb.astype(BF16), kbf) * decay, 0.0)
    qk = _bmm_t(q.astype(BF16), kbf) * decay
    mpow = jnp.where((row // 8) == (col // 8), -low, 0.0)
    tinv = eye_f + mpow
    for _ in range(2):
        mb = mpow.astype(BF16)
        mpow = _bmm(mb, mb)
        tinv = tinv + _bmm(tinv.astype(BF16), mpow.astype(BF16))
    for s in (8, 16, 32):
        off = ((row // (2 * s)) == (col // (2 * s))) & ((row // s) != (col // s))
        tb = tinv.astype(BF16)
        tinv = tinv - _bmm(_bmm(tb, jnp.where(off, low, 0.0).astype(BF16)).astype(BF16), tb)
    tb = tinv.astype(BF16)
    wu = _bmm(tb, jnp.concatenate([kb * eg, v * beta], axis=-1).astype(BF16)).astype(BF16)
    kdt = _bmm_t(eye_b, (k * ek).astype(BF16)).astype(BF16)
    ab = _bmm(kdt, wu)
    qo = _bmm(qk.astype(BF16), wu)
    a_out[...] = (cd * eye_f - ab[:, :, 0:GDN_DK]).reshape(a_out.shape)
    b_out[...] = ab[:, :, GDN_DK:].reshape(b_out.shape)
    qe_out[...] = (q * eg - qo[:, :, 0:GDN_DK]).reshape(qe_out.shape)
    o0_out[...] = qo[:, :, GDN_DK:].reshape(o0_out.shape)


def _gdn_intra(q, k, v, gr):
    b, h, tt, dk = q.shape
    assert dk == GDN_CHUNK and v.shape[-1] == GDN_CHUNK
    nc = tt // GDN_CHUNK
    hb, cb = h, 4
    rows = cb * GDN_CHUNK
    qkv = pl.BlockSpec((None, hb, rows, dk), lambda d, bi, c: (bi, 0, c, 0))
    tokb = pl.BlockSpec((None, None, hb, rows, dk), lambda d, bi, c: (d, bi, 0, c, 0))
    matb = pl.BlockSpec((None, None, hb, cb, dk, dk), lambda d, bi, c: (d, bi, 0, c, 0, 0))
    return pl.pallas_call(
        functools.partial(_gdn_intra_kernel, hb=hb, cb=cb),
        grid=(2, b, nc // cb),
        in_specs=[qkv, qkv, qkv,
                  pl.BlockSpec((None, None, hb, cb, SUBLANES, GDN_CHUNK), lambda d, bi, c: (d, bi, 0, c, 0, 0))],
        out_specs=[matb, matb, tokb, tokb],
        out_shape=[jax.ShapeDtypeStruct((2, b, h, nc, dk, dk), F32)] * 2
                  + [jax.ShapeDtypeStruct((2, b, h, tt, dk), F32)] * 2,
        compiler_params=_cparams("parallel", "parallel", "parallel"),
        name="gdn_intra",
    )(q, k, v, gr)


def _gdn_inter_kernel(af, bf, qf, of, ab, bb, qb, ob, o_f, o_b, s_sc):
    @pl.when(pl.program_id(1) == 0)
    def _():
        s_sc[...] = jnp.zeros(s_sc.shape, F32)

    kc = af.shape[1]
    c_len = GDN_CHUNK
    s_f, s_b = s_sc[0], s_sc[1]
    for i in range(kc):
        r = kc - 1 - i
        sf, sb = s_f.astype(BF16), s_b.astype(BF16)
        o_f[:, i * c_len:(i + 1) * c_len, :] = (
            _bmm(qf[:, i * c_len:(i + 1) * c_len, :].astype(BF16), sf) + of[:, i * c_len:(i + 1) * c_len, :])
        o_b[:, r * c_len:(r + 1) * c_len, :] = (
            _bmm(qb[:, r * c_len:(r + 1) * c_len, :].astype(BF16), sb) + ob[:, r * c_len:(r + 1) * c_len, :])
        s_f = _bmm(af[:, i].astype(BF16), sf) + bf[:, i]
        s_b = _bmm(ab[:, r].astype(BF16), sb) + bb[:, r]
    s_sc[0] = s_f
    s_sc[1] = s_b


def _gdn_inter(a, bm, qe, o0, nc_ctx):
    _, b, h, nc, dk, _ = a.shape
    kc = math.gcd(nc_ctx, nc - nc_ctx, 4)
    nb, nb_ctx = nc // kc, nc_ctx // kc
    rev = lambda c: jnp.where(c < nb_ctx, nb_ctx - 1 - c, nb - 1 - (c - nb_ctx))
    mat_f = pl.BlockSpec((None, None, h, kc, dk, dk), lambda bi, c: (0, bi, 0, c, 0, 0))
    mat_b = pl.BlockSpec((None, None, h, kc, dk, dk), lambda bi, c: (1, bi, 0, rev(c), 0, 0))
    tok_f = pl.BlockSpec((None, None, h, kc * GDN_CHUNK, dk), lambda bi, c: (0, bi, 0, c, 0))
    tok_b = pl.BlockSpec((None, None, h, kc * GDN_CHUNK, dk), lambda bi, c: (1, bi, 0, rev(c), 0))
    out_f = pl.BlockSpec((None, h, kc * GDN_CHUNK, dk), lambda bi, c: (bi, 0, c, 0))
    out_b = pl.BlockSpec((None, h, kc * GDN_CHUNK, dk), lambda bi, c: (bi, 0, rev(c), 0))
    osd = jax.ShapeDtypeStruct(qe.shape[1:], F32)
    return pl.pallas_call(
        _gdn_inter_kernel,
        grid=(b, nb),
        in_specs=[mat_f, mat_f, tok_f, tok_f, mat_b, mat_b, tok_b, tok_b],
        out_specs=[out_f, out_b],
        out_shape=[osd, osd],
        scratch_shapes=[pltpu.VMEM((2, h, dk, dk), F32)],
        compiler_params=_cparams("parallel", "arbitrary"),
        name="gdn_inter",
    )(a, bm, qe, o0, a, bm, qe, o0)


def _gdn_scan(prep, tc):
    q, k, v, s1, s2 = prep
    b, nh, tt, _ = q.shape
    s1t, s2t = s1.transpose(0, 2, 1), s2.transpose(0, 2, 1)
    nc = tt // GDN_CHUNK
    zero = jnp.zeros((b, nh, nc, GDN_CHUNK), F32)
    gr = jnp.stack([
        jnp.stack([x.reshape(b, nh, nc, GDN_CHUNK) for x in
                   (s1t[:, d * nh:(d + 1) * nh], s1t[:, (2 + d) * nh:(3 + d) * nh], s2t[:, d * nh:(d + 1) * nh])]
                  + [zero] * (SUBLANES - 3), axis=3)
        for d in range(2)])
    a, bm, qe, o0 = _gdn_intra(q, k, v, gr)
    return _gdn_inter(a, bm, qe, o0, tc // GDN_CHUNK)


def _mix_even_kernel(x_ref, mod_ref, ao_ref, gf_ref, gb_ref, gz_ref, on_ref, eseg_ref, wa_ref, wg_ref, out_ref):
    o = jnp.concatenate([gf_ref[hd] + gb_ref[hd] for hd in range(GDN_HEADS)], axis=-1)
    ms = _seg_sumsq(o, eseg_ref[...]) * (1.0 / GDN_DV)
    y = o * lax.rsqrt(ms + NORM_EPS) * on_ref[...] * _silu(gz_ref[...])
    mix = _dot(ao_ref[...], wa_ref[...]) + _dot(y.astype(BF16), wg_ref[...])
    out_ref[...] = x_ref[...] + mod_ref[2:3, :] * mix


def _mix_even(x, mod, ao, go_f, go_b, tile_off, gz, on, eseg, wa, wg):
    b, t, d = x.shape
    tm = TOKEN_TILE
    tok = lambda w: pl.BlockSpec((None, tm, w), lambda bi, i: (bi, i, 0))
    head = pl.BlockSpec((None, GDN_HEADS, tm, GDN_DV), lambda bi, i: (bi, 0, i + tile_off, 0))
    return pl.pallas_call(
        _mix_even_kernel,
        grid=(b, t // tm),
        in_specs=[tok(d), pl.BlockSpec((None, 6, d), lambda bi, i: (bi, 0, 0)), tok(ao.shape[-1]),
                  head, head, tok(gz.shape[-1]), _full(on), _full(eseg), _full(wa), _full(wg)],
        out_specs=tok(d),
        out_shape=jax.ShapeDtypeStruct(x.shape, F32),
        compiler_params=_cparams("parallel", "parallel"),
        name="mix_even",
    )(x, mod, ao, go_f, go_b, gz, on, eseg, wa, wg)


def _ffn_kernel(x_ref, mod_ref, g_ref, wg_ref, wu_ref, wd_ref, out_ref, h_sc, acc_sc):
    j = pl.program_id(2)

    @pl.when(j == 0)
    def _():
        h_sc[...] = _rms_mod(x_ref[...], g_ref[...], mod_ref[4:5, :], mod_ref[3:4, :]).astype(BF16)
        acc_sc[...] = jnp.zeros(acc_sc.shape, F32)

    h = h_sc[...]
    act = (_silu(_dot(h, wg_ref[...])) * _dot(h, wu_ref[...])).astype(BF16)
    acc_sc[...] += _dot(act, wd_ref[...])

    @pl.when(j == pl.num_programs(2) - 1)
    def _():
        out_ref[...] = x_ref[...] + mod_ref[5:6, :] * acc_sc[...]


def _ffn_tile(f):
    for cand in (1408, 1792, 1024, 896, 768, 512, 256, 128):
        if f % cand == 0:
            return cand
    raise ValueError(f"ffn width {f} is not a multiple of 128")


def _ffn(x, mod, g, w_gu, w_down):
    b, t, d = x.shape
    f = w_down.shape[0]
    tm = 512 if t % 512 == 0 else TOKEN_TILE
    tf = _ffn_tile(f)
    nf = f // tf
    tok = pl.BlockSpec((None, tm, d), lambda bi, i, j: (bi, i, 0))
    return pl.pallas_call(
        _ffn_kernel,
        grid=(b, t // tm, nf),
        in_specs=[tok, pl.BlockSpec((None, 6, d), lambda bi, i, j: (bi, 0, 0)),
                  pl.BlockSpec(g.shape, lambda bi, i, j: (0, 0)),
                  pl.BlockSpec((d, tf), lambda bi, i, j: (0, j)),
                  pl.BlockSpec((d, tf), lambda bi, i, j: (0, nf + j)),
                  pl.BlockSpec((tf, d), lambda bi, i, j: (j, 0))],
        out_specs=tok,
        out_shape=jax.ShapeDtypeStruct(x.shape, F32),
        scratch_shapes=[pltpu.VMEM((tm, d), BF16), pltpu.VMEM((tm, d), F32)],
        compiler_params=_cparams("parallel", "parallel", "arbitrary"),
        name="ffn",
    )(x, mod, g, w_gu, w_gu, w_down)


def _odd_proj_kernel(x_ref, mod_ref, g_ref, c_t, s_t, wdq, wdk, wdv, wgq, wgk, wgv, gqg, gkg, *rest, qscale):
    dq_out, dk_out, dv_out, gq_out, gk_out, gv_out = rest[-6:]
    h = _rms_mod(x_ref[...], g_ref[...], mod_ref[1:2, :], mod_ref[0:1, :]).astype(BF16)
    c, s = c_t[...], s_t[...]
    lane = lax.broadcasted_iota(jnp.int32, c.shape, 1)
    is_m1 = ((lane // 32) % 2) == 1
    dq = _dot(h, wdq[...])
    dk = _dot(h, wdk[...])
    for hd in range(DIFF_HEADS):
        sl = slice(hd * LANES, (hd + 1) * LANES)
        r = _rope(dq[:, sl], c, s) * qscale
        dq_out[:, 2 * hd * LANES:(2 * hd + 1) * LANES] = jnp.where(is_m1, 0.0, r).astype(BF16)
        dq_out[:, (2 * hd + 1) * LANES:(2 * hd + 2) * LANES] = jnp.where(is_m1, r, 0.0).astype(BF16)
        dk_out[:, sl] = _rope(dk[:, sl], c, s).astype(BF16)
    _store_vt(dv_out, _dot_t(wdv[...], h))
    gq = _dot(h, wgq[...])
    gqgv = gqg[...]
    inv_dh = 1.0 / GQA_DH
    for hd in range(GQA_HEADS):
        sl = slice(hd * LANES, (hd + 1) * LANES)
        xh = gq[:, sl]
        ms = jnp.sum(xh * xh, axis=-1, keepdims=True) * inv_dh
        xn = xh * lax.rsqrt(ms + NORM_EPS) * gqgv[:, sl]
        gq_out[:, sl] = (_rope(xn, c, s) * qscale).astype(BF16)
    gk = _dot(h, wgk[...])
    sq = gk * gk
    ms0 = jnp.sum(jnp.where(is_m1, 0.0, sq), axis=-1, keepdims=True) * inv_dh
    ms1 = jnp.sum(jnp.where(is_m1, sq, 0.0), axis=-1, keepdims=True) * inv_dh
    rs = jnp.where(is_m1, lax.rsqrt(ms1 + NORM_EPS), lax.rsqrt(ms0 + NORM_EPS))
    gk_out[...] = _rope(gk * rs * gkg[...], c, s).astype(BF16)
    _store_vt(gv_out, _dot_t(wgv[...], h))


def _odd_proj(x, mod, g, tabs, ws, gains, qscale, tt, tile_off, bufs=None):
    b, t, d = x.shape
    tm = TOKEN_TILE
    tok = lambda w: pl.BlockSpec((None, tm, w), lambda bi, i: (bi, i, 0))
    tab = pl.BlockSpec((tm, LANES), lambda bi, i: (i, 0))
    outs = ((2 * DIFF_HEADS * LANES, BF16, "own"), (DIFF_HEADS * LANES, BF16, "keys"), (0, BF16, DIFF_HEADS),
            (GQA_HEADS * LANES, BF16, "own"), (LANES, BF16, "keys"), (0, BF16, 1))
    specs, shapes, shared = _proj_outputs(b, t, tt, tile_off, outs)
    args = [x, mod, g, *tabs, *ws, *gains]
    in_specs = ([tok(d), pl.BlockSpec((None, 6, d), lambda bi, i: (bi, 0, 0)), _full(g), tab, tab]
                + [_full(w) for w in ws] + [_full(w) for w in gains])
    args, in_specs, aliases = _alias_shared(args, in_specs, shapes, shared, bufs)
    return pl.pallas_call(
        functools.partial(_odd_proj_kernel, qscale=qscale),
        grid=(b, t // tm),
        in_specs=in_specs,
        out_specs=specs,
        out_shape=shapes,
        input_output_aliases=aliases,
        compiler_params=_cparams("parallel", "parallel"),
        name="odd_proj",
    )(*args)


def _mix_odd_kernel(x_ref, mod_ref, do_ref, go_ref, lam_ref, dn_ref, wd_ref, wg_ref, out_ref, *, lambda_init):
    lp = lam_ref[...]
    lam = (jnp.exp(jnp.sum(lp[0:1, :] * lp[1:2, :], axis=-1, keepdims=True))
           - jnp.exp(jnp.sum(lp[2:3, :] * lp[3:4, :], axis=-1, keepdims=True)) + lambda_init)
    dn = dn_ref[...]
    parts = []
    for hd in range(DIFF_HEADS):
        d0 = do_ref[:, 2 * hd * LANES:(2 * hd + 1) * LANES]
        d1 = do_ref[:, (2 * hd + 1) * LANES:(2 * hd + 2) * LANES]
        dd = d0 - lam * d1
        ms = jnp.mean(dd * dd, axis=-1, keepdims=True)
        parts.append(((dd * lax.rsqrt(ms + NORM_EPS) * dn) * (1.0 - lambda_init)).astype(BF16))
    dcat = jnp.concatenate(parts, axis=1)
    mix = _dot(dcat, wd_ref[...]) + _dot(go_ref[...], wg_ref[...])
    out_ref[...] = x_ref[...] + mod_ref[2:3, :] * mix


def _mix_odd(x, mod, do, go, lam_p, dn, wd, wg, lambda_init):
    b, t, d = x.shape
    tm = TOKEN_TILE
    tok = lambda w: pl.BlockSpec((None, tm, w), lambda bi, i: (bi, i, 0))
    return pl.pallas_call(
        functools.partial(_mix_odd_kernel, lambda_init=lambda_init),
        grid=(b, t // tm),
        in_specs=[tok(d), pl.BlockSpec((None, 6, d), lambda bi, i: (bi, 0, 0)), tok(do.shape[-1]),
                  tok(go.shape[-1]), _full(lam_p), _full(dn), _full(wd), _full(wg)],
        out_specs=tok(d),
        out_shape=jax.ShapeDtypeStruct(x.shape, F32),
        compiler_params=_cparams("parallel", "parallel"),
        name="mix_odd",
    )(x, mod, do, go, lam_p, dn, wd, wg)


def _router_kernel(x_ref, mod_ref, g_ref, rw_ref, h_out, route_out):
    h = _rms_mod(x_ref[...], g_ref[...], mod_ref[4:5, :], mod_ref[3:4, :])
    h_out[...] = h
    h1, h2, h3 = _split3(h)
    w1, w2, w3 = _split3(rw_ref[...])
    logits = (_dot(h1, w1) + _dot(h1, w2) + _dot(h2, w1) + _dot(h2, w2) + _dot(h1, w3) + _dot(h3, w1))
    lane = lax.broadcasted_iota(jnp.int32, logits.shape, 1).astype(F32)
    neg = -jnp.inf
    l1 = jnp.where(lane < N_EXPERTS, logits, neg)
    m1 = jnp.max(l1, axis=-1, keepdims=True)
    i1 = jnp.min(jnp.where(l1 == m1, lane, float(LANES)), axis=-1, keepdims=True)
    l2 = jnp.where(lane == i1, neg, l1)
    m2 = jnp.max(l2, axis=-1, keepdims=True)
    i2 = jnp.min(jnp.where(l2 == m2, lane, float(LANES)), axis=-1, keepdims=True)
    e = jnp.exp(m2 - m1)
    p1 = 1.0 / (1.0 + e)
    p2 = e / (1.0 + e)
    route = jnp.where(lane == 0.0, i1,
                      jnp.where(lane == 1.0, i2, jnp.where(lane == 2.0, p1, jnp.where(lane == 3.0, p2, 0.0))))
    route_out[...] = route[:, 0:SUBLANES]


def _router(x, mod, g, rw):
    b, t, d = x.shape
    tm = TOKEN_TILE
    tok = lambda w: pl.BlockSpec((None, tm, w), lambda bi, i: (bi, i, 0))
    return pl.pallas_call(
        _router_kernel,
        grid=(b, t // tm),
        in_specs=[tok(d), pl.BlockSpec((None, 6, d), lambda bi, i: (bi, 0, 0)), _full(g), _full(rw)],
        out_specs=[tok(d), tok(SUBLANES)],
        out_shape=[jax.ShapeDtypeStruct(x.shape, F32), jax.ShapeDtypeStruct((b, t, SUBLANES), F32)],
        compiler_params=_cparams("parallel", "parallel"),
        name="moe_router",
    )(x, mod, g, rw)


def _row_copies(idx_ref, n, src_of, dst_of, sem):
    def body(r, carry):
        i = idx_ref[0, r]
        pltpu.make_async_copy(src_of(r, i), dst_of(r, i), sem).start()
        return carry
    lax.fori_loop(0, n, body, 0)


def _moe_kernel(be_ref, tok_ref, tok_next_ref, dst_ref, dst_prev_ref, h_hbm, wg_ref, wu_ref, wd_ref, y_hbm,
                xbuf, xb_sc, acc_sc, ybuf, gsem, ssem, *, nf):
    i, j = pl.program_id(0), pl.program_id(1)
    nb = pl.num_programs(0)
    blk = acc_sc.shape[0]
    per_step = blk // nf
    nxt = (i + 1) % 2
    gather_wait = lambda s: pltpu.make_async_copy(h_hbm.at[pl.ds(0, blk), :], xbuf.at[s], gsem.at[s]).wait()
    scatter_wait = lambda: pltpu.make_async_copy(ybuf, y_hbm.at[pl.ds(0, blk), :], ssem).wait()
    gather_row = lambda idx_ref, r, s: pltpu.make_async_copy(
        h_hbm.at[pl.ds(idx_ref[0, r], 1), :], xbuf.at[s, pl.ds(r, 1), :], gsem.at[s]).start()
    scatter_row = lambda idx_ref, r, priority=0: pltpu.make_async_copy(
        ybuf.at[pl.ds(r, 1), :], y_hbm.at[pl.ds(idx_ref[0, r], 1), :], ssem).start(priority=priority)

    @pl.when(j == 0)
    def _():
        @pl.when(i == 0)
        def _():
            ybuf[...] = jnp.zeros(ybuf.shape, F32)
            lax.fori_loop(0, blk, lambda r, c: (gather_row(tok_ref, r, 0), c)[1], 0)

        gather_wait(i % 2)
        xb_sc[...] = xbuf[i % 2].astype(BF16)
        acc_sc[...] = jnp.zeros(acc_sc.shape, F32)

    for r in range(per_step):
        gather_row(tok_next_ref, j * per_step + r, nxt)
        scatter_row(dst_prev_ref, j * per_step + r, priority=r % 2)

    x = xb_sc[...]
    act = (_silu(_dot(x, wg_ref[...])) * _dot(x, wu_ref[...])).astype(BF16)
    acc_sc[...] += _dot(act, wd_ref[...])

    @pl.when(j == nf - 1)
    def _():
        scatter_wait()
        ybuf[...] = acc_sc[...]

        @pl.when(i == nb - 1)
        def _():
            gather_wait(nxt)
            lax.fori_loop(0, blk, lambda r, c: (scatter_row(dst_ref, r), c)[1], 0)
            scatter_wait()


def _moe_experts(h, buf_tok, dest, block_e, w_gu, w_down):
    nb, blk = buf_tok.shape
    d = h.shape[1]
    f = w_down.shape[1]
    tf = _ffn_tile(f)
    nf = f // tf
    idx3 = lambda a: a.reshape(nb, 1, blk)
    smem = lambda imap: pl.BlockSpec((None, 1, blk), imap, memory_space=pltpu.SMEM)
    grid_spec = pltpu.PrefetchScalarGridSpec(
        num_scalar_prefetch=1,
        grid=(nb, nf),
        in_specs=[smem(lambda i, j, be: (i, 0, 0)),
                  smem(lambda i, j, be: (jnp.minimum(i + 1, nb - 1), 0, 0)),
                  smem(lambda i, j, be: (i, 0, 0)),
                  smem(lambda i, j, be: (jnp.maximum(i - 1, 0), 0, 0)),
                  pl.BlockSpec(memory_space=pl.ANY),
                  pl.BlockSpec((None, d, tf), lambda i, j, be: (be[i], 0, j)),
                  pl.BlockSpec((None, d, tf), lambda i, j, be: (be[i], 0, nf + j)),
                  pl.BlockSpec((None, tf, d), lambda i, j, be: (be[i], j, 0))],
        out_specs=pl.BlockSpec(memory_space=pl.ANY),
        scratch_shapes=[pltpu.VMEM((2, blk, d), F32), pltpu.VMEM((blk, d), BF16), pltpu.VMEM((blk, d), F32),
                        pltpu.VMEM((blk, d), F32), pltpu.SemaphoreType.DMA((2,)), pltpu.SemaphoreType.DMA(())],
    )
    return pl.pallas_call(
        functools.partial(_moe_kernel, nf=nf),
        grid_spec=grid_spec,
        out_shape=jax.ShapeDtypeStruct((nb * blk, d), F32),
        compiler_params=_cparams("arbitrary", "arbitrary"),
        name="moe_experts",
    )(block_e, idx3(buf_tok), idx3(buf_tok), idx3(dest), idx3(dest), h, w_gu, w_gu, w_down)


def _final_kernel(x_ref, mod_ref, rt_ref, fn_ref, y0_ref, y1_ref, out_ref):
    rt = rt_ref[...]
    y = rt[:, 2:3] * y0_ref[...] + rt[:, 3:4] * y1_ref[...]
    xo = x_ref[...] + mod_ref[5:6, :] * y
    ms = jnp.mean(xo * xo, axis=-1, keepdims=True)
    out_ref[...] = xo * lax.rsqrt(ms + NORM_EPS) * fn_ref[...]


def _moe_combine_final(x, mod, route, y, fn):
    b, t, d = x.shape
    tm = TOKEN_TILE
    nt = t // tm
    tok = lambda w: pl.BlockSpec((None, tm, w), lambda bi, i: (bi, i, 0))
    choice = lambda c: pl.BlockSpec((tm, d), lambda bi, i: (c * b * nt + bi * nt + i, 0))
    return pl.pallas_call(
        _final_kernel,
        grid=(b, nt),
        in_specs=[tok(d), pl.BlockSpec((None, 6, d), lambda bi, i: (bi, 0, 0)), tok(route.shape[-1]),
                  _full(fn), choice(0), choice(1)],
        out_specs=tok(d),
        out_shape=jax.ShapeDtypeStruct(x.shape, F32),
        compiler_params=_cparams("parallel", "parallel"),
        name="moe_combine_final",
    )(x, mod, route, fn, y, y)


def _moe_plan(top_e, n_tok):
    nk = n_tok * TOP_K
    flat_e = top_e.reshape(-1)
    onehot = (flat_e[:, None] == jnp.arange(N_EXPERTS, dtype=jnp.int32)[None, :]).astype(jnp.int32)
    rank = jnp.sum((jnp.cumsum(onehot, axis=0) - onehot) * onehot, axis=1)
    counts = jnp.sum(onehot, axis=0)
    padded = (counts + MOE_BLOCK - 1) // MOE_BLOCK * MOE_BLOCK
    pad_end = jnp.cumsum(padded)
    pad_start = pad_end - padded
    slot = (pad_start[flat_e] + rank).astype(jnp.int32)
    n_blocks = -(-nk // MOE_BLOCK) + N_EXPERTS
    rows = n_blocks * MOE_BLOCK
    assign = jnp.full((rows,), -1, jnp.int32).at[slot].set(jnp.arange(nk, dtype=jnp.int32))
    is_pad = assign < 0
    pad_rank = jnp.cumsum(is_pad.astype(jnp.int32)) - 1
    buf_tok = jnp.where(is_pad, 0, assign // TOP_K)
    dest = jnp.where(is_pad, nk + pad_rank, (assign % TOP_K) * n_tok + assign // TOP_K)
    block_e = jnp.minimum(
        jnp.searchsorted(pad_end, jnp.arange(n_blocks, dtype=jnp.int32) * MOE_BLOCK, side="right"),
        N_EXPERTS - 1).astype(jnp.int32)
    return buf_tok.reshape(n_blocks, MOE_BLOCK), dest.reshape(n_blocks, MOE_BLOCK), block_e


def _prep_even(w_in, w_uq, w_ukv, w_out):
    src, _, _, _ = _mla_layout()
    o = np.cumsum((0, MLA_Q_RANK, MLA_KV_RANK, MLA_ROPE, GDN_QKV, GDN_HEADS * GDN_DV, 2 * GDN_HEADS,
                   2 * GDN_HEADS))
    wcq, wckv = w_in[:, o[0]:o[1]], w_in[:, o[1]:o[2]]
    wkpe = _take_cols(w_in[:, o[2]:o[3]], np.where(src >= MLA_NOPE, src - MLA_NOPE, -1))
    wgqkv, wgz, wab = w_in[:, o[3]:o[4]], w_in[:, o[4]:o[5]], w_in[:, o[5]:o[7]]
    dq = MLA_NOPE + MLA_ROPE
    uq_idx = np.concatenate([np.where(src >= 0, h * dq + src, -1) for h in range(MLA_HEADS)])
    dkv = MLA_NOPE + MLA_V
    uk_idx = np.concatenate([np.where((src >= 0) & (src < MLA_NOPE), h * dkv + src, -1)
                             for h in range(MLA_HEADS)])
    uv_idx = np.concatenate([h * dkv + MLA_NOPE + np.arange(MLA_V) for h in range(MLA_HEADS)])
    wuq, wuk, wuv = _take_cols(w_uq, uq_idx), _take_cols(w_ukv, uk_idx), _take_cols(w_ukv, uv_idx)
    rows = -np.ones(MLA_HEADS * LANES, np.int64)
    for h in range(MLA_HEADS):
        base = h * LANES + (h % 2) * MLA_V
        rows[base:base + MLA_V] = h * MLA_V + np.arange(MLA_V)
    n_mla = MLA_HEADS * MLA_V
    wa = _take_cols(w_out[:n_mla].T, rows).T
    wg = w_out[n_mla:]
    ws = [w.astype(BF16) for w in (wcq, wckv, wkpe, wgqkv, wgz, wab)]
    return ws, [wuq.astype(BF16), wuk.astype(BF16), wuv.T.astype(BF16)], wa.astype(BF16), wg.astype(BF16)


def _prep_odd(w_in, q_norm, k_norm, w_out):
    m, src, _, _, _ = _pair_layout()
    nd = DIFF_HEADS * 2 * DIFF_DH
    o_dq, o_dk, o_dv, o_gq = 0, nd, 2 * nd, 3 * nd
    o_gk = o_gq + GQA_HEADS * GQA_DH
    o_gv = o_gk + GQA_KV_HEADS * GQA_DH
    pair = np.concatenate([h * 2 * DIFF_DH + m * DIFF_DH + src for h in range(DIFF_HEADS)])
    wdq = _take_cols(w_in, o_dq + pair)
    wdk = _take_cols(w_in, o_dk + pair)
    wdv = w_in[:, o_dv:o_gq].T
    grp = GQA_HEADS // GQA_KV_HEADS
    gq_idx = np.concatenate([np.where(m == h // grp, o_gq + h * GQA_DH + src, -1) for h in range(GQA_HEADS)])
    wgq = _take_cols(w_in, gq_idx)
    wgk = _take_cols(w_in, o_gk + m * GQA_DH + src)
    wgv = w_in[:, o_gv:o_gv + GQA_KV_HEADS * GQA_DH].T
    gqg = jnp.tile(q_norm[src], GQA_HEADS).reshape(1, -1)
    gkg = k_norm[src].reshape(1, -1)
    rows = -np.ones(GQA_HEADS * LANES, np.int64)
    for h in range(GQA_HEADS):
        base = h * LANES + (h // grp) * GQA_DH
        rows[base:base + GQA_DH] = h * GQA_DH + np.arange(GQA_DH)
    wd = w_out[:nd]
    wg = _take_cols(w_out[nd:].T, rows).T
    ws = [w.astype(BF16) for w in (wdq, wdk, wdv, wgq, wgk, wgv)]
    return ws, [gqg.astype(F32), gkg.astype(F32)], wd.astype(BF16), wg.astype(BF16)


def _layer_mods(m_layer, batch):
    d = m_layer.shape[1] // 6
    lat = m_layer[:batch].reshape(batch, 6, d)
    ctx = jnp.broadcast_to(m_layer[batch].reshape(1, 6, d), (batch, 6, d))
    return lat, ctx


def kernel(x, c, ctx, c_ctx, mod_w, mod_b, norm_g, ev_w_in, ev_mla_q_norm, ev_mla_kv_norm, ev_mla_w_uq,
           ev_mla_w_ukv, ev_gdn_conv, ev_gdn_a_log, ev_gdn_dt_bias, ev_gdn_out_norm, ev_w_out, ev_ffn_w_gu,
           ev_ffn_w_down, od_w_in, od_diff_lambda, od_diff_norm, od_gqa_q_norm, od_gqa_k_norm, od_w_out,
           od_router_w, od_moe_w_gu, od_moe_w_down, final_norm):
    batch, t_lat, d = x.shape
    t_ctx = ctx.shape[1]
    depth = mod_w.shape[0]
    assert depth == 2 and batch < 16

    cs = jnp.zeros((16, d), F32).at[:batch].set(c).at[batch].set(c_ctx)
    mods = _modulation(cs, mod_w, mod_b)

    j = 0
    mod_lat, mod_ctx = _layer_mods(mods[0], batch)
    g1 = norm_g[0, 0].reshape(1, d)
    g2 = norm_g[0, 1].reshape(1, d)
    ws, ups, wa, wg = _prep_even(ev_w_in[j], ev_mla_w_uq[j], ev_mla_w_ukv[j], ev_w_out[j])
    ws = ws + [ev_mla_q_norm[j].reshape(1, -1), ev_mla_kv_norm[j].reshape(1, -1)] + ups
    _, kind, freq, sign = _mla_layout()
    qscale = (MLA_NOPE + MLA_ROPE) ** -0.5 * LOG2E
    nfq = MLA_ROPE // 4
    tabs_lat = (_rope_tables(t_lat, kind, freq, sign, nfq, qscale, True)
                + _rope_tables(t_lat, kind, freq, sign, nfq, 1.0, True))
    tabs_ctx = (_rope_tables(t_ctx, kind, freq, sign, nfq, qscale, False)
                + _rope_tables(t_ctx, kind, freq, sign, nfq, 1.0, False))
    t_all = t_ctx + t_lat
    ctx_tiles = t_ctx // TOKEN_TILE
    qc, k_all, v_all, gqkv_c, gz_c, ab_c = _even_proj(ctx, mod_ctx, g1, tabs_ctx, ws, t_all, 0)
    ql, k_all, v_all, gqkv_l, gz_l, ab_l = _even_proj(x, mod_lat, g1, tabs_lat, ws, t_all, ctx_tiles,
                                                      bufs=(k_all, v_all))

    same = lambda hg: hg
    mla_kw = dict(groups=MLA_HEADS, g=1, kmap=same, vmap=lambda hg: hg // 2, out_dtype=BF16)
    ao_l = _flash(ql, k_all, v_all, **mla_kw)
    ao_c = _flash(qc, k_all, v_all, n_keys=t_ctx, **mla_kw)

    nh2 = 2 * GDN_HEADS
    prm = jnp.zeros((2, 2 * nh2), F32)
    prm = prm.at[0, :nh2].set(ev_gdn_a_log[j].reshape(-1)).at[1, :nh2].set(ev_gdn_dt_bias[j].reshape(-1))
    eseg = _seg_ones(GDN_HEADS * GDN_DK, GDN_DK)
    prep = _gdn_prep(gqkv_c, ab_c, ev_gdn_conv[j], prm, eseg, t_all, 0)
    prep = _gdn_prep(gqkv_l, ab_l, ev_gdn_conv[j], prm, eseg, t_all, ctx_tiles, bufs=prep)
    go_f, go_b = _gdn_scan(prep, t_ctx)

    on = jnp.tile(ev_gdn_out_norm[j], GDN_HEADS).reshape(1, -1)
    x = _mix_even(x, mod_lat, ao_l, go_f, go_b, ctx_tiles, gz_l, on, eseg, wa, wg)
    ctx = _mix_even(ctx, mod_ctx, ao_c, go_f, go_b, 0, gz_c, on, eseg, wa, wg)
    w_gu, w_dn = ev_ffn_w_gu[j].astype(BF16), ev_ffn_w_down[j].astype(BF16)
    x = _ffn(x, mod_lat, g2, w_gu, w_dn)
    ctx = _ffn(ctx, mod_ctx, g2, w_gu, w_dn)

    lambda_init = 0.8 - 0.6 * math.exp(-0.3 * 1)
    mod_lat, mod_ctx = _layer_mods(mods[1], batch)
    g1 = norm_g[1, 0].reshape(1, d)
    g2 = norm_g[1, 1].reshape(1, d)
    ws, gains, wd, wgx = _prep_odd(od_w_in[j], od_gqa_q_norm[j], od_gqa_k_norm[j], od_w_out[j])
    _, _, kind, freq, sign = _pair_layout()
    nfq = DIFF_DH // 4
    tabs_lat = _rope_tables(t_lat, kind, freq, sign, nfq, 1.0, True)
    tabs_ctx = _rope_tables(t_ctx, kind, freq, sign, nfq, 1.0, False)
    qscale = DIFF_DH ** -0.5 * LOG2E
    _, dk, dvt, _, gk, gvt = _odd_proj(ctx, mod_ctx, g1, tabs_ctx, ws, gains, qscale, t_all, 0)
    dq, dk, dvt, gq, gk, gvt = _odd_proj(x, mod_lat, g1, tabs_lat, ws, gains, qscale, t_all, ctx_tiles,
                                         bufs=(dk, dvt, gk, gvt))
    do = _flash(dq, dk, dvt, groups=DIFF_HEADS, g=2, kmap=same, vmap=same, out_dtype=F32)
    zero = lambda hg: 0
    go = _flash(gq, gk, gvt, groups=1, g=GQA_HEADS, kmap=zero, vmap=zero, out_dtype=BF16)
    x = _mix_odd(x, mod_lat, do, go, od_diff_lambda[j], od_diff_norm[j].reshape(1, -1), wd, wgx, lambda_init)

    rw = jnp.zeros((d, LANES), F32).at[:, :N_EXPERTS].set(od_router_w[j])
    h2, route = _router(x, mod_lat, g2, rw)
    n_tok = batch * t_lat
    top_e = route[..., 0:TOP_K].astype(jnp.int32).reshape(n_tok, TOP_K)
    buf_tok, dest, block_e = _moe_plan(top_e, n_tok)
    y = _moe_experts(h2.reshape(n_tok, d), buf_tok, dest, block_e, od_moe_w_gu[j].astype(BF16),
                     od_moe_w_down[j].astype(BF16))
    return _moe_combine_final(x, mod_lat, route, y, final_norm.reshape(1, d))
```
